```python
import math
import jax, jax.numpy as jnp
from jax import lax
import numpy as np

D_MODEL = 4096
BATCH = 4
SEQ = 2048
DEPTH = 1

HEAD_DIM = 128
FOX_HEADS = 16
DSA_HEADS = 16
FOX_WIDTH = FOX_HEADS * HEAD_DIM
DSA_WIDTH = DSA_HEADS * HEAD_DIM
MIX_WIDTH = FOX_WIDTH + DSA_WIDTH
Q_LORA = 1536
KV_LORA = 512
IDX_HEADS = 32
IDX_DIM = 128
TOPK_MAX = 256
Q_BLOCK = 128
N_BUCKETS = 32
MAX_DISTANCE = 128
N_EXPERTS = 32
TOP_K = 4
D_EXPERT = 1536
SWIGLU_LIMIT = 7.0
SWIGLU_ALPHA = 1.702
EXPERT_BLOCK = 128
ALPHA_RES = (2 * DEPTH) ** 0.25
BETA_INIT = (8 * DEPTH) ** -0.25
_IN_SIZES = (FOX_WIDTH, FOX_WIDTH, FOX_WIDTH, FOX_HEADS, Q_LORA, KV_LORA, IDX_DIM, IDX_HEADS)
IN_COLS = sum(_IN_SIZES)
IN_SPLITS = tuple(int(v) for v in np.cumsum(_IN_SIZES)[:-1])

kernel_name = 'hybrid_fox_dsa_moe_block'


def layer_norm(x, g, b, eps=1e-5):
    x32 = x.astype(jnp.float32)
    mu = jnp.mean(x32, axis=-1, keepdims=True)
    var = jnp.mean(jnp.square(x32 - mu), axis=-1, keepdims=True)
    y = (x32 - mu) * lax.rsqrt(var + eps) * g.astype(jnp.float32) + b.astype(jnp.float32)
    return y.astype(x.dtype)


def rms_norm(x, g, eps=1e-6):
    x32 = x.astype(jnp.float32)
    y = x32 * lax.rsqrt(jnp.mean(jnp.square(x32), axis=-1, keepdims=True) + eps) * g.astype(jnp.float32)
    return y.astype(x.dtype)


def to_blocks(a):
    b, s = a.shape[:2]
    a = a.reshape((b, s // Q_BLOCK, Q_BLOCK) + a.shape[2:])
    return jnp.moveaxis(a, 1, 0)


def from_blocks(a):
    a = jnp.moveaxis(a, 0, 1)
    return a.reshape((a.shape[0], a.shape[1] * a.shape[2]) + a.shape[3:])


def t5_bucket(dist):
    max_exact = N_BUCKETS // 2
    d = jnp.maximum(dist.astype(jnp.float32), 1.0)
    large = max_exact + (jnp.log(d / max_exact) / math.log(MAX_DISTANCE / max_exact)
                         * (N_BUCKETS - max_exact)).astype(jnp.int32)
    large = jnp.minimum(large, N_BUCKETS - 1)
    return jnp.where(dist < max_exact, dist, large)


def fox_attention(q, k, v, log_f):
    seq = q.shape[1]
    cum = jnp.cumsum(log_f, axis=1)
    cum_keys = jnp.swapaxes(cum, 1, 2)
    pos_k = jnp.arange(seq)
    scale = HEAD_DIM ** -0.5

    def block(args):
        i, q_i, cum_i = args
        pos_q = i * Q_BLOCK + jnp.arange(Q_BLOCK)
        s = jnp.einsum('bthd,bshd->bhts', q_i, k).astype(jnp.float32) * scale
        s = s + jnp.swapaxes(cum_i, 1, 2)[..., None] - cum_keys[:, :, None, :]
        s = jnp.where(pos_k[None, :] <= pos_q[:, None], s, -jnp.inf)
        p = jax.nn.softmax(s, axis=-1).astype(v.dtype)
        return jnp.einsum('bhts,bshd->bthd', p, v)

    nb = seq // Q_BLOCK
    out = lax.map(block, (jnp.arange(nb), to_blocks(q), to_blocks(cum)))
    return from_blocks(out)


def dsa_attention(q, c_kv, q_idx, k_idx, w_idx, w_uk, w_uv, rel_bias, k_sel):
    seq = q.shape[1]
    pos_k = jnp.arange(seq)
    scale = HEAD_DIM ** -0.5

    def block(args):
        i, q_i, qi_i, wi_i = args
        pos_q = i * Q_BLOCK + jnp.arange(Q_BLOCK)
        causal = pos_k[None, :] <= pos_q[:, None]
        rel = jax.nn.relu(jnp.einsum('bthd,bsd->bths', qi_i, k_idx).astype(jnp.float32))
        score = jnp.einsum('bths,bth->bts', rel, wi_i.astype(jnp.float32))
        score = jnp.where(causal[None], score, -jnp.inf)
        _, idx = lax.top_k(score, k_sel)
        valid = idx <= pos_q[None, :, None]
        ckv_sel = jax.vmap(lambda c_b, j_b: c_b[j_b])(c_kv, idx)
        q_abs = jnp.einsum('bthd,rhd->bthr', q_i, w_uk)
        s = jnp.einsum('bthr,btkr->bhtk', q_abs, ckv_sel).astype(jnp.float32) * scale
        dist = jnp.maximum(pos_q[None, :, None] - idx, 0)
        bias = rel_bias[t5_bucket(dist)].astype(jnp.float32)
        s = s + jnp.transpose(bias, (0, 3, 1, 2))
        s = jnp.where(valid[:, None], s, -jnp.inf)
        p = jax.nn.softmax(s, axis=-1).astype(c_kv.dtype)
        lat = jnp.einsum('bhtk,btkr->bthr', p, ckv_sel)
        return jnp.einsum('bthr,rhd->bthd', lat, w_uv)

    nb = seq // Q_BLOCK
    out = lax.map(block, (jnp.arange(nb), to_blocks(q), to_blocks(q_idx), to_blocks(w_idx)))
    return from_blocks(out)


def hybrid_mixer(u, w_in, b_forget, q_norm_g, kv_norm_g, kidx_ln_g, kidx_ln_b,
                 w_uq, w_uk, w_uv, w_iq, fox_out_g, dsa_out_g, w_o, rel_bias):
    bsz, seq, _ = u.shape
    proj = u @ w_in
    q_f, k_f, v_f, f_log, c_q, c_kv, k_idx, w_idx = jnp.split(proj, IN_SPLITS, axis=-1)
    log_f = jax.nn.log_sigmoid((f_log + b_forget).astype(jnp.float32))
    shp = (bsz, seq, FOX_HEADS, HEAD_DIM)
    fox = fox_attention(q_f.reshape(shp), k_f.reshape(shp), v_f.reshape(shp), log_f)
    fox = fox.reshape(bsz, seq, FOX_WIDTH)
    c_q = rms_norm(c_q, q_norm_g)
    c_kv = rms_norm(c_kv, kv_norm_g)
    q_d = (c_q @ w_uq).reshape(bsz, seq, DSA_HEADS, HEAD_DIM)
    q_i = (c_q @ w_iq).reshape(bsz, seq, IDX_HEADS, IDX_DIM)
    k_i = layer_norm(k_idx, kidx_ln_g, kidx_ln_b)
    w_i = w_idx * (IDX_HEADS ** -0.5 * IDX_DIM ** -0.5)
    k_sel = min(TOPK_MAX, seq // 4)
    dsa = dsa_attention(q_d, c_kv, q_i, k_i, w_i, w_uk, w_uv, rel_bias, k_sel)
    dsa = dsa.reshape(bsz, seq, DSA_WIDTH)
    merged = jnp.concatenate([rms_norm(fox, fox_out_g), rms_norm(dsa, dsa_out_g)], axis=-1)
    return merged @ w_o


def clamped_swiglu(h):
    gate, up = h[..., :D_EXPERT], h[..., D_EXPERT:]
    gate = jnp.minimum(gate, SWIGLU_LIMIT)
    up = jnp.clip(up, -SWIGLU_LIMIT, SWIGLU_LIMIT)
    return (up + 1.0) * gate * jax.nn.sigmoid(SWIGLU_ALPHA * gate)


def moe_ffn(u, w_router, b_router, w1, b1, w2, b2):
    bsz, seq, d = u.shape
    n_tok = bsz * seq
    xf = u.reshape(n_tok, d)
    logits = (xf @ w_router + b_router).astype(jnp.float32)
    top_val, top_idx = lax.top_k(logits, TOP_K)
    gates = jax.nn.softmax(top_val, axis=-1)
    n_assign = n_tok * TOP_K
    e_flat = top_idx.reshape(n_assign)
    tok_flat = jnp.repeat(jnp.arange(n_tok, dtype=jnp.int32), TOP_K)
    g_flat = gates.reshape(n_assign)
    order = jnp.argsort(e_flat)
    e_s, tok_s, g_s = e_flat[order], tok_flat[order], g_flat[order]
    counts = jnp.zeros((N_EXPERTS,), jnp.int32).at[e_flat].add(1)
    starts = jnp.cumsum(counts) - counts
    padded = (counts + EXPERT_BLOCK - 1) // EXPERT_BLOCK * EXPERT_BLOCK
    pad_ends = jnp.cumsum(padded)
    pad_starts = pad_ends - padded
    dest = pad_starts[e_s] + (jnp.arange(n_assign, dtype=jnp.int32) - starts[e_s])
    n_blocks = -(-n_assign // EXPERT_BLOCK) + N_EXPERTS
    n_slots = n_blocks * EXPERT_BLOCK
    slot_tok = jnp.full((n_slots,), n_tok, jnp.int32).at[dest].set(tok_s)
    slot_gate = jnp.zeros((n_slots,), jnp.float32).at[dest].set(g_s)
    blk_start = jnp.arange(n_blocks, dtype=jnp.int32) * EXPERT_BLOCK
    blk_expert = jnp.minimum(jnp.searchsorted(pad_ends, blk_start, side='right'), N_EXPERTS - 1)
    x_pad = jnp.concatenate([xf, jnp.zeros((1, d), xf.dtype)], axis=0)

    def block(args):
        toks, e = args
        h = x_pad[toks] @ w1[e] + b1[e]
        return clamped_swiglu(h) @ w2[e] + b2[e]

    out = lax.map(block, (slot_tok.reshape(n_blocks, EXPERT_BLOCK), blk_expert))
    contrib = out.reshape(n_slots, d) * slot_gate[:, None].astype(out.dtype)
    y = jnp.zeros((n_tok + 1, d), out.dtype).at[slot_tok].add(contrib)
    return y[:n_tok].reshape(bsz, seq, d)


def setup_inputs(seed: int = 0) -> dict:
    key = jax.random.key(seed)
    ks = jax.random.split(key, 32)
    L, D = DEPTH, D_MODEL

    def nrm(k, shape, std):
        return jax.random.normal(k, shape, jnp.float32) * std

    def gain(k, shape):
        return 1.0 + nrm(k, shape, 0.02)

    col_scale = jnp.ones((IN_COLS,), jnp.float32).at[2 * FOX_WIDTH:3 * FOX_WIDTH].set(BETA_INIT)
    return {
        'x': nrm(ks[0], (BATCH, SEQ, D), 1.0),
        'c': nrm(ks[1], (BATCH, D), 1.0),
        'w_ada': nrm(ks[2], (L, D, 6 * D), 0.5 * D ** -0.5),
        'b_ada': nrm(ks[3], (L, 6 * D), 0.02),
        'w_in': nrm(ks[4], (L, D, IN_COLS), D ** -0.5) * col_scale,
        'b_forget': jax.random.uniform(ks[5], (L, FOX_HEADS), jnp.float32, 1.0, 6.0),
        'q_norm_g': gain(ks[6], (L, Q_LORA)),
        'kv_norm_g': gain(ks[7], (L, KV_LORA)),
        'kidx_ln_g': gain(ks[8], (L, IDX_DIM)),
        'kidx_ln_b': nrm(ks[9], (L, IDX_DIM), 0.02),
        'w_uq': nrm(ks[10], (L, Q_LORA, DSA_WIDTH), Q_LORA ** -0.5),
        'w_uk': nrm(ks[11], (L, KV_LORA, DSA_HEADS, HEAD_DIM), KV_LORA ** -0.5),
        'w_uv': nrm(ks[12], (L, KV_LORA, DSA_HEADS, HEAD_DIM), BETA_INIT * KV_LORA ** -0.5),
        'w_iq': nrm(ks[13], (L, Q_LORA, IDX_HEADS * IDX_DIM), Q_LORA ** -0.5),
        'fox_out_g': gain(ks[14], (L, FOX_WIDTH)),
        'dsa_out_g': gain(ks[15], (L, DSA_WIDTH)),
        'w_o': nrm(ks[16], (L, MIX_WIDTH, D), BETA_INIT * MIX_WIDTH ** -0.5),
        'ln1_g': gain(ks[17], (L, D)),
        'ln1_b': nrm(ks[18], (L, D), 0.02),
        'w_router': nrm(ks[19], (L, D, N_EXPERTS), D ** -0.5),
        'b_router': nrm(ks[20], (L, N_EXPERTS), 0.01),
        'w1': nrm(ks[21], (L, N_EXPERTS, D, 2 * D_EXPERT), BETA_INIT * D ** -0.5),
        'b1': nrm(ks[22], (L, N_EXPERTS, 2 * D_EXPERT), 0.02),
        'w2': nrm(ks[23], (L, N_EXPERTS, D_EXPERT, D), BETA_INIT * D_EXPERT ** -0.5),
        'b2': nrm(ks[24], (L, N_EXPERTS, D), 0.02),
        'ln2_g': gain(ks[25], (L, D)),
        'ln2_b': nrm(ks[26], (L, D), 0.02),
        'rel_bias': nrm(ks[27], (N_BUCKETS, DSA_HEADS), 0.5),
    }


def reference(x, c, w_ada, b_ada, w_in, b_forget, q_norm_g, kv_norm_g, kidx_ln_g, kidx_ln_b,
              w_uq, w_uk, w_uv, w_iq, fox_out_g, dsa_out_g, w_o, ln1_g, ln1_b,
              w_router, b_router, w1, b1, w2, b2, ln2_g, ln2_b, rel_bias):
    for l in range(DEPTH):
        mod = jax.nn.silu(c) @ w_ada[l] + b_ada[l]
        sh_a, sc_a, g_a, sh_m, sc_m, g_m = jnp.split(mod[:, None, :], 6, axis=-1)
        u = x * (1.0 + sc_a) + sh_a
        mix = hybrid_mixer(u, w_in[l], b_forget[l], q_norm_g[l], kv_norm_g[l], kidx_ln_g[l],
                           kidx_ln_b[l], w_uq[l], w_uk[l], w_uv[l], w_iq[l], fox_out_g[l],
                           dsa_out_g[l], w_o[l], rel_bias)
        x = layer_norm(ALPHA_RES * x + g_a * mix, ln1_g[l], ln1_b[l])
        u = x * (1.0 + sc_m) + sh_m
        ffn = moe_ffn(u, w_router[l], b_router[l], w1[l], b1[l], w2[l], b2[l])
        x = layer_norm(ALPHA_RES * x + g_m * ffn, ln2_g[l], ln2_b[l])
    return x
```

```python
import functools
import math

import numpy as np
import jax
import jax.numpy as jnp
from jax import lax
from jax.experimental import pallas as pl
from jax.experimental.pallas import tpu as pltpu

F32 = jnp.float32
BF16 = jnp.bfloat16
I32 = jnp.int32

LANES = 128
HEAD_DIM = 128
IDX_DIM = 128
TOPK_MAX = 256
N_BUCKETS = 32
MAX_DISTANCE = 128
TOP_K = 4
SWIGLU_LIMIT = 7.0
SWIGLU_ALPHA = 1.702
NEG_BIG = -1e30
VMEM_LIMIT = 56 * 1024 * 1024
INT_MIN = -2 ** 31


def _tile(n, pref, unit=LANES):
    if n <= pref:
        return n
    t = (pref // unit) * unit
    while t > unit and n % t:
        t -= unit
    assert n % t == 0, (n, pref)
    return t


def _params(sem, vmem=VMEM_LIMIT):
    return pltpu.CompilerParams(dimension_semantics=sem, vmem_limit_bytes=vmem)


def _mm_body(*refs, nk, nk1, a_silu, has_bias):
    refs = list(refs)
    a_ref = refs.pop(0)
    a2_ref = refs.pop(0) if nk1 < nk else None
    w_ref = refs.pop(0)
    b_ref = refs.pop(0) if has_bias else None
    o_ref, acc_ref = refs
    k = pl.program_id(2)

    @pl.when(k == 0)
    def _():
        acc_ref[...] = jnp.zeros_like(acc_ref)

    def accumulate(src_ref):
        a = src_ref[...]
        if a_silu:
            a = a.astype(F32)
            a = a * jax.nn.sigmoid(a)
        acc_ref[...] += jnp.dot(a.astype(BF16), w_ref[...].astype(BF16),
                                preferred_element_type=F32)

    if a2_ref is None:
        accumulate(a_ref)
    else:
        pl.when(k < nk1)(lambda: accumulate(a_ref))
        pl.when(k >= nk1)(lambda: accumulate(a2_ref))

    @pl.when(k == nk - 1)
    def _():
        r = acc_ref[...]
        if has_bias:
            r = r + b_ref[...]
        o_ref[...] = r.astype(o_ref.dtype)


def _matmul(a, w, n_out, out_dtype, *, a2=None, bias=None, a_silu=False, tm=1024, tn=1024, tk=1024,
            name="matmul"):
    m, kdim = a.shape
    tm = _tile(m, tm, 16)
    tn = _tile(n_out, tn)
    tk = _tile(kdim, tk)
    nk1 = kdim // tk
    nk = nk1 if a2 is None else nk1 + a2.shape[1] // tk
    in_specs = [pl.BlockSpec((tm, tk), lambda i, j, k: (i, jnp.minimum(k, nk1 - 1)))]
    args = [a]
    if a2 is not None:
        assert a2.shape[1] % tk == 0
        in_specs.append(pl.BlockSpec((tm, tk), lambda i, j, k: (i, jnp.maximum(k - nk1, 0))))
        args.append(a2)
    in_specs.append(pl.BlockSpec((tk, tn), lambda i, j, k: (k, j)))
    args.append(w)
    if bias is not None:
        in_specs.append(pl.BlockSpec((1, tn), lambda i, j, k: (0, j)))
        args.append(bias)
    return pl.pallas_call(
        functools.partial(_mm_body, nk=nk, nk1=nk1, a_silu=a_silu, has_bias=bias is not None),
        grid=(m // tm, n_out // tn, nk),
        in_specs=in_specs,
        out_specs=pl.BlockSpec((tm, tn), lambda i, j, k: (i, j)),
        out_shape=jax.ShapeDtypeStruct((m, n_out), out_dtype),
        scratch_shapes=[pltpu.VMEM((tm, tn), F32)],
        compiler_params=_params(("parallel", "parallel", "arbitrary")),
        name=name,
    )(*args)


def _modulate_body(x_ref, sh_ref, sc_ref, o_ref):
    o_ref[0] = (x_ref[0] * (1.0 + sc_ref[0]) + sh_ref[0]).astype(o_ref.dtype)


def _modulate(x, mod3, shift_chunk, scale_chunk):
    b, s, d = x.shape
    ts = _tile(s, 512, 16)
    return pl.pallas_call(
        _modulate_body,
        grid=(b, s // ts),
        in_specs=[pl.BlockSpec((1, ts, d), lambda i, j: (i, j, 0)),
                  pl.BlockSpec((1, 1, d), lambda i, j: (i, 0, shift_chunk)),
                  pl.BlockSpec((1, 1, d), lambda i, j: (i, 0, scale_chunk))],
        out_specs=pl.BlockSpec((1, ts, d), lambda i, j: (i, j, 0)),
        out_shape=jax.ShapeDtypeStruct((b, s, d), BF16),
        compiler_params=_params(("parallel", "parallel")),
        name="modulate",
    )(x, mod3, mod3)


def _tail_body(t_ref, qg_ref, kvg_ref, lng_ref, lnb_ref, bf_ref,
               cq_ref, ckv_ref, kin_ref, small_ref, cumt_ref, carry_ref, *, ql, kvl, w_scale):
    j = pl.program_id(1)
    ts = t_ref.shape[0]

    @pl.when(j == 0)
    def _():
        carry_ref[...] = jnp.zeros_like(carry_ref)

    cq = t_ref[:, 0:ql]
    cq_ref[...] = (cq * lax.rsqrt(jnp.mean(cq * cq, axis=-1, keepdims=True) + 1e-6)
                   * qg_ref[...]).astype(cq_ref.dtype)
    ckv = t_ref[:, ql:ql + kvl]
    ckv_ref[...] = (ckv * lax.rsqrt(jnp.mean(ckv * ckv, axis=-1, keepdims=True) + 1e-6)
                    * kvg_ref[...]).astype(ckv_ref.dtype)
    ki = t_ref[:, ql + kvl:ql + kvl + IDX_DIM]
    mu = jnp.mean(ki, axis=-1, keepdims=True)
    kc = ki - mu
    var = jnp.mean(kc * kc, axis=-1, keepdims=True)
    kin_ref[...] = (kc * lax.rsqrt(var + 1e-5) * lng_ref[...] + lnb_ref[...]).astype(kin_ref.dtype)

    small = t_ref[:, ql + kvl + IDX_DIM:ql + kvl + IDX_DIM + LANES]
    small_ref[...] = small * w_scale
    z = small + bf_ref[...]
    log_f = jnp.minimum(z, 0.0) - jnp.log(1.0 + jnp.exp(-jnp.abs(z)))
    row = lax.broadcasted_iota(I32, (ts, ts), 0)
    col = lax.broadcasted_iota(I32, (ts, ts), 1)
    tri = jnp.where(col <= row, 1.0, 0.0).astype(F32)
    incl = jnp.dot(tri, log_f, preferred_element_type=F32,
                   precision=lax.Precision.HIGHEST) + carry_ref[0:1, :]
    carry_ref[0:1, :] = incl[ts - 1:ts, :]
    cumt_ref[0] = incl.T


def _tail_prep(tail, b, s, ql, kvl, fh, ih, q_norm_g, kv_norm_g, ln_g, ln_b, b_forget):
    n = tail.shape[0]
    ts = _tile(s, 256)
    nj = s // ts
    bf_pad = jnp.zeros((1, LANES), F32).at[0, :fh].set(b_forget)
    w_scale = (ih ** -0.5) * (IDX_DIM ** -0.5)
    row = lambda i, j: (i * nj + j, 0)
    const = lambda i, j: (0, 0)
    return pl.pallas_call(
        functools.partial(_tail_body, ql=ql, kvl=kvl, w_scale=w_scale),
        grid=(b, nj),
        in_specs=[pl.BlockSpec((ts, tail.shape[1]), row),
                  pl.BlockSpec((1, ql), const), pl.BlockSpec((1, kvl), const),
                  pl.BlockSpec((1, IDX_DIM), const), pl.BlockSpec((1, IDX_DIM), const),
                  pl.BlockSpec((1, LANES), const)],
        out_specs=[pl.BlockSpec((ts, ql), row), pl.BlockSpec((ts, kvl), row),
                   pl.BlockSpec((ts, IDX_DIM), row), pl.BlockSpec((ts, LANES), row),
                   pl.BlockSpec((1, LANES, ts), lambda i, j: (i, 0, j))],
        out_shape=[jax.ShapeDtypeStruct((n, ql), BF16), jax.ShapeDtypeStruct((n, kvl), BF16),
                   jax.ShapeDtypeStruct((n, IDX_DIM), BF16), jax.ShapeDtypeStruct((n, LANES), F32),
                   jax.ShapeDtypeStruct((b, LANES, s), F32)],
        scratch_shapes=[pltpu.VMEM((8, LANES), F32)],
        compiler_params=_params(("parallel", "arbitrary")),
        name="tail_prep",
    )(tail, q_norm_g[None], kv_norm_g[None], ln_g[None], ln_b[None], bf_pad)


_MAX_EXACT = N_BUCKETS // 2
_BUCKET_THRESHOLDS = tuple(
    int(math.ceil(_MAX_EXACT * (MAX_DISTANCE / _MAX_EXACT) ** (k / (N_BUCKETS - _MAX_EXACT)) - 1e-9))
    for k in range(1, N_BUCKETS - _MAX_EXACT))


def _bias_body(rb_ref, o_ref, *, t):
    which = pl.program_id(0)
    h = pl.program_id(1)
    row = lax.broadcasted_iota(I32, (t, t), 0)
    col = lax.broadcasted_iota(I32, (t, t), 1)
    d = jnp.maximum(which * t + row - col, 0)
    large = jnp.full((t, t), _MAX_EXACT, I32)
    for thr in _BUCKET_THRESHOLDS:
        large = large + jnp.where(d >= thr, 1, 0)
    bucket = jnp.where(d < _MAX_EXACT, d, large)
    bias = jnp.zeros((t, t), F32)
    for bkt in range(N_BUCKETS):
        bias = jnp.where(bucket == bkt, rb_ref[bkt, h], bias)
    o_ref[0, 0] = bias


def _bias_tiles(rel_bias, nh, t):
    return pl.pallas_call(
        functools.partial(_bias_body, t=t),
        grid=(2, nh),
        in_specs=[pl.BlockSpec(memory_space=pltpu.SMEM)],
        out_specs=pl.BlockSpec((1, 1, t, t), lambda w, h: (w, h, 0, 0)),
        out_shape=jax.ShapeDtypeStruct((2, nh, t, t), F32),
        compiler_params=_params(("parallel", "parallel")),
        name="bias_tiles",
    )(rel_bias)


def _attn_body(qi_ref, kj_ref, *refs, nh, t, scale, mode):
    if mode == "fox":
        q_ref, k_ref, v_ref, g_ref, cum_ref, o_ref, m_ref, l_ref, acc_ref = refs
    else:
        (q_ref, k_ref, v_ref, g_ref, mask_ref, bt_ref, rb_ref,
         o_ref, m_ref, l_ref, acc_ref) = refs
    p = pl.program_id(1)
    qi = qi_ref[p]
    kj = kj_ref[p]

    @pl.when(kj == 0)
    def _():
        m_ref[...] = jnp.full_like(m_ref, NEG_BIG)
        l_ref[...] = jnp.zeros_like(l_ref)
        acc_ref[...] = jnp.zeros_like(acc_ref)

    if mode == "fox":
        row = lax.broadcasted_iota(I32, (t, t), 0) + qi * t
        col = lax.broadcasted_iota(I32, (t, t), 1) + kj * t
        neg = jnp.where(col <= row, 0.0, NEG_BIG).astype(F32)
    else:
        neg = (mask_ref[0, 0].astype(F32) - 1.0) * (-NEG_BIG)
        near = jnp.minimum(qi - kj, 1)
        far = (qi - kj) >= 2

    for h in range(nh):
        sl = slice(h * HEAD_DIM, (h + 1) * HEAD_DIM)
        s = lax.dot_general(q_ref[:, sl], k_ref[:, sl], (((1,), (1,)), ((), ())),
                            preferred_element_type=F32)
        if mode == "fox":
            s = s * scale + (neg - cum_ref[0, h:h + 1, :])
        else:
            bias = jnp.where(far, rb_ref[N_BUCKETS - 1, h], bt_ref[near, h])
            s = s * scale + (neg + bias)
        m_prev = m_ref[h]
        m_next = jnp.maximum(m_prev, jnp.max(s, axis=1, keepdims=True))
        alpha = jnp.exp(m_prev - m_next)
        pr = jnp.exp(s - jnp.tile(m_next, (1, t // LANES)))
        l_ref[h] = alpha * l_ref[h] + jnp.sum(pr, axis=1, keepdims=True)
        acc_ref[h] = acc_ref[h] * alpha + jnp.dot(pr.astype(BF16), v_ref[:, sl],
                                                  preferred_element_type=F32)
        m_ref[h] = m_next

    @pl.when(kj == qi)
    def _():
        ss = jnp.zeros((t, HEAD_DIM), F32)
        for h in range(nh):
            o = acc_ref[h] / l_ref[h]
            acc_ref[h] = o
            ss = ss + o * o
        rinv = lax.rsqrt(jnp.sum(ss, axis=1, keepdims=True) / (nh * HEAD_DIM) + 1e-6)
        for h in range(nh):
            sl = slice(h * HEAD_DIM, (h + 1) * HEAD_DIM)
            o_ref[:, sl] = (acc_ref[h] * rinv * g_ref[:, sl]).astype(o_ref.dtype)


def _attention(mode, b, s, nh, t, q_src, k_src, v_src, gain, *,
               cum_t=None, mask=None, bias_tiles=None, rel_bias=None):
    nq = s // t
    pairs = [(i, j) for i in range(nq) for j in range(i + 1)]
    qi_arr = jnp.asarray(np.array([pq for pq, _ in pairs], np.int32))
    kj_arr = jnp.asarray(np.array([pk for _, pk in pairs], np.int32))
    hw = nh * HEAD_DIM
    n = b * s

    def qmap(cb):
        return lambda bi, p, qi, kj: (bi * nq + qi[p], cb)

    def kmap(cb):
        return lambda bi, p, qi, kj: (bi * nq + kj[p], cb)

    in_specs = [pl.BlockSpec((t, hw), qmap(q_src[1])),
                pl.BlockSpec((t, hw), kmap(k_src[1])),
                pl.BlockSpec((t, hw), kmap(v_src[1])),
                pl.BlockSpec((1, hw), lambda bi, p, qi, kj: (0, 0))]
    args = [q_src[0], k_src[0], v_src[0], gain[None]]
    if mode == "fox":
        in_specs.append(pl.BlockSpec((1, LANES, t), lambda bi, p, qi, kj: (bi, 0, kj[p])))
        args.append(cum_t)
    else:
        in_specs += [pl.BlockSpec((1, 1, t, t), lambda bi, p, qi, kj: (bi, kj[p], qi[p], 0)),
                     pl.BlockSpec(memory_space=pltpu.VMEM),
                     pl.BlockSpec(memory_space=pltpu.SMEM)]
        args += [mask, bias_tiles, rel_bias]
    grid_spec = pltpu.PrefetchScalarGridSpec(
        num_scalar_prefetch=2,
        grid=(b, len(pairs)),
        in_specs=in_specs,
        out_specs=pl.BlockSpec((t, hw), lambda bi, p, qi, kj: (bi * nq + qi[p], 0)),
        scratch_shapes=[pltpu.VMEM((nh, t, LANES), F32), pltpu.VMEM((nh, t, LANES), F32),
                        pltpu.VMEM((nh, t, HEAD_DIM), F32)],
    )
    return pl.pallas_call(
        functools.partial(_attn_body, nh=nh, t=t, scale=HEAD_DIM ** -0.5, mode=mode),
        grid_spec=grid_spec,
        out_shape=jax.ShapeDtypeStruct((n, hw), BF16),
        compiler_params=_params(("parallel", "arbitrary")),
        name=mode + "_attention",
    )(qi_arr, kj_arr, *args)


def _float_key(x):
    bits = lax.bitcast_convert_type(x, I32)
    return bits ^ (lax.shift_right_arithmetic(bits, 31) & 0x7FFFFFFF)


def _indexer_body(q_ref, k_ref, w_ref, o_ref, key_ref, *, ih, w_lane0, tq, ck, nchunk, k_sel):
    i = pl.program_id(1)
    n_valid = ((i + 1) * tq + ck - 1) // ck
    row = lax.broadcasted_iota(I32, (tq, ck), 0) + i * tq
    lane_col = lax.broadcasted_iota(I32, (tq, ck), 1)
    w_all = w_ref[...]

    def score_chunk(c, carry):
        kc = k_ref[0, pl.ds(pl.multiple_of(c * ck, ck), ck), :]
        acc = jnp.zeros((tq, ck), F32)
        for h in range(ih):
            sh = lax.dot_general(q_ref[:, h * IDX_DIM:(h + 1) * IDX_DIM], kc,
                                 (((1,), (1,)), ((), ())), preferred_element_type=F32)
            acc = acc + w_all[:, w_lane0 + h:w_lane0 + h + 1] * jnp.maximum(sh, 0.0)
        acc = jnp.where(lane_col + c * ck <= row, acc, -jnp.inf)
        key_ref[c] = _float_key(acc)
        return carry

    lax.fori_loop(0, n_valid, score_chunk, 0)

    def count(pred_fn):
        def body(c, acc):
            return acc + jnp.where(pred_fn(key_ref[c]), 1, 0)
        acc = lax.fori_loop(0, n_valid, body, jnp.zeros((tq, ck), I32))
        return jnp.sum(acc, axis=1, keepdims=True)

    t0 = jnp.full((tq, 1), INT_MIN, I32)
    zero = jnp.zeros((tq, 1), I32)
    thr = jnp.where(count(lambda kv: kv >= 0) >= k_sel, zero, t0)

    def bit_step(bi, thr):
        cand = thr | lax.shift_left(jnp.int32(1), 30 - bi)
        return jnp.where(count(lambda kv: kv >= cand) >= k_sel, cand, thr)

    thr = lax.fori_loop(0, 31, bit_step, thr)
    need = (k_sel - count(lambda kv: kv > thr)).astype(F32)

    urow = lax.broadcasted_iota(I32, (ck, ck), 0)
    ucol = lax.broadcasted_iota(I32, (ck, ck), 1)
    upper = jnp.where(urow <= ucol, 1.0, 0.0).astype(BF16)
    ones = jnp.ones((ck, LANES), BF16)

    def select_chunk(c, offset):
        kv = key_ref[c]
        eq = kv == thr
        eqb = jnp.where(eq, 1.0, 0.0).astype(BF16)
        rank = jnp.dot(eqb, upper, preferred_element_type=F32) + offset[:, 0:1]
        take = (kv > thr) | (eq & (rank <= need))
        take = take & (lane_col + c * ck <= row)
        o_ref[0, c] = jnp.where(take, 1, 0).astype(jnp.int8)
        return offset + jnp.dot(eqb, ones, preferred_element_type=F32)

    lax.fori_loop(0, n_valid, select_chunk, jnp.zeros((tq, LANES), F32))

    def zero_chunk(c, carry):
        o_ref[0, c] = jnp.zeros((tq, ck), jnp.int8)
        return carry

    lax.fori_loop(n_valid, nchunk, zero_chunk, 0)


def _indexer(q_idx, k_in, small, b, s, ih, w_lane0, ck, k_sel):
    tq = 128
    nq = s // tq
    nchunk = s // ck
    k3 = k_in.reshape(b, s, IDX_DIM)
    return pl.pallas_call(
        functools.partial(_indexer_body, ih=ih, w_lane0=w_lane0, tq=tq, ck=ck, nchunk=nchunk,
                          k_sel=k_sel),
        grid=(b, nq),
        in_specs=[pl.BlockSpec((tq, ih * IDX_DIM), lambda bi, i: (bi * nq + i, 0)),
                  pl.BlockSpec((1, s, IDX_DIM), lambda bi, i: (bi, 0, 0)),
                  pl.BlockSpec((tq, LANES), lambda bi, i: (bi * nq + i, 0))],
        out_specs=pl.BlockSpec((1, nchunk, tq, ck), lambda bi, i: (bi, 0, i, 0)),
        out_shape=jax.ShapeDtypeStruct((b, nchunk, s, ck), jnp.int8),
        scratch_shapes=[pltpu.VMEM((nchunk, tq, ck), I32)],
        compiler_params=_params(("parallel", "parallel")),
        name="indexer_topk",
    )(q_idx, k3, small)


def _layer_norm(h, g, bvec):
    mu = jnp.mean(h, axis=-1, keepdims=True)
    hc = h - mu
    var = jnp.mean(hc * hc, axis=-1, keepdims=True)
    return hc * lax.rsqrt(var + 1e-5) * g + bvec


def _router_body(x_ref, mix_ref, ga_ref, shm_ref, scm_ref, lg_ref, lb_ref, wr_ref, br_ref,
                 x1_ref, up_ref, eidx_ref, rank_ref, gate_ref, cnt_ref, carry_ref,
                 *, alpha_res, ne, nsteps):
    i = pl.program_id(0)
    tm, d = x_ref.shape

    @pl.when(i == 0)
    def _():
        carry_ref[...] = jnp.zeros_like(carry_ref)

    x1 = _layer_norm(alpha_res * x_ref[...] + ga_ref[0] * mix_ref[...], lg_ref[...], lb_ref[...])
    x1_ref[...] = x1
    u = x1 * (1.0 + scm_ref[0]) + shm_ref[0]
    ub = u.astype(BF16).astype(F32)
    lo = lax.bitcast_convert_type(ub[:, :d // 2], I32)
    hi = lax.bitcast_convert_type(ub[:, d // 2:], I32)
    up_ref[...] = lax.shift_right_logical(lo, 16) | (hi & jnp.int32(-65536))

    logits = jnp.dot(u, wr_ref[...], preferred_element_type=F32,
                     precision=lax.Precision.HIGHEST) + br_ref[...]
    lane = lax.broadcasted_iota(I32, (tm, LANES), 1)
    lane_f = lane.astype(F32)
    work = jnp.where(lane < ne, logits, -jnp.inf)
    sel = jnp.zeros((tm, LANES), F32)
    idxs, vals, hits = [], [], []
    for _ in range(TOP_K):
        mx = jnp.max(work, axis=1, keepdims=True)
        ik = jnp.min(jnp.where(work == mx, lane_f, float(LANES)), axis=1, keepdims=True)
        hit = lane_f == ik
        sel = jnp.where(hit, 1.0, sel)
        work = jnp.where(hit, -jnp.inf, work)
        idxs.append(ik.astype(I32))
        vals.append(mx)
        hits.append(hit)
    exps = [jnp.exp(v - vals[0]) for v in vals]
    denom = exps[0] + exps[1] + exps[2] + exps[3]

    row = lax.broadcasted_iota(I32, (tm, tm), 0)
    col = lax.broadcasted_iota(I32, (tm, tm), 1)
    tri = jnp.where(col < row, 1.0, 0.0).astype(BF16)
    before = jnp.dot(tri, sel.astype(BF16), preferred_element_type=F32) + carry_ref[0:1, :]
    total = carry_ref[0:1, :] + jnp.sum(sel, axis=0, keepdims=True)
    carry_ref[0:1, :] = total

    eidx = jnp.zeros((tm, LANES), I32)
    rank = jnp.zeros((tm, LANES), I32)
    gate = jnp.zeros((tm, LANES), F32)
    for kk in range(TOP_K):
        rk = jnp.sum(jnp.where(hits[kk], before, 0.0), axis=1, keepdims=True)
        eidx = jnp.where(lane == kk, idxs[kk], eidx)
        rank = jnp.where(lane == kk, rk.astype(I32), rank)
        gate = jnp.where(lane == kk, exps[kk] / denom, gate)
    eidx_ref[...] = eidx
    rank_ref[...] = rank
    gate_ref[...] = gate
    cnt_ref[...] = jnp.broadcast_to(total, cnt_ref.shape).astype(I32)


def _router(x2d, mix, mod3, ln_g, ln_b, w_router_pad, b_router_pad, s, ne, alpha_res):
    n, d = x2d.shape
    tm = _tile(s, 256, 16)
    nsteps = n // tm
    per_b = s // tm
    row = lambda i: (i, 0)
    const = lambda i: (0, 0)
    modspec = lambda chunk: pl.BlockSpec((1, 1, d), lambda i: (i // per_b, 0, chunk))
    return pl.pallas_call(
        functools.partial(_router_body, alpha_res=alpha_res, ne=ne, nsteps=nsteps),
        grid=(nsteps,),
        in_specs=[pl.BlockSpec((tm, d), row), pl.BlockSpec((tm, d), row),
                  modspec(2), modspec(3), modspec(4),
                  pl.BlockSpec((1, d), const), pl.BlockSpec((1, d), const),
                  pl.BlockSpec((d, LANES), const), pl.BlockSpec((1, LANES), const)],
        out_specs=[pl.BlockSpec((tm, d), row), pl.BlockSpec((tm, d // 2), row),
                   pl.BlockSpec((tm, LANES), row), pl.BlockSpec((tm, LANES), row),
                   pl.BlockSpec((tm, LANES), row), pl.BlockSpec((8, LANES), const)],
        out_shape=[jax.ShapeDtypeStruct((n, d), F32), jax.ShapeDtypeStruct((n, d // 2), I32),
                   jax.ShapeDtypeStruct((n, LANES), I32), jax.ShapeDtypeStruct((n, LANES), I32),
                   jax.ShapeDtypeStruct((n, LANES), F32), jax.ShapeDtypeStruct((8, LANES), I32)],
        scratch_shapes=[pltpu.VMEM((8, LANES), F32)],
        compiler_params=_params(("arbitrary",)),
        name="ln1_router",
    )(x2d, mix, mod3, mod3, mod3, ln_g[None], ln_b[None], w_router_pad, b_router_pad)


def _dispatch_body(dest_ref, u_ref, xs_ref, sem, *, n_tok, chunk):
    def copy(tok, kk):
        d = dest_ref[tok * TOP_K + kk]
        return pltpu.make_async_copy(u_ref.at[pl.ds(tok, 1)], xs_ref.at[pl.ds(d, 1)], sem.at[0])

    def chunk_body(c, carry):
        def start(j, cc):
            for kk in range(TOP_K):
                copy(c * chunk + j, kk).start()
            return cc

        def wait(j, cc):
            for kk in range(TOP_K):
                copy(c * chunk + j, kk).wait()
            return cc

        lax.fori_loop(0, chunk, start, 0)
        lax.fori_loop(0, chunk, wait, 0)
        return carry

    lax.fori_loop(0, n_tok // chunk, chunk_body, 0)


def _dispatch(dest_flat, u_packed, n_slots):
    n, dw = u_packed.shape
    chunk = _tile(n, 256, 8)
    grid_spec = pltpu.PrefetchScalarGridSpec(
        num_scalar_prefetch=1, grid=(1,),
        in_specs=[pl.BlockSpec(memory_space=pl.ANY)],
        out_specs=pl.BlockSpec(memory_space=pl.ANY),
        scratch_shapes=[pltpu.SemaphoreType.DMA((1,))])
    return pl.pallas_call(
        functools.partial(_dispatch_body, n_tok=n, chunk=chunk),
        grid_spec=grid_spec,
        out_shape=jax.ShapeDtypeStruct((n_slots, dw), I32),
        compiler_params=pltpu.CompilerParams(dimension_semantics=("arbitrary",),
                                             has_side_effects=True),
        name="moe_dispatch",
    )(dest_flat, u_packed)


def _ffn_body(te_ref, nsub_ref, nused_ref, x_ref, w1g_ref, w1u_ref, b1g_ref, b1u_ref, w2_ref, b2_ref,
              o_ref, h_ref, *, nf, tf, ts, nr):
    i = pl.program_id(0)
    st = pl.program_id(1)
    tm, dh = x_ref.shape
    nsub = nsub_ref[i]

    def unpack(rows):
        xw = x_ref[0:rows, :]
        lo = lax.bitcast_convert_type(lax.shift_left(xw, 16), F32).astype(BF16)
        hi = lax.bitcast_convert_type(xw & jnp.int32(-65536), F32).astype(BF16)
        return lo, hi

    def up_step(rows):
        lo, hi = unpack(rows)
        def proj(w_ref, b_ref):
            return (jnp.dot(lo, w_ref[0, 0:dh, :].astype(BF16), preferred_element_type=F32)
                    + jnp.dot(hi, w_ref[0, dh:2 * dh, :].astype(BF16), preferred_element_type=F32)
                    + b_ref[0])
        gate = jnp.minimum(proj(w1g_ref, b1g_ref), SWIGLU_LIMIT)
        up = jnp.clip(proj(w1u_ref, b1u_ref), -SWIGLU_LIMIT, SWIGLU_LIMIT)
        act = (up + 1.0) * gate * jax.nn.sigmoid(SWIGLU_ALPHA * gate)
        h_ref[st, 0:rows, :] = act.astype(BF16)

    def down_step(rows):
        acc = jnp.broadcast_to(b2_ref[0], (rows, o_ref.shape[1])).astype(F32)
        for f in range(nf):
            acc = acc + jnp.dot(h_ref[f, 0:rows, :], w2_ref[0, f * tf:(f + 1) * tf, :].astype(BF16),
                                preferred_element_type=F32)
        o_ref[0:rows, :] = acc
        if rows < tm:
            o_ref[rows:tm, :] = jnp.zeros((tm - rows, o_ref.shape[1]), F32)

    for r in range(1, nr + 1):
        @pl.when(jnp.logical_and(nsub == r, st < nf))
        def _(r=r):
            up_step(r * ts)

        @pl.when(jnp.logical_and(nsub == r, st >= nf))
        def _(r=r):
            down_step(r * ts)


def _expert_ffn(xs, w1, b1, w2, b2, tile_e, tile_nsub, n_used, *, tm, ts, n_tiles):
    ne, d, de2 = w1.shape
    de = de2 // 2
    tf = _tile(de, 256)
    td = _tile(d, 512)
    nf, nd = de // tf, d // td
    nstep = nf + nd
    dh = d // 2
    b1r = b1.reshape(ne, 1, de2)
    b2r = b2.reshape(ne, 1, d)

    def tile_idx(i, nu):
        return jnp.minimum(i, nu[0] - 1)

    def step_idx(i, s, nu):
        return jnp.where(i < nu[0], s, nstep - 1)

    def up_map(off):
        return lambda i, s, te, ns, nu: (te[tile_idx(i, nu)], 0,
                                         off + jnp.minimum(step_idx(i, s, nu), nf - 1))

    def down_map(i, s, te, ns, nu):
        return (te[tile_idx(i, nu)], 0, jnp.maximum(step_idx(i, s, nu) - nf, 0))

    grid_spec = pltpu.PrefetchScalarGridSpec(
        num_scalar_prefetch=3,
        grid=(n_tiles, nstep),
        in_specs=[pl.BlockSpec((tm, dh), lambda i, s, te, ns, nu: (tile_idx(i, nu), 0)),
                  pl.BlockSpec((1, d, tf), up_map(0)), pl.BlockSpec((1, d, tf), up_map(nf)),
                  pl.BlockSpec((1, 1, tf), up_map(0)), pl.BlockSpec((1, 1, tf), up_map(nf)),
                  pl.BlockSpec((1, de, td), down_map), pl.BlockSpec((1, 1, td), down_map)],
        out_specs=pl.BlockSpec((tm, td), lambda i, s, te, ns, nu: (
            tile_idx(i, nu), jnp.maximum(step_idx(i, s, nu) - nf, 0))),
        scratch_shapes=[pltpu.VMEM((nf, tm, tf), BF16)],
    )
    return pl.pallas_call(
        functools.partial(_ffn_body, nf=nf, tf=tf, ts=ts, nr=tm // ts),
        grid_spec=grid_spec,
        out_shape=jax.ShapeDtypeStruct((n_tiles * tm, d), F32),
        compiler_params=_params(("arbitrary", "arbitrary")),
        name="expert_ffn",
    )(tile_e, tile_nsub, n_used, xs, w1, w1, b1r, b1r, w2, b2r)


def _combine_body(dest_ref, y_ref, gate_ref, x1_ref, gm_ref, lg_ref, lb_ref, o_ref, ybuf, sem,
                  *, alpha_res):
    i = pl.program_id(0)
    tc = x1_ref.shape[0]

    def copy(j, kk):
        d = dest_ref[(i * tc + j) * TOP_K + kk]
        return pltpu.make_async_copy(y_ref.at[pl.ds(d, 1)], ybuf.at[kk, pl.ds(j, 1)], sem.at[0])

    def start(j, cc):
        for kk in range(TOP_K):
            copy(j, kk).start()
        return cc

    def wait(j, cc):
        for kk in range(TOP_K):
            copy(j, kk).wait()
        return cc

    lax.fori_loop(0, tc, start, 0)
    lax.fori_loop(0, tc, wait, 0)
    g = gate_ref[...]
    y = g[:, 0:1] * ybuf[0]
    for kk in range(1, TOP_K):
        y = y + g[:, kk:kk + 1] * ybuf[kk]
    o_ref[...] = _layer_norm(alpha_res * x1_ref[...] + gm_ref[0] * y, lg_ref[...], lb_ref[...])


def _combine(dest_flat, y_sorted, gate, x1, mod3, ln_g, ln_b, s, alpha_res):
    n, d = x1.shape
    tc = _tile(s, 128, 8)
    per_b = s // tc
    row = lambda i, dr: (i, 0)
    const = lambda i, dr: (0, 0)
    grid_spec = pltpu.PrefetchScalarGridSpec(
        num_scalar_prefetch=1, grid=(n // tc,),
        in_specs=[pl.BlockSpec(memory_space=pl.ANY),
                  pl.BlockSpec((tc, LANES), row), pl.BlockSpec((tc, d), row),
                  pl.BlockSpec((1, 1, d), lambda i, dr: (i // per_b, 0, 5)),
                  pl.BlockSpec((1, d), const), pl.BlockSpec((1, d), const)],
        out_specs=pl.BlockSpec((tc, d), row),
        scratch_shapes=[pltpu.VMEM((TOP_K, tc, d), F32), pltpu.SemaphoreType.DMA((1,))])
    return pl.pallas_call(
        functools.partial(_combine_body, alpha_res=alpha_res),
        grid_spec=grid_spec,
        out_shape=jax.ShapeDtypeStruct((n, d), F32),
        compiler_params=_params(("arbitrary",)),
        name="moe_combine",
    )(dest_flat, y_sorted, gate, x1, mod3, ln_g[None], ln_b[None])


def _layer(x, c_pad, w_ada, b_ada, w_in, b_forget, q_norm_g, kv_norm_g, kidx_ln_g, kidx_ln_b,
           w_uq, w_uk, w_uv, w_iq, fox_out_g, dsa_out_g, w_o, ln1_g, ln1_b,
           w_router, b_router, w1, b1, w2, b2, ln2_g, ln2_b, rel_bias, alpha_res):
    b, s, d = x.shape
    n = b * s
    fh = b_forget.shape[0]
    ql = q_norm_g.shape[0]
    kvl = kv_norm_g.shape[0]
    dh = w_uk.shape[1]
    ih = w_iq.shape[1] // IDX_DIM
    ne = w_router.shape[1]
    fw = fh * HEAD_DIM
    dw = dh * HEAD_DIM
    assert fw == dw, "head groups share the attention kernel's column-block width"
    assert fh + ih <= LANES and ne <= LANES

    mod = _matmul(c_pad, w_ada, 6 * d, F32, bias=b_ada[None], a_silu=True,
                  tm=16, tn=512, tk=d, name="adaln")
    mod3 = mod[:b].reshape(b, 1, 6 * d)
    u = _modulate(x, mod3, 0, 1).reshape(n, d)

    qkv = _matmul(u, w_in, 3 * fw, BF16, tm=2048, name="proj_qkv")
    c0 = 3 * fw
    tail_w = -(-(ql + kvl + IDX_DIM + LANES) // LANES) * LANES
    w_tail = jnp.concatenate(
        [w_in[:, c0 + fh:c0 + fh + ql + kvl + IDX_DIM],
         w_in[:, c0:c0 + fh],
         w_in[:, c0 + fh + ql + kvl + IDX_DIM:],
         jnp.zeros((d, tail_w - (ql + kvl + IDX_DIM + fh + ih)), F32)], axis=1)
    tail = _matmul(u, w_tail, tail_w, F32, tn=1152, name="proj_tail")
    cqn, ckvn, kin, small, cum_t = _tail_prep(tail, b, s, ql, kvl, fh, ih, q_norm_g, kv_norm_g,
                                              kidx_ln_g, kidx_ln_b, b_forget)

    t = 256 if s % 256 == 0 else 128
    fox = _attention("fox", b, s, fh, t, (qkv, 0), (qkv, 1), (qkv, 2), fox_out_g, cum_t=cum_t)

    q_d = _matmul(cqn, w_uq, dw, BF16, tm=2048, name="proj_qd")
    q_i = _matmul(cqn, w_iq, ih * IDX_DIM, BF16, tm=2048, name="proj_qi")
    k_d = _matmul(ckvn, w_uk.reshape(kvl, dw), dw, BF16, tm=2048, name="expand_k")
    v_d = _matmul(ckvn, w_uv.reshape(kvl, dw), dw, BF16, tm=2048, name="expand_v")
    k_sel = min(TOPK_MAX, s // 4)
    mask = _indexer(q_i, kin, small, b, s, ih, fh, t, k_sel)
    bias_t = _bias_tiles(rel_bias, dh, t)
    dsa = _attention("dsa", b, s, dh, t, (q_d, 0), (k_d, 0), (v_d, 0), dsa_out_g,
                     mask=mask, bias_tiles=bias_t, rel_bias=rel_bias)

    mix = _matmul(fox, w_o, d, F32, a2=dsa, name="proj_out")

    wr_pad = jnp.zeros((d, LANES), F32).at[:, :ne].set(w_router)
    br_pad = jnp.zeros((1, LANES), F32).at[0, :ne].set(b_router)
    x1, u_packed, eidx, rank, gate, cnt = _router(x.reshape(n, d), mix, mod3, ln1_g, ln1_b,
                                                  wr_pad, br_pad, s, ne, alpha_res)
    tm = _tile(n, 1024, 16)
    ts = _tile(tm, 256, 16)
    n_tiles = -(-(n * TOP_K) // tm) + ne
    counts = cnt[0, :ne]
    ntile = (counts + tm - 1) // tm
    tend = jnp.cumsum(ntile)
    tstart = tend - ntile
    n_used = tend[-1]
    dest = (tstart * tm)[eidx[:, :TOP_K]] + rank[:, :TOP_K]
    dest_flat = dest.reshape(n * TOP_K).astype(I32)
    tid = jnp.minimum(jnp.arange(n_tiles, dtype=I32), n_used - 1)
    tile_e = jnp.minimum(jnp.searchsorted(tend, tid, side="right"), ne - 1).astype(I32)
    valid = jnp.clip(counts[tile_e] - (tid - tstart[tile_e]) * tm, 0, tm)
    valid = jnp.where(jnp.arange(n_tiles) < n_used, valid, 0)
    tile_nsub = ((valid + ts - 1) // ts).astype(I32)
    xs = _dispatch(dest_flat, u_packed, n_tiles * tm)
    y_sorted = _expert_ffn(xs, w1, b1, w2, b2, tile_e, tile_nsub,
                           n_used.reshape(1).astype(I32), tm=tm, ts=ts, n_tiles=n_tiles)
    out = _combine(dest_flat, y_sorted, gate, x1, mod3, ln2_g, ln2_b, s, alpha_res)
    return out.reshape(b, s, d)


def kernel(x, c, w_ada, b_ada, w_in, b_forget, q_norm_g, kv_norm_g, kidx_ln_g, kidx_ln_b, w_uq, w_uk, w_uv, w_iq, fox_out_g, dsa_out_g, w_o, ln1_g, ln1_b, w_router, b_router, w1, b1, w2, b2, ln2_g, ln2_b, rel_bias):
    depth = w_ada.shape[0]
    alpha_res = (2 * depth) ** 0.25
    b, d = c.shape
    c_pad = jnp.zeros((16, d), F32).at[:b].set(c)
    for l in range(depth):
        x = _layer(x, c_pad, w_ada[l], b_ada[l], w_in[l], b_forget[l], q_norm_g[l], kv_norm_g[l],
                   kidx_ln_g[l], kidx_ln_b[l], w_uq[l], w_uk[l], w_uv[l], w_iq[l], fox_out_g[l],
                   dsa_out_g[l], w_o[l], ln1_g[l], ln1_b[l], w_router[l], b_router[l],
                   w1[l], b1[l], w2[l], b2[l], ln2_g[l], ln2_b[l], rel_bias, alpha_res)
    return x
```

```python
import functools
import math

import numpy as np
import jax
import jax.numpy as jnp
from jax import lax
from jax.experimental import pallas as pl
from jax.experimental.pallas import tpu as pltpu

F32 = jnp.float32
BF16 = jnp.bfloat16
I32 = jnp.int32

LANES = 128
HEAD_DIM = 128
IDX_DIM = 128
TOPK_MAX = 256
N_BUCKETS = 32
MAX_DISTANCE = 128
TOP_K = 4
SWIGLU_LIMIT = 7.0
SWIGLU_ALPHA = 1.702
NEG_BIG = -1e30
VMEM_LIMIT = 56 * 1024 * 1024
INT_MIN = -2 ** 31


def _tile(n, pref, unit=LANES):
    if n <= pref:
        return n
    t = (pref // unit) * unit
    while t > unit and n % t:
        t -= unit
    assert n % t == 0, (n, pref)
    return t


def _params(sem, vmem=VMEM_LIMIT):
    return pltpu.CompilerParams(dimension_semantics=sem, vmem_limit_bytes=vmem)


def _mm_body(*refs, nk, nk1, a_silu, has_bias, col_block0, w_cols_if_overhang, w_t):
    refs = list(refs)
    a_ref = refs.pop(0)
    a2_ref = refs.pop(0) if nk1 < nk else None
    w_ref = refs.pop(0)
    b_ref = refs.pop(0) if has_bias else None
    o_ref, acc_ref = refs
    k = pl.program_id(2)

    @pl.when(k == 0)
    def _():
        acc_ref[...] = jnp.zeros_like(acc_ref)

    def accumulate(src_ref):
        a = src_ref[...]
        if a_silu:
            a = a.astype(F32)
            a = a * jax.nn.sigmoid(a)
        w = w_ref[...]
        n_axis = 0 if w_t else 1
        if w_cols_if_overhang is not None:
            tn = w.shape[n_axis]
            col = lax.broadcasted_iota(I32, w.shape, n_axis) + (pl.program_id(1) + col_block0) * tn
            w = jnp.where(col < w_cols_if_overhang, w, 0.0)
        acc_ref[...] += lax.dot_general(a.astype(BF16), w.astype(BF16),
                                        (((1,), (1 if w_t else 0,)), ((), ())),
                                        preferred_element_type=F32)

    if a2_ref is None:
        accumulate(a_ref)
    else:
        pl.when(k < nk1)(lambda: accumulate(a_ref))
        pl.when(k >= nk1)(lambda: accumulate(a2_ref))

    @pl.when(k == nk - 1)
    def _():
        r = acc_ref[...]
        if has_bias:
            r = r + b_ref[...]
        o_ref[...] = r.astype(o_ref.dtype)


def _matmul(a, w, n_out, out_dtype, *, a2=None, bias=None, a_silu=False, col0=0, w_t=False,
            tm=1024, tn=1024, tk=1024, name="matmul"):
    m, kdim = a.shape
    w_cols = w.shape[0] if w_t else w.shape[1]
    tm = _tile(m, tm, 16)
    tn = _tile(math.gcd(n_out, col0) if col0 else n_out, tn)
    tk = _tile(kdim, tk)
    nk1 = kdim // tk
    nk = nk1 if a2 is None else nk1 + a2.shape[1] // tk
    col_block0 = col0 // tn
    overhang = col0 + n_out > w_cols
    in_specs = [pl.BlockSpec((tm, tk), lambda i, j, k: (i, jnp.minimum(k, nk1 - 1)))]
    args = [a]
    if a2 is not None:
        assert a2.shape[1] % tk == 0
        in_specs.append(pl.BlockSpec((tm, tk), lambda i, j, k: (i, jnp.maximum(k - nk1, 0))))
        args.append(a2)
    if w_t:
        in_specs.append(pl.BlockSpec((tn, tk), lambda i, j, k: (j + col_block0, k)))
    else:
        in_specs.append(pl.BlockSpec((tk, tn), lambda i, j, k: (k, j + col_block0)))
    args.append(w)
    if bias is not None:
        in_specs.append(pl.BlockSpec((1, tn), lambda i, j, k: (0, j)))
        args.append(bias)
    return pl.pallas_call(
        functools.partial(_mm_body, nk=nk, nk1=nk1, a_silu=a_silu, has_bias=bias is not None,
                          col_block0=col_block0, w_t=w_t,
                          w_cols_if_overhang=w_cols if overhang else None),
        grid=(m // tm, n_out // tn, nk),
        in_specs=in_specs,
        out_specs=pl.BlockSpec((tm, tn), lambda i, j, k: (i, j)),
        out_shape=jax.ShapeDtypeStruct((m, n_out), out_dtype),
        scratch_shapes=[pltpu.VMEM((tm, tn), F32)],
        compiler_params=_params(("parallel", "parallel", "arbitrary")),
        name=name,
    )(*args)


def _modulate_body(x_ref, sh_ref, sc_ref, o_ref):
    o_ref[0] = (x_ref[0] * (1.0 + sc_ref[0]) + sh_ref[0]).astype(o_ref.dtype)


def _modulate(x, mod3, shift_chunk, scale_chunk):
    b, s, d = x.shape
    ts = _tile(s, 512, 16)
    return pl.pallas_call(
        _modulate_body,
        grid=(b, s // ts),
        in_specs=[pl.BlockSpec((1, ts, d), lambda i, j: (i, j, 0)),
                  pl.BlockSpec((1, 1, d), lambda i, j: (i, 0, shift_chunk)),
                  pl.BlockSpec((1, 1, d), lambda i, j: (i, 0, scale_chunk))],
        out_specs=pl.BlockSpec((1, ts, d), lambda i, j: (i, j, 0)),
        out_shape=jax.ShapeDtypeStruct((b, s, d), BF16),
        compiler_params=_params(("parallel", "parallel")),
        name="modulate",
    )(x, mod3, mod3)


def _tail_body(t_ref, qg_ref, kvg_ref, lng_ref, lnb_ref, bf_ref,
               cq_ref, ckv_ref, kin_ref, small_ref, cumt_ref, carry_ref, *, fh, ql, kvl, w_scale):
    j = pl.program_id(1)
    ts = t_ref.shape[0]

    @pl.when(j == 0)
    def _():
        carry_ref[...] = jnp.zeros_like(carry_ref)

    cq = t_ref[:, fh:fh + ql]
    cq_ref[...] = (cq * lax.rsqrt(jnp.mean(cq * cq, axis=-1, keepdims=True) + 1e-6)
                   * qg_ref[...]).astype(cq_ref.dtype)
    ckv = t_ref[:, fh + ql:fh + ql + kvl]
    ckv_ref[...] = (ckv * lax.rsqrt(jnp.mean(ckv * ckv, axis=-1, keepdims=True) + 1e-6)
                    * kvg_ref[...]).astype(ckv_ref.dtype)
    ki = t_ref[:, fh + ql + kvl:fh + ql + kvl + IDX_DIM]
    mu = jnp.mean(ki, axis=-1, keepdims=True)
    kc = ki - mu
    var = jnp.mean(kc * kc, axis=-1, keepdims=True)
    kin_ref[...] = (kc * lax.rsqrt(var + 1e-5) * lng_ref[...] + lnb_ref[...]).astype(kin_ref.dtype)

    small_ref[...] = t_ref[:, ql + kvl + IDX_DIM:ql + kvl + IDX_DIM + LANES] * w_scale
    z = t_ref[:, 0:LANES] + bf_ref[...]
    log_f = jnp.minimum(z, 0.0) - jnp.log(1.0 + jnp.exp(-jnp.abs(z)))
    row = lax.broadcasted_iota(I32, (ts, ts), 0)
    col = lax.broadcasted_iota(I32, (ts, ts), 1)
    tri = jnp.where(col <= row, 1.0, 0.0).astype(F32)
    incl = jnp.dot(tri, log_f, preferred_element_type=F32,
                   precision=lax.Precision.HIGHEST) + carry_ref[0:1, :]
    carry_ref[0:1, :] = incl[ts - 1:ts, :]
    cumt_ref[0] = incl.T


def _tail_prep(tail, b, s, ql, kvl, fh, ih, q_norm_g, kv_norm_g, ln_g, ln_b, b_forget):
    n = tail.shape[0]
    assert ql % LANES == 0 and kvl % LANES == 0
    ts = _tile(s, 256)
    nj = s // ts
    bf_pad = jnp.zeros((1, LANES), F32).at[0, :fh].set(b_forget)
    w_scale = (ih ** -0.5) * (IDX_DIM ** -0.5)
    row = lambda i, j: (i * nj + j, 0)
    const = lambda i, j: (0, 0)
    return pl.pallas_call(
        functools.partial(_tail_body, fh=fh, ql=ql, kvl=kvl, w_scale=w_scale),
        grid=(b, nj),
        in_specs=[pl.BlockSpec((ts, tail.shape[1]), row),
                  pl.BlockSpec((1, ql), const), pl.BlockSpec((1, kvl), const),
                  pl.BlockSpec((1, IDX_DIM), const), pl.BlockSpec((1, IDX_DIM), const),
                  pl.BlockSpec((1, LANES), const)],
        out_specs=[pl.BlockSpec((ts, ql), row), pl.BlockSpec((ts, kvl), row),
                   pl.BlockSpec((ts, IDX_DIM), row), pl.BlockSpec((ts, LANES), row),
                   pl.BlockSpec((1, LANES, ts), lambda i, j: (i, 0, j))],
        out_shape=[jax.ShapeDtypeStruct((n, ql), BF16), jax.ShapeDtypeStruct((n, kvl), BF16),
                   jax.ShapeDtypeStruct((n, IDX_DIM), BF16), jax.ShapeDtypeStruct((n, LANES), F32),
                   jax.ShapeDtypeStruct((b, LANES, s), F32)],
        scratch_shapes=[pltpu.VMEM((8, LANES), F32)],
        compiler_params=_params(("parallel", "arbitrary")),
        name="tail_prep",
    )(tail, q_norm_g[None], kv_norm_g[None], ln_g[None], ln_b[None], bf_pad)


_MAX_EXACT = N_BUCKETS // 2
_BUCKET_THRESHOLDS = tuple(
    int(math.ceil(_MAX_EXACT * (MAX_DISTANCE / _MAX_EXACT) ** (k / (N_BUCKETS - _MAX_EXACT)) - 1e-9))
    for k in range(1, N_BUCKETS - _MAX_EXACT))


def _bias_body(rb_ref, o_ref, *, t):
    which = pl.program_id(0)
    h = pl.program_id(1)
    row = lax.broadcasted_iota(I32, (t, t), 0)
    col = lax.broadcasted_iota(I32, (t, t), 1)
    d = jnp.maximum(which * t + row - col, 0)
    large = jnp.full((t, t), _MAX_EXACT, I32)
    for thr in _BUCKET_THRESHOLDS:
        large = large + jnp.where(d >= thr, 1, 0)
    bucket = jnp.where(d < _MAX_EXACT, d, large)
    bias = jnp.zeros((t, t), F32)
    for bkt in range(N_BUCKETS):
        bias = jnp.where(bucket == bkt, rb_ref[bkt, h], bias)
    o_ref[0, 0] = bias


def _bias_tiles(rel_bias, nh, t):
    return pl.pallas_call(
        functools.partial(_bias_body, t=t),
        grid=(2, nh),
        in_specs=[pl.BlockSpec(memory_space=pltpu.SMEM)],
        out_specs=pl.BlockSpec((1, 1, t, t), lambda w, h: (w, h, 0, 0)),
        out_shape=jax.ShapeDtypeStruct((2, nh, t, t), F32),
        compiler_params=_params(("parallel", "parallel")),
        name="bias_tiles",
    )(rel_bias)


def _attn_body(qi_ref, kj_ref, *refs, nh, t, scale, mode):
    if mode == "fox":
        q_ref, k_ref, v_ref, g_ref, cum_ref, o_ref, m_ref, l_ref, acc_ref = refs
    else:
        (q_ref, k_ref, v_ref, g_ref, mask_ref, bt_ref, rb_ref,
         o_ref, m_ref, l_ref, acc_ref) = refs
    p = pl.program_id(1)
    qi = qi_ref[p]
    kj = kj_ref[p]

    @pl.when(kj == 0)
    def _():
        m_ref[...] = jnp.full_like(m_ref, NEG_BIG)
        l_ref[...] = jnp.zeros_like(l_ref)
        acc_ref[...] = jnp.zeros_like(acc_ref)

    if mode == "fox":
        row = lax.broadcasted_iota(I32, (t, t), 0) + qi * t
        col = lax.broadcasted_iota(I32, (t, t), 1) + kj * t
        neg = jnp.where(col <= row, 0.0, NEG_BIG).astype(F32)
    else:
        neg = (mask_ref[0, 0].astype(F32) - 1.0) * (-NEG_BIG)
        near = jnp.minimum(qi - kj, 1)
        far = (qi - kj) >= 2

    for h in range(nh):
        sl = slice(h * HEAD_DIM, (h + 1) * HEAD_DIM)
        s = lax.dot_general(q_ref[:, sl], k_ref[:, sl], (((1,), (1,)), ((), ())),
                            preferred_element_type=F32)
        if mode == "fox":
            s = s * scale + (neg - cum_ref[0, h:h + 1, :])
        else:
            bias = jnp.where(far, rb_ref[N_BUCKETS - 1, h], bt_ref[near, h])
            s = s * scale + (neg + bias)
        m_prev = m_ref[h]
        m_next = jnp.maximum(m_prev, jnp.max(s, axis=1, keepdims=True))
        alpha = jnp.exp(m_prev - m_next)
        pr = jnp.exp(s - jnp.tile(m_next, (1, t // LANES)))
        l_ref[h] = alpha * l_ref[h] + jnp.sum(pr, axis=1, keepdims=True)
        acc_ref[h] = acc_ref[h] * alpha + jnp.dot(pr.astype(BF16), v_ref[:, sl],
                                                  preferred_element_type=F32)
        m_ref[h] = m_next

    @pl.when(kj == qi)
    def _():
        ss = jnp.zeros((t, HEAD_DIM), F32)
        for h in range(nh):
            o = acc_ref[h] / l_ref[h]
            acc_ref[h] = o
            ss = ss + o * o
        rinv = lax.rsqrt(jnp.sum(ss, axis=1, keepdims=True) / (nh * HEAD_DIM) + 1e-6)
        for h in range(nh):
            sl = slice(h * HEAD_DIM, (h + 1) * HEAD_DIM)
            o_ref[:, sl] = (acc_ref[h] * rinv * g_ref[:, sl]).astype(o_ref.dtype)


def _attention(mode, b, s, nh, t, q_src, k_src, v_src, gain, *,
               cum_t=None, mask=None, bias_tiles=None, rel_bias=None):
    nq = s // t
    pairs = [(i, j) for i in range(nq) for j in range(i + 1)]
    qi_arr = jnp.asarray(np.array([pq for pq, _ in pairs], np.int32))
    kj_arr = jnp.asarray(np.array([pk for _, pk in pairs], np.int32))
    hw = nh * HEAD_DIM
    n = b * s

    def qmap(cb):
        return lambda bi, p, qi, kj: (bi * nq + qi[p], cb)

    def kmap(cb):
        return lambda bi, p, qi, kj: (bi * nq + kj[p], cb)

    in_specs = [pl.BlockSpec((t, hw), qmap(q_src[1])),
                pl.BlockSpec((t, hw), kmap(k_src[1])),
                pl.BlockSpec((t, hw), kmap(v_src[1])),
                pl.BlockSpec((1, hw), lambda bi, p, qi, kj: (0, 0))]
    args = [q_src[0], k_src[0], v_src[0], gain[None]]
    if mode == "fox":
        in_specs.append(pl.BlockSpec((1, LANES, t), lambda bi, p, qi, kj: (bi, 0, kj[p])))
        args.append(cum_t)
    else:
        in_specs += [pl.BlockSpec((1, 1, t, t), lambda bi, p, qi, kj: (bi, kj[p], qi[p], 0)),
                     pl.BlockSpec(memory_space=pltpu.VMEM),
                     pl.BlockSpec(memory_space=pltpu.SMEM)]
        args += [mask, bias_tiles, rel_bias]
    grid_spec = pltpu.PrefetchScalarGridSpec(
        num_scalar_prefetch=2,
        grid=(b, len(pairs)),
        in_specs=in_specs,
        out_specs=pl.BlockSpec((t, hw), lambda bi, p, qi, kj: (bi * nq + qi[p], 0)),
        scratch_shapes=[pltpu.VMEM((nh, t, LANES), F32), pltpu.VMEM((nh, t, LANES), F32),
                        pltpu.VMEM((nh, t, HEAD_DIM), F32)],
    )
    return pl.pallas_call(
        functools.partial(_attn_body, nh=nh, t=t, scale=HEAD_DIM ** -0.5, mode=mode),
        grid_spec=grid_spec,
        out_shape=jax.ShapeDtypeStruct((n, hw), BF16),
        compiler_params=_params(("parallel", "arbitrary")),
        name=mode + "_attention",
    )(qi_arr, kj_arr, *args)


def _float_key(x):
    bits = lax.bitcast_convert_type(x, I32)
    return bits ^ (lax.shift_right_arithmetic(bits, 31) & 0x7FFFFFFF)


def _indexer_body(q_ref, k_ref, w_ref, o_ref, key_ref, *, ih, w_lane0, tq, ck, nchunk, k_sel):
    i = pl.program_id(1)
    n_valid = ((i + 1) * tq + ck - 1) // ck
    row = lax.broadcasted_iota(I32, (tq, ck), 0) + i * tq
    lane_col = lax.broadcasted_iota(I32, (tq, ck), 1)
    w_all = w_ref[...]

    def score_chunk(c, carry):
        kc = k_ref[0, pl.ds(pl.multiple_of(c * ck, ck), ck), :]
        acc = jnp.zeros((tq, ck), F32)
        for h in range(ih):
            sh = lax.dot_general(q_ref[:, h * IDX_DIM:(h + 1) * IDX_DIM], kc,
                                 (((1,), (1,)), ((), ())), preferred_element_type=F32)
            acc = acc + w_all[:, w_lane0 + h:w_lane0 + h + 1] * jnp.maximum(sh, 0.0)
        acc = jnp.where(lane_col + c * ck <= row, acc, -jnp.inf)
        key_ref[c] = _float_key(acc)
        return carry

    lax.fori_loop(0, n_valid, score_chunk, 0)

    def count(pred_fn):
        def body(c, acc):
            return acc + jnp.where(pred_fn(key_ref[c]), 1, 0)
        acc = lax.fori_loop(0, n_valid, body, jnp.zeros((tq, ck), I32))
        return jnp.sum(acc, axis=1, keepdims=True)

    t0 = jnp.full((tq, 1), INT_MIN, I32)
    zero = jnp.zeros((tq, 1), I32)
    thr = jnp.where(count(lambda kv: kv >= 0) >= k_sel, zero, t0)

    def bit_step(bi, thr):
        cand = thr | lax.shift_left(jnp.int32(1), 30 - bi)
        return jnp.where(count(lambda kv: kv >= cand) >= k_sel, cand, thr)

    thr = lax.fori_loop(0, 31, bit_step, thr)
    need = (k_sel - count(lambda kv: kv > thr)).astype(F32)

    urow = lax.broadcasted_iota(I32, (ck, ck), 0)
    ucol = lax.broadcasted_iota(I32, (ck, ck), 1)
    upper = jnp.where(urow <= ucol, 1.0, 0.0).astype(BF16)
    ones = jnp.ones((ck, LANES), BF16)

    def select_chunk(c, offset):
        kv = key_ref[c]
        eq = kv == thr
        eqb = jnp.where(eq, 1.0, 0.0).astype(BF16)
        rank = jnp.dot(eqb, upper, preferred_element_type=F32) + offset[:, 0:1]
        take = (kv > thr) | (eq & (rank <= need))
        take = take & (lane_col + c * ck <= row)
        o_ref[0, c] = jnp.where(take, 1, 0).astype(jnp.int8)
        return offset + jnp.dot(eqb, ones, preferred_element_type=F32)

    lax.fori_loop(0, n_valid, select_chunk, jnp.zeros((tq, LANES), F32))

    def zero_chunk(c, carry):
        o_ref[0, c] = jnp.zeros((tq, ck), jnp.int8)
        return carry

    lax.fori_loop(n_valid, nchunk, zero_chunk, 0)


def _indexer(q_idx, k_in, small, b, s, ih, w_lane0, ck, k_sel):
    tq = 128
    nq = s // tq
    nchunk = s // ck
    k3 = k_in.reshape(b, s, IDX_DIM)
    return pl.pallas_call(
        functools.partial(_indexer_body, ih=ih, w_lane0=w_lane0, tq=tq, ck=ck, nchunk=nchunk,
                          k_sel=k_sel),
        grid=(b, nq),
        in_specs=[pl.BlockSpec((tq, ih * IDX_DIM), lambda bi, i: (bi * nq + i, 0)),
                  pl.BlockSpec((1, s, IDX_DIM), lambda bi, i: (bi, 0, 0)),
                  pl.BlockSpec((tq, LANES), lambda bi, i: (bi * nq + i, 0))],
        out_specs=pl.BlockSpec((1, nchunk, tq, ck), lambda bi, i: (bi, 0, i, 0)),
        out_shape=jax.ShapeDtypeStruct((b, nchunk, s, ck), jnp.int8),
        scratch_shapes=[pltpu.VMEM((nchunk, tq, ck), I32)],
        compiler_params=_params(("parallel", "parallel")),
        name="indexer_topk",
    )(q_idx, k3, small)


def _layer_norm(h, g, bvec):
    mu = jnp.mean(h, axis=-1, keepdims=True)
    hc = h - mu
    var = jnp.mean(hc * hc, axis=-1, keepdims=True)
    return hc * lax.rsqrt(var + 1e-5) * g + bvec


def _router_body(x_ref, mix_ref, ga_ref, shm_ref, scm_ref, lg_ref, lb_ref, wr_ref, br_ref,
                 x1_ref, up_ref, eidx_ref, rank_ref, gate_ref, cnt_ref, carry_ref,
                 *, alpha_res, ne, nsteps):
    i = pl.program_id(0)
    tm, d = x_ref.shape

    @pl.when(i == 0)
    def _():
        carry_ref[...] = jnp.zeros_like(carry_ref)

    x1 = _layer_norm(alpha_res * x_ref[...] + ga_ref[0] * mix_ref[...], lg_ref[...], lb_ref[...])
    x1_ref[...] = x1
    u = x1 * (1.0 + scm_ref[0]) + shm_ref[0]
    ub = u.astype(BF16).astype(F32)
    lo = lax.bitcast_convert_type(ub[:, :d // 2], I32)
    hi = lax.bitcast_convert_type(ub[:, d // 2:], I32)
    up_ref[...] = lax.shift_right_logical(lo, 16) | (hi & jnp.int32(-65536))

    logits = jnp.dot(u, wr_ref[...], preferred_element_type=F32,
                     precision=lax.Precision.HIGHEST) + br_ref[...]
    lane = lax.broadcasted_iota(I32, (tm, LANES), 1)
    lane_f = lane.astype(F32)
    work = jnp.where(lane < ne, logits, -jnp.inf)
    sel = jnp.zeros((tm, LANES), F32)
    idxs, vals, hits = [], [], []
    for _ in range(TOP_K):
        mx = jnp.max(work, axis=1, keepdims=True)
        ik = jnp.min(jnp.where(work == mx, lane_f, float(LANES)), axis=1, keepdims=True)
        hit = lane_f == ik
        sel = jnp.where(hit, 1.0, sel)
        work = jnp.where(hit, -jnp.inf, work)
        idxs.append(ik.astype(I32))
        vals.append(mx)
        hits.append(hit)
    exps = [jnp.exp(v - vals[0]) for v in vals]
    denom = exps[0] + exps[1] + exps[2] + exps[3]

    row = lax.broadcasted_iota(I32, (tm, tm), 0)
    col = lax.broadcasted_iota(I32, (tm, tm), 1)
    tri = jnp.where(col < row, 1.0, 0.0).astype(BF16)
    before = jnp.dot(tri, sel.astype(BF16), preferred_element_type=F32) + carry_ref[0:1, :]
    total = carry_ref[0:1, :] + jnp.sum(sel, axis=0, keepdims=True)
    carry_ref[0:1, :] = total

    eidx = jnp.zeros((tm, LANES), I32)
    rank = jnp.zeros((tm, LANES), I32)
    gate = jnp.zeros((tm, LANES), F32)
    for kk in range(TOP_K):
        rk = jnp.sum(jnp.where(hits[kk], before, 0.0), axis=1, keepdims=True)
        eidx = jnp.where(lane == kk, idxs[kk], eidx)
        rank = jnp.where(lane == kk, rk.astype(I32), rank)
        gate = jnp.where(lane == kk, exps[kk] / denom, gate)
    eidx_ref[...] = eidx
    rank_ref[...] = rank
    gate_ref[...] = gate
    cnt_ref[...] = jnp.broadcast_to(total, cnt_ref.shape).astype(I32)


def _router(x2d, mix, mod3, ln_g, ln_b, w_router_pad, b_router_pad, s, ne, alpha_res):
    n, d = x2d.shape
    tm = _tile(s, 256, 16)
    nsteps = n // tm
    per_b = s // tm
    row = lambda i: (i, 0)
    const = lambda i: (0, 0)
    modspec = lambda chunk: pl.BlockSpec((1, 1, d), lambda i: (i // per_b, 0, chunk))
    return pl.pallas_call(
        functools.partial(_router_body, alpha_res=alpha_res, ne=ne, nsteps=nsteps),
        grid=(nsteps,),
        in_specs=[pl.BlockSpec((tm, d), row), pl.BlockSpec((tm, d), row),
                  modspec(2), modspec(3), modspec(4),
                  pl.BlockSpec((1, d), const), pl.BlockSpec((1, d), const),
                  pl.BlockSpec((d, LANES), const), pl.BlockSpec((1, LANES), const)],
        out_specs=[pl.BlockSpec((tm, d), row), pl.BlockSpec((tm, d // 2), row),
                   pl.BlockSpec((tm, LANES), row), pl.BlockSpec((tm, LANES), row),
                   pl.BlockSpec((tm, LANES), row), pl.BlockSpec((8, LANES), const)],
        out_shape=[jax.ShapeDtypeStruct((n, d), F32), jax.ShapeDtypeStruct((n, d // 2), I32),
                   jax.ShapeDtypeStruct((n, LANES), I32), jax.ShapeDtypeStruct((n, LANES), I32),
                   jax.ShapeDtypeStruct((n, LANES), F32), jax.ShapeDtypeStruct((8, LANES), I32)],
        scratch_shapes=[pltpu.VMEM((8, LANES), F32)],
        compiler_params=_params(("arbitrary",)),
        name="ln1_router",
    )(x2d, mix, mod3, mod3, mod3, ln_g[None], ln_b[None], w_router_pad, b_router_pad)


def _dispatch_body(dest_ref, u_ref, xs_ref, sem):
    i = pl.program_id(0)
    tc = u_ref.shape[0]

    def copy(j, kk):
        d = dest_ref[(i * tc + j) * TOP_K + kk]
        return pltpu.make_async_copy(u_ref.at[pl.ds(j, 1)], xs_ref.at[pl.ds(d, 1)], sem.at[0])

    def start(j, cc):
        for kk in range(TOP_K):
            copy(j, kk).start()
        return cc

    def wait(j, cc):
        for kk in range(TOP_K):
            copy(j, kk).wait()
        return cc

    lax.fori_loop(0, tc, start, 0)
    lax.fori_loop(0, tc, wait, 0)


def _dispatch(dest_flat, u_packed, n_slots):
    n, dw = u_packed.shape
    tc = _tile(n, 256, 8)
    grid_spec = pltpu.PrefetchScalarGridSpec(
        num_scalar_prefetch=1, grid=(n // tc,),
        in_specs=[pl.BlockSpec((tc, dw), lambda i, dr: (i, 0))],
        out_specs=pl.BlockSpec(memory_space=pl.ANY),
        scratch_shapes=[pltpu.SemaphoreType.DMA((1,))])
    return pl.pallas_call(
        _dispatch_body,
        grid_spec=grid_spec,
        out_shape=jax.ShapeDtypeStruct((n_slots, dw), I32),
        compiler_params=pltpu.CompilerParams(dimension_semantics=("arbitrary",),
                                             has_side_effects=True),
        name="moe_dispatch",
    )(dest_flat, u_packed)


def _ffn_body(te_ref, nsub_ref, nused_ref, x_ref, w1g_ref, w1u_ref, b1g_ref, b1u_ref, w2_ref, b2_ref,
              o_ref, h_ref, *, nf, tf, ts, nr):
    i = pl.program_id(0)
    st = pl.program_id(1)
    tm, dh = x_ref.shape
    nsub = nsub_ref[i]

    def unpack(rows):
        xw = x_ref[0:rows, :]
        lo = lax.bitcast_convert_type(lax.shift_left(xw, 16), F32).astype(BF16)
        hi = lax.bitcast_convert_type(xw & jnp.int32(-65536), F32).astype(BF16)
        return lo, hi

    def up_step(rows):
        lo, hi = unpack(rows)
        def proj(w_ref, b_ref):
            return (jnp.dot(lo, w_ref[0, 0:dh, :].astype(BF16), preferred_element_type=F32)
                    + jnp.dot(hi, w_ref[0, dh:2 * dh, :].astype(BF16), preferred_element_type=F32)
                    + b_ref[0])
        gate = jnp.minimum(proj(w1g_ref, b1g_ref), SWIGLU_LIMIT)
        up = jnp.clip(proj(w1u_ref, b1u_ref), -SWIGLU_LIMIT, SWIGLU_LIMIT)
        act = (up + 1.0) * gate * jax.nn.sigmoid(SWIGLU_ALPHA * gate)
        h_ref[st, 0:rows, :] = act.astype(BF16)

    def down_step(rows):
        acc = jnp.broadcast_to(b2_ref[0], (rows, o_ref.shape[1])).astype(F32)
        for f in range(nf):
            acc = acc + jnp.dot(h_ref[f, 0:rows, :], w2_ref[0, f * tf:(f + 1) * tf, :].astype(BF16),
                                preferred_element_type=F32)
        o_ref[0:rows, :] = acc
        if rows < tm:
            o_ref[rows:tm, :] = jnp.zeros((tm - rows, o_ref.shape[1]), F32)

    for r in range(1, nr + 1):
        @pl.when(jnp.logical_and(nsub == r, st < nf))
        def _(r=r):
            up_step(r * ts)

        @pl.when(jnp.logical_and(nsub == r, st >= nf))
        def _(r=r):
            down_step(r * ts)


def _expert_ffn(xs, w1, b1, w2, b2, tile_e, tile_nsub, n_used, *, tm, ts, n_tiles):
    ne, d, de2 = w1.shape
    de = de2 // 2
    tf = _tile(de, 256)
    td = _tile(d, 512)
    nf, nd = de // tf, d // td
    nstep = nf + nd
    dh = d // 2
    b1r = b1.reshape(ne, 1, de2)
    b2r = b2.reshape(ne, 1, d)

    def tile_idx(i, nu):
        return jnp.minimum(i, nu[0] - 1)

    def step_idx(i, s, nu):
        return jnp.where(i < nu[0], s, nstep - 1)

    def up_map(off):
        return lambda i, s, te, ns, nu: (te[tile_idx(i, nu)], 0,
                                         off + jnp.minimum(step_idx(i, s, nu), nf - 1))

    def down_map(i, s, te, ns, nu):
        return (te[tile_idx(i, nu)], 0, jnp.maximum(step_idx(i, s, nu) - nf, 0))

    grid_spec = pltpu.PrefetchScalarGridSpec(
        num_scalar_prefetch=3,
        grid=(n_tiles, nstep),
        in_specs=[pl.BlockSpec((tm, dh), lambda i, s, te, ns, nu: (tile_idx(i, nu), 0)),
                  pl.BlockSpec((1, d, tf), up_map(0)), pl.BlockSpec((1, d, tf), up_map(nf)),
                  pl.BlockSpec((1, 1, tf), up_map(0)), pl.BlockSpec((1, 1, tf), up_map(nf)),
                  pl.BlockSpec((1, de, td), down_map), pl.BlockSpec((1, 1, td), down_map)],
        out_specs=pl.BlockSpec((tm, td), lambda i, s, te, ns, nu: (
            tile_idx(i, nu), jnp.maximum(step_idx(i, s, nu) - nf, 0))),
        scratch_shapes=[pltpu.VMEM((nf, tm, tf), BF16)],
    )
    return pl.pallas_call(
        functools.partial(_ffn_body, nf=nf, tf=tf, ts=ts, nr=tm // ts),
        grid_spec=grid_spec,
        out_shape=jax.ShapeDtypeStruct((n_tiles * tm, d), F32),
        compiler_params=_params(("arbitrary", "arbitrary")),
        name="expert_ffn",
    )(tile_e, tile_nsub, n_used, xs, w1, w1, b1r, b1r, w2, b2r)


def _combine_body(dest_ref, y_ref, gate_ref, x1_ref, gm_ref, lg_ref, lb_ref, o_ref, ybuf, sem,
                  *, alpha_res):
    i = pl.program_id(0)
    tc = x1_ref.shape[0]

    def copy(j, kk):
        d = dest_ref[(i * tc + j) * TOP_K + kk]
        return pltpu.make_async_copy(y_ref.at[pl.ds(d, 1)], ybuf.at[kk, pl.ds(j, 1)], sem.at[0])

    def start(j, cc):
        for kk in range(TOP_K):
            copy(j, kk).start()
        return cc

    def wait(j, cc):
        for kk in range(TOP_K):
            copy(j, kk).wait()
        return cc

    lax.fori_loop(0, tc, start, 0)
    lax.fori_loop(0, tc, wait, 0)
    g = gate_ref[...]
    y = g[:, 0:1] * ybuf[0]
    for kk in range(1, TOP_K):
        y = y + g[:, kk:kk + 1] * ybuf[kk]
    o_ref[...] = _layer_norm(alpha_res * x1_ref[...] + gm_ref[0] * y, lg_ref[...], lb_ref[...])


def _combine(dest_flat, y_sorted, gate, x1, mod3, ln_g, ln_b, s, alpha_res):
    n, d = x1.shape
    tc = _tile(s, 128, 8)
    per_b = s // tc
    row = lambda i, dr: (i, 0)
    const = lambda i, dr: (0, 0)
    grid_spec = pltpu.PrefetchScalarGridSpec(
        num_scalar_prefetch=1, grid=(n // tc,),
        in_specs=[pl.BlockSpec(memory_space=pl.ANY),
                  pl.BlockSpec((tc, LANES), row), pl.BlockSpec((tc, d), row),
                  pl.BlockSpec((1, 1, d), lambda i, dr: (i // per_b, 0, 5)),
                  pl.BlockSpec((1, d), const), pl.BlockSpec((1, d), const)],
        out_specs=pl.BlockSpec((tc, d), row),
        scratch_shapes=[pltpu.VMEM((TOP_K, tc, d), F32), pltpu.SemaphoreType.DMA((1,))])
    return pl.pallas_call(
        functools.partial(_combine_body, alpha_res=alpha_res),
        grid_spec=grid_spec,
        out_shape=jax.ShapeDtypeStruct((n, d), F32),
        compiler_params=_params(("arbitrary",)),
        name="moe_combine",
    )(dest_flat, y_sorted, gate, x1, mod3, ln_g[None], ln_b[None])


MOE_SUBTILES_PER_TILE = 5


def _moe_tiles(n_tok, n_exp):
    mean_load = n_tok * TOP_K // n_exp
    ts = 16
    while ts * 2 <= min(256, mean_load // 4):
        ts *= 2
    return ts, MOE_SUBTILES_PER_TILE * ts
def _layer(x, c_pad, w_ada, b_ada, w_in, b_forget, q_norm_g, kv_norm_g, kidx_ln_g, kidx_ln_b,
           w_uq, w_uk, w_uv, w_iq, fox_out_g, dsa_out_g, w_o, ln1_g, ln1_b,
           w_router, b_router, w1, b1, w2, b2, ln2_g, ln2_b, rel_bias, alpha_res):
    b, s, d = x.shape
    n = b * s
    fh = b_forget.shape[0]
    ql = q_norm_g.shape[0]
    kvl = kv_norm_g.shape[0]
    dh = w_uk.shape[1]
    ih = w_iq.shape[1] // IDX_DIM
    ne = w_router.shape[1]
    fw = fh * HEAD_DIM
    dw = dh * HEAD_DIM
    assert fw == dw, "head groups share the attention kernel's column-block width"
    assert fh + ih <= LANES and ne <= LANES

    mod = _matmul(c_pad, w_ada, 6 * d, F32, bias=b_ada[None], a_silu=True,
                  tm=16, tn=512, tk=d, name="adaln")
    mod3 = mod[:b].reshape(b, 1, 6 * d)
    u = _modulate(x, mod3, 0, 1).reshape(n, d)

    w_in_t = jnp.swapaxes(w_in, 0, 1)
    qkv = _matmul(u, w_in_t, 3 * fw, BF16, w_t=True, tm=2048, name="proj_qkv")
    tail_w = -(-(ql + kvl + IDX_DIM + fh + ih) // LANES) * LANES
    tail = _matmul(u, w_in_t, tail_w, F32, w_t=True, col0=3 * fw, name="proj_tail")
    cqn, ckvn, kin, small, cum_t = _tail_prep(tail, b, s, ql, kvl, fh, ih, q_norm_g, kv_norm_g,
                                              kidx_ln_g, kidx_ln_b, b_forget)

    t = 256 if s % 256 == 0 else 128
    fox = _attention("fox", b, s, fh, t, (qkv, 0), (qkv, 1), (qkv, 2), fox_out_g, cum_t=cum_t)

    q_d = _matmul(cqn, w_uq, dw, BF16, tm=2048, name="proj_qd")
    q_i = _matmul(cqn, w_iq, ih * IDX_DIM, BF16, tm=2048, name="proj_qi")
    k_d = _matmul(ckvn, w_uk.reshape(kvl, dw), dw, BF16, tm=2048, name="expand_k")
    v_d = _matmul(ckvn, w_uv.reshape(kvl, dw), dw, BF16, tm=2048, name="expand_v")
    k_sel = min(TOPK_MAX, s // 4)
    mask = _indexer(q_i, kin, small, b, s, ih, fh, t, k_sel)
    bias_t = _bias_tiles(rel_bias, dh, t)
    dsa = _attention("dsa", b, s, dh, t, (q_d, 0), (k_d, 0), (v_d, 0), dsa_out_g,
                     mask=mask, bias_tiles=bias_t, rel_bias=rel_bias)

    mix = _matmul(fox, w_o, d, F32, a2=dsa, name="proj_out")

    wr_pad = jnp.zeros((d, LANES), F32).at[:, :ne].set(w_router)
    br_pad = jnp.zeros((1, LANES), F32).at[0, :ne].set(b_router)
    x1, u_packed, eidx, rank, gate, cnt = _router(x.reshape(n, d), mix, mod3, ln1_g, ln1_b,
                                                  wr_pad, br_pad, s, ne, alpha_res)
    ts, tm = _moe_tiles(n, ne)
    n_tiles = -(-(n * TOP_K) // tm) + ne
    counts = cnt[0, :ne]
    ntile = (counts + tm - 1) // tm
    tend = jnp.cumsum(ntile)
    tstart = tend - ntile
    n_used = tend[-1]
    dest = (tstart * tm)[eidx[:, :TOP_K]] + rank[:, :TOP_K]
    dest_flat = dest.reshape(n * TOP_K).astype(I32)
    tid = jnp.minimum(jnp.arange(n_tiles, dtype=I32), n_used - 1)
    tile_e = jnp.minimum(jnp.searchsorted(tend, tid, side="right"), ne - 1).astype(I32)
    valid = jnp.clip(counts[tile_e] - (tid - tstart[tile_e]) * tm, 0, tm)
    valid = jnp.where(jnp.arange(n_tiles) < n_used, valid, 0)
    tile_nsub = ((valid + ts - 1) // ts).astype(I32)
    xs = _dispatch(dest_flat, u_packed, n_tiles * tm)
    y_sorted = _expert_ffn(xs, w1, b1, w2, b2, tile_e, tile_nsub,
                           n_used.reshape(1).astype(I32), tm=tm, ts=ts, n_tiles=n_tiles)
    out = _combine(dest_flat, y_sorted, gate, x1, mod3, ln2_g, ln2_b, s, alpha_res)
    return out.reshape(b, s, d)


def kernel(x, c, w_ada, b_ada, w_in, b_forget, q_norm_g, kv_norm_g, kidx_ln_g, kidx_ln_b, w_uq, w_uk, w_uv, w_iq, fox_out_g, dsa_out_g, w_o, ln1_g, ln1_b, w_router, b_router, w1, b1, w2, b2, ln2_g, ln2_b, rel_bias):
    depth = w_ada.shape[0]
    alpha_res = (2 * depth) ** 0.25
    b, d = c.shape
    c_pad = jnp.zeros((16, d), F32).at[:b].set(c)
    for l in range(depth):
        x = _layer(x, c_pad, w_ada[l], b_ada[l], w_in[l], b_forget[l], q_norm_g[l], kv_norm_g[l],
                   kidx_ln_g[l], kidx_ln_b[l], w_uq[l], w_uk[l], w_uv[l], w_iq[l], fox_out_g[l],
                   dsa_out_g[l], w_o[l], ln1_g[l], ln1_b[l], w_router[l], b_router[l],
                   w1[l], b1[l], w2[l], b2[l], ln2_g[l], ln2_b[l], rel_bias, alpha_res)
    return x
```

```python
import functools
import math

import numpy as np
import jax
import jax.numpy as jnp
from jax import lax
from jax.experimental import pallas as pl
from jax.experimental.pallas import tpu as pltpu

F32 = jnp.float32
BF16 = jnp.bfloat16
I32 = jnp.int32

LANES = 128
HEAD_DIM = 128
IDX_DIM = 128
TOPK_MAX = 256
N_BUCKETS = 32
MAX_DISTANCE = 128
TOP_K = 4
SWIGLU_LIMIT = 7.0
SWIGLU_ALPHA = 1.702
NEG_BIG = -1e30
LOG2E = math.log2(math.e)
VMEM_LIMIT = 56 * 1024 * 1024
INT_MIN = -2 ** 31


def _tile(n, pref, unit=LANES):
    if n <= pref:
        return n
    t = (pref // unit) * unit
    while t > unit and n % t:
        t -= unit
    assert n % t == 0, (n, pref)
    return t


def _pack_bf16_pair(lo, hi):
    lo_bits = lax.bitcast_convert_type(lo.astype(BF16).astype(F32), I32)
    hi_bits = lax.bitcast_convert_type(hi.astype(BF16).astype(F32), I32)
    return lax.shift_right_logical(lo_bits, 16) | (hi_bits & jnp.int32(-65536))


def _unpack_bf16_pair(words):
    lo = lax.bitcast_convert_type(lax.shift_left(words, 16), F32)
    hi = lax.bitcast_convert_type(words & jnp.int32(-65536), F32)
    return lo, hi


def _params(sem, vmem=VMEM_LIMIT):
    return pltpu.CompilerParams(dimension_semantics=sem, vmem_limit_bytes=vmem)


def _mm_body(*refs, nk, nk1, a_silu, has_bias, col_block0, w_cols_if_overhang, w_t, out_scale):
    refs = list(refs)
    a_ref = refs.pop(0)
    a2_ref = refs.pop(0) if nk1 < nk else None
    w_ref = refs.pop(0)
    b_ref = refs.pop(0) if has_bias else None
    o_ref, acc_ref = refs
    k = pl.program_id(2)

    @pl.when(k == 0)
    def _():
        acc_ref[...] = jnp.zeros_like(acc_ref)

    def accumulate(src_ref):
        a = src_ref[...]
        if a_silu:
            a = a.astype(F32)
            a = a * jax.nn.sigmoid(a)
        w = w_ref[...]
        n_axis = 0 if w_t else 1
        if w_cols_if_overhang is not None:
            tn = w.shape[n_axis]
            col = lax.broadcasted_iota(I32, w.shape, n_axis) + (pl.program_id(1) + col_block0) * tn
            w = jnp.where(col < w_cols_if_overhang, w, 0.0)
        acc_ref[...] += lax.dot_general(a.astype(BF16), w.astype(BF16),
                                        (((1,), (1 if w_t else 0,)), ((), ())),
                                        preferred_element_type=F32)

    if a2_ref is None:
        accumulate(a_ref)
    else:
        pl.when(k < nk1)(lambda: accumulate(a_ref))
        pl.when(k >= nk1)(lambda: accumulate(a2_ref))

    @pl.when(k == nk - 1)
    def _():
        r = acc_ref[...]
        if has_bias:
            r = r + b_ref[...]
        if out_scale is not None:
            n_blocks, factor = out_scale
            r = r * jnp.where(pl.program_id(1) < n_blocks, factor, 1.0)
        o_ref[...] = r.astype(o_ref.dtype)


def _matmul(a, w, n_out, out_dtype, *, a2=None, bias=None, a_silu=False, col0=0, w_t=False,
            scale_cols=None, tm=1024, tn=1024, tk=1024, name="matmul"):
    m, kdim = a.shape
    w_cols = w.shape[0] if w_t else w.shape[1]
    tm = _tile(m, tm, 16)
    tn = _tile(math.gcd(n_out, col0, scale_cols[0] if scale_cols else 0), tn)
    tk = _tile(kdim, tk)
    nk1 = kdim // tk
    nk = nk1 if a2 is None else nk1 + a2.shape[1] // tk
    col_block0 = col0 // tn
    overhang = col0 + n_out > w_cols
    out_scale = None
    if scale_cols is not None:
        assert scale_cols[0] % tn == 0
        out_scale = (scale_cols[0] // tn, scale_cols[1])
    in_specs = [pl.BlockSpec((tm, tk), lambda i, j, k: (i, jnp.minimum(k, nk1 - 1)))]
    args = [a]
    if a2 is not None:
        assert a2.shape[1] % tk == 0
        in_specs.append(pl.BlockSpec((tm, tk), lambda i, j, k: (i, jnp.maximum(k - nk1, 0))))
        args.append(a2)
    if w_t:
        in_specs.append(pl.BlockSpec((tn, tk), lambda i, j, k: (j + col_block0, k)))
    else:
        in_specs.append(pl.BlockSpec((tk, tn), lambda i, j, k: (k, j + col_block0)))
    args.append(w)
    if bias is not None:
        in_specs.append(pl.BlockSpec((1, tn), lambda i, j, k: (0, j)))
        args.append(bias)
    return pl.pallas_call(
        functools.partial(_mm_body, nk=nk, nk1=nk1, a_silu=a_silu, has_bias=bias is not None,
                          col_block0=col_block0, w_t=w_t, out_scale=out_scale,
                          w_cols_if_overhang=w_cols if overhang else None),
        grid=(m // tm, n_out // tn, nk),
        in_specs=in_specs,
        out_specs=pl.BlockSpec((tm, tn), lambda i, j, k: (i, j)),
        out_shape=jax.ShapeDtypeStruct((m, n_out), out_dtype),
        scratch_shapes=[pltpu.VMEM((tm, tn), F32)],
        compiler_params=_params(("parallel", "parallel", "arbitrary")),
        name=name,
    )(*args)


def _modulate_body(x_ref, sh_ref, sc_ref, o_ref):
    o_ref[0] = (x_ref[0] * (1.0 + sc_ref[0]) + sh_ref[0]).astype(o_ref.dtype)


def _modulate(x, mod3, shift_chunk, scale_chunk):
    b, s, d = x.shape
    ts = _tile(s, 512, 16)
    return pl.pallas_call(
        _modulate_body,
        grid=(b, s // ts),
        in_specs=[pl.BlockSpec((1, ts, d), lambda i, j: (i, j, 0)),
                  pl.BlockSpec((1, 1, d), lambda i, j: (i, 0, shift_chunk)),
                  pl.BlockSpec((1, 1, d), lambda i, j: (i, 0, scale_chunk))],
        out_specs=pl.BlockSpec((1, ts, d), lambda i, j: (i, j, 0)),
        out_shape=jax.ShapeDtypeStruct((b, s, d), BF16),
        compiler_params=_params(("parallel", "parallel")),
        name="modulate",
    )(x, mod3, mod3)


def _tail_body(t_ref, qg_ref, kvg_ref, lng_ref, lnb_ref, bf_ref,
               cq_ref, ckv_ref, kin_ref, small_ref, cumt_ref, carry_ref, *, fh, ql, kvl, w_scale):
    j = pl.program_id(1)
    ts = t_ref.shape[0]

    @pl.when(j == 0)
    def _():
        carry_ref[...] = jnp.zeros_like(carry_ref)

    cq = t_ref[:, fh:fh + ql]
    cq_ref[...] = (cq * lax.rsqrt(jnp.mean(cq * cq, axis=-1, keepdims=True) + 1e-6)
                   * qg_ref[...]).astype(cq_ref.dtype)
    ckv = t_ref[:, fh + ql:fh + ql + kvl]
    ckv_ref[...] = (ckv * lax.rsqrt(jnp.mean(ckv * ckv, axis=-1, keepdims=True) + 1e-6)
                    * kvg_ref[...]).astype(ckv_ref.dtype)
    ki = t_ref[:, fh + ql + kvl:fh + ql + kvl + IDX_DIM]
    mu = jnp.mean(ki, axis=-1, keepdims=True)
    kc = ki - mu
    var = jnp.mean(kc * kc, axis=-1, keepdims=True)
    kin_ref[...] = (kc * lax.rsqrt(var + 1e-5) * lng_ref[...] + lnb_ref[...]).astype(kin_ref.dtype)

    small_ref[...] = t_ref[:, ql + kvl + IDX_DIM:ql + kvl + IDX_DIM + LANES] * w_scale
    z = t_ref[:, 0:LANES] + bf_ref[...]
    log_f = jnp.minimum(z, 0.0) - jnp.log(1.0 + jnp.exp(-jnp.abs(z)))
    row = lax.broadcasted_iota(I32, (ts, ts), 0)
    col = lax.broadcasted_iota(I32, (ts, ts), 1)
    tri = jnp.where(col <= row, 1.0, 0.0).astype(F32)
    incl = jnp.dot(tri, log_f, preferred_element_type=F32,
                   precision=lax.Precision.HIGHEST) + carry_ref[0:1, :]
    carry_ref[0:1, :] = incl[ts - 1:ts, :]
    cumt_ref[0] = (incl * LOG2E).T


def _tail_prep(tail, b, s, ql, kvl, fh, ih, q_norm_g, kv_norm_g, ln_g, ln_b, b_forget):
    n = tail.shape[0]
    assert ql % LANES == 0 and kvl % LANES == 0
    ts = _tile(s, 256)
    nj = s // ts
    bf_pad = jnp.zeros((1, LANES), F32).at[0, :fh].set(b_forget)
    w_scale = (ih ** -0.5) * (IDX_DIM ** -0.5)
    row = lambda i, j: (i * nj + j, 0)
    const = lambda i, j: (0, 0)
    return pl.pallas_call(
        functools.partial(_tail_body, fh=fh, ql=ql, kvl=kvl, w_scale=w_scale),
        grid=(b, nj),
        in_specs=[pl.BlockSpec((ts, tail.shape[1]), row),
                  pl.BlockSpec((1, ql), const), pl.BlockSpec((1, kvl), const),
                  pl.BlockSpec((1, IDX_DIM), const), pl.BlockSpec((1, IDX_DIM), const),
                  pl.BlockSpec((1, LANES), const)],
        out_specs=[pl.BlockSpec((ts, ql), row), pl.BlockSpec((ts, kvl), row),
                   pl.BlockSpec((ts, IDX_DIM), row), pl.BlockSpec((ts, LANES), row),
                   pl.BlockSpec((1, LANES, ts), lambda i, j: (i, 0, j))],
        out_shape=[jax.ShapeDtypeStruct((n, ql), BF16), jax.ShapeDtypeStruct((n, kvl), BF16),
                   jax.ShapeDtypeStruct((n, IDX_DIM), BF16), jax.ShapeDtypeStruct((n, LANES), F32),
                   jax.ShapeDtypeStruct((b, LANES, s), F32)],
        scratch_shapes=[pltpu.VMEM((8, LANES), F32)],
        compiler_params=_params(("parallel", "arbitrary")),
        name="tail_prep",
    )(tail, q_norm_g[None], kv_norm_g[None], ln_g[None], ln_b[None], bf_pad)


_MAX_EXACT = N_BUCKETS // 2
_BUCKET_THRESHOLDS = tuple(
    int(math.ceil(_MAX_EXACT * (MAX_DISTANCE / _MAX_EXACT) ** (k / (N_BUCKETS - _MAX_EXACT)) - 1e-9))
    for k in range(1, N_BUCKETS - _MAX_EXACT))


def _bias_body(rb_ref, o_ref, *, t):
    which = pl.program_id(0)
    h = pl.program_id(1)
    row = lax.broadcasted_iota(I32, (t, t), 0)
    col = lax.broadcasted_iota(I32, (t, t), 1)
    d = jnp.maximum(which * t + row - col, 0)
    large = jnp.full((t, t), _MAX_EXACT, I32)
    for thr in _BUCKET_THRESHOLDS:
        large = large + jnp.where(d >= thr, 1, 0)
    bucket = jnp.where(d < _MAX_EXACT, d, large)
    bias = jnp.zeros((t, t), F32)
    for bkt in range(N_BUCKETS):
        bias = jnp.where(bucket == bkt, rb_ref[bkt, h], bias)
    o_ref[0, 0] = (bias - rb_ref[N_BUCKETS - 1, h]) * LOG2E


def _bias_tiles(rel_bias, nh, t):
    assert 2 * t - (t - 1) >= _BUCKET_THRESHOLDS[-1], "key tiles two or more away are all last-bucket"
    return pl.pallas_call(
        functools.partial(_bias_body, t=t),
        grid=(2, nh),
        in_specs=[pl.BlockSpec(memory_space=pltpu.SMEM)],
        out_specs=pl.BlockSpec((1, 1, t, t), lambda w, h: (w, h, 0, 0)),
        out_shape=jax.ShapeDtypeStruct((2, nh, t, t), F32),
        compiler_params=_params(("parallel", "parallel")),
        name="bias_tiles",
    )(rel_bias)


ATTN_ROWS = 128


def _attn_body(qi_ref, kj_ref, *refs, nh, t, mode):
    if mode == "fox":
        q_ref, k_ref, v_ref, g_ref, cum_ref, o_ref, m_ref, l_ref, acc_ref, neg_ref = refs
    else:
        q_ref, k_ref, v_ref, g_ref, mask_ref, bt_ref, o_ref, m_ref, l_ref, acc_ref, neg_ref = refs
    p = pl.program_id(1)
    qi = qi_ref[p]
    kj = kj_ref[p]

    @pl.when(kj == 0)
    def _():
        m_ref[...] = jnp.full_like(m_ref, NEG_BIG)
        l_ref[...] = jnp.zeros_like(l_ref)
        acc_ref[...] = jnp.zeros_like(acc_ref)

    def heads(addend):
        for h in range(nh):
            sl = slice(h * HEAD_DIM, (h + 1) * HEAD_DIM)
            for r in range(t // ATTN_ROWS):
                rows = slice(r * ATTN_ROWS, (r + 1) * ATTN_ROWS)
                s = lax.dot_general(q_ref[rows, sl], k_ref[:, sl], (((1,), (1,)), ((), ())),
                                    preferred_element_type=F32)
                s = s + addend(h, rows)
                m_prev = m_ref[h, rows]
                m_next = jnp.maximum(m_prev, jnp.max(s, axis=1, keepdims=True))
                alpha = jnp.exp2(m_prev - m_next)
                pr = jnp.exp2(s - jnp.tile(m_next, (1, t // LANES)))
                l_ref[h, rows] = alpha * l_ref[h, rows] + jnp.sum(pr, axis=1, keepdims=True)
                acc_ref[h, rows] = acc_ref[h, rows] * alpha + jnp.dot(
                    pr.astype(BF16), v_ref[:, sl], preferred_element_type=F32)
                m_ref[h, rows] = m_next

    if mode == "fox":
        @pl.when(kj == qi)
        def _():
            row = lax.broadcasted_iota(I32, (t, t), 0)
            col = lax.broadcasted_iota(I32, (t, t), 1)
            neg_ref[...] = jnp.where(col <= row, 0.0, NEG_BIG).astype(F32)
            heads(lambda h, rows: neg_ref[rows, :] - cum_ref[0, h:h + 1, :])

        @pl.when(kj != qi)
        def _():
            heads(lambda h, rows: -cum_ref[0, h:h + 1, :])
    else:
        neg_ref[...] = (mask_ref[0, 0].astype(F32) - 1.0) * (-NEG_BIG)

        @pl.when(qi - kj >= 2)
        def _():
            heads(lambda h, rows: neg_ref[rows, :])

        @pl.when(qi - kj < 2)
        def _():
            near = qi - kj
            heads(lambda h, rows: neg_ref[rows, :] + bt_ref[near, h, rows, :])

    @pl.when(kj == qi)
    def _():
        ss = jnp.zeros((t, HEAD_DIM), F32)
        for h in range(nh):
            o = acc_ref[h] / l_ref[h]
            acc_ref[h] = o
            ss = ss + o * o
        rinv = lax.rsqrt(jnp.sum(ss, axis=1, keepdims=True) / (nh * HEAD_DIM) + 1e-6)
        for h in range(nh):
            sl = slice(h * HEAD_DIM, (h + 1) * HEAD_DIM)
            o_ref[:, sl] = (acc_ref[h] * rinv * g_ref[:, sl]).astype(o_ref.dtype)


def _attention(mode, b, s, nh, t, q_src, k_src, v_src, gain, *,
               cum_t=None, mask=None, bias_tiles=None):
    nq = s // t
    pairs = [(i, j) for i in range(nq) for j in range(i + 1)]
    qi_arr = jnp.asarray(np.array([pq for pq, _ in pairs], np.int32))
    kj_arr = jnp.asarray(np.array([pk for _, pk in pairs], np.int32))
    hw = nh * HEAD_DIM
    n = b * s

    def qmap(cb):
        return lambda bi, p, qi, kj: (bi * nq + qi[p], cb)

    def kmap(cb):
        return lambda bi, p, qi, kj: (bi * nq + kj[p], cb)

    in_specs = [pl.BlockSpec((t, hw), qmap(q_src[1])),
                pl.BlockSpec((t, hw), kmap(k_src[1])),
                pl.BlockSpec((t, hw), kmap(v_src[1])),
                pl.BlockSpec((1, hw), lambda bi, p, qi, kj: (0, 0))]
    args = [q_src[0], k_src[0], v_src[0], gain[None]]
    if mode == "fox":
        in_specs.append(pl.BlockSpec((1, LANES, t), lambda bi, p, qi, kj: (bi, 0, kj[p])))
        args.append(cum_t)
    else:
        in_specs += [pl.BlockSpec((1, 1, t, t), lambda bi, p, qi, kj: (bi, kj[p], qi[p], 0)),
                     pl.BlockSpec(memory_space=pltpu.VMEM)]
        args += [mask, bias_tiles]
    grid_spec = pltpu.PrefetchScalarGridSpec(
        num_scalar_prefetch=2,
        grid=(b, len(pairs)),
        in_specs=in_specs,
        out_specs=pl.BlockSpec((t, hw), lambda bi, p, qi, kj: (bi * nq + qi[p], 0)),
        scratch_shapes=[pltpu.VMEM((nh, t, LANES), F32), pltpu.VMEM((nh, t, LANES), F32),
                        pltpu.VMEM((nh, t, HEAD_DIM), F32), pltpu.VMEM((t, t), F32)],
    )
    return pl.pallas_call(
        functools.partial(_attn_body, nh=nh, t=t, mode=mode),
        grid_spec=grid_spec,
        out_shape=jax.ShapeDtypeStruct((n, hw), BF16),
        compiler_params=_params(("parallel", "arbitrary")),
        name=mode + "_attention",
    )(qi_arr, kj_arr, *args)


IDX_ROWS = 128


def _float_key(x):
    bits = lax.bitcast_convert_type(x, I32)
    return bits ^ (lax.shift_right_arithmetic(bits, 31) & 0x7FFFFFFF)


def _indexer_body(q_ref, k_ref, w_ref, o_ref, key_ref, *, ih, w_lane0, tq, ck, nchunk, k_sel):
    i = pl.program_id(1)
    n_valid = ((i + 1) * tq + ck - 1) // ck
    row = lax.broadcasted_iota(I32, (tq, ck), 0) + i * tq
    lane_col = lax.broadcasted_iota(I32, (tq, ck), 1)

    def score_chunk(c, carry):
        kc = k_ref[0, pl.ds(pl.multiple_of(c * ck, ck), ck), :]
        for r in range(tq // IDX_ROWS):
            rows = slice(r * IDX_ROWS, (r + 1) * IDX_ROWS)
            w_blk = w_ref[rows, :]
            acc = jnp.zeros((IDX_ROWS, ck), F32)
            for h in range(ih):
                sh = lax.dot_general(q_ref[rows, h * IDX_DIM:(h + 1) * IDX_DIM], kc,
                                     (((1,), (1,)), ((), ())), preferred_element_type=F32)
                acc = acc + w_blk[:, w_lane0 + h:w_lane0 + h + 1] * jnp.maximum(sh, 0.0)
            causal = (lax.broadcasted_iota(I32, (IDX_ROWS, ck), 1) + c * ck
                      <= lax.broadcasted_iota(I32, (IDX_ROWS, ck), 0) + (i * tq + r * IDX_ROWS))
            acc = jnp.where(causal, acc, -jnp.inf)
            key_ref[c, rows, :] = _float_key(acc)
        return carry

    lax.fori_loop(0, n_valid, score_chunk, 0)

    def count(pred_fn):
        def body(c, acc):
            ind = jnp.where(pred_fn(key_ref[c]), 1, 0)
            for part in range(ck // LANES):
                acc = acc + ind[:, part * LANES:(part + 1) * LANES]
            return acc
        acc = lax.fori_loop(0, n_valid, body, jnp.zeros((tq, LANES), I32))
        return jnp.sum(acc, axis=1, keepdims=True)

    t0 = jnp.full((tq, 1), INT_MIN, I32)
    zero = jnp.zeros((tq, 1), I32)
    thr = jnp.where(count(lambda kv: kv >= 0) >= k_sel, zero, t0)

    def bit_step(bi, thr):
        cand = thr | lax.shift_left(jnp.int32(1), 30 - bi)
        return jnp.where(count(lambda kv: kv >= cand) >= k_sel, cand, thr)

    thr = lax.fori_loop(0, 31, bit_step, thr)
    need = (k_sel - count(lambda kv: kv > thr)).astype(F32)

    urow = lax.broadcasted_iota(I32, (ck, ck), 0)
    ucol = lax.broadcasted_iota(I32, (ck, ck), 1)
    upper = jnp.where(urow <= ucol, 1.0, 0.0).astype(BF16)
    ones = jnp.ones((ck, LANES), BF16)

    def select_chunk(c, offset):
        kv = key_ref[c]
        eq = kv == thr
        eqb = jnp.where(eq, 1.0, 0.0).astype(BF16)
        rank = jnp.dot(eqb, upper, preferred_element_type=F32) + offset[:, 0:1]
        take = (kv > thr) | (eq & (rank <= need))
        take = take & (lane_col + c * ck <= row)
        o_ref[0, c] = jnp.where(take, 1, 0).astype(jnp.int8)
        return offset + jnp.dot(eqb, ones, preferred_element_type=F32)

    lax.fori_loop(0, n_valid, select_chunk, jnp.zeros((tq, LANES), F32))

    def zero_chunk(c, carry):
        o_ref[0, c] = jnp.zeros((tq, ck), jnp.int8)
        return carry

    lax.fori_loop(n_valid, nchunk, zero_chunk, 0)


def _indexer(q_idx, k_in, small, b, s, ih, w_lane0, ck, k_sel):
    tq = _tile(s, 256, IDX_ROWS)
    nq = s // tq
    nchunk = s // ck
    k3 = k_in.reshape(b, s, IDX_DIM)
    return pl.pallas_call(
        functools.partial(_indexer_body, ih=ih, w_lane0=w_lane0, tq=tq, ck=ck, nchunk=nchunk,
                          k_sel=k_sel),
        grid=(b, nq),
        in_specs=[pl.BlockSpec((tq, ih * IDX_DIM), lambda bi, i: (bi * nq + i, 0)),
                  pl.BlockSpec((1, s, IDX_DIM), lambda bi, i: (bi, 0, 0)),
                  pl.BlockSpec((tq, LANES), lambda bi, i: (bi * nq + i, 0))],
        out_specs=pl.BlockSpec((1, nchunk, tq, ck), lambda bi, i: (bi, 0, i, 0)),
        out_shape=jax.ShapeDtypeStruct((b, nchunk, s, ck), jnp.int8),
        scratch_shapes=[pltpu.VMEM((nchunk, tq, ck), I32)],
        compiler_params=_params(("parallel", "parallel")),
        name="indexer_topk",
    )(q_idx, k3, small)


def _layer_norm(h, g, bvec):
    mu = jnp.mean(h, axis=-1, keepdims=True)
    hc = h - mu
    var = jnp.mean(hc * hc, axis=-1, keepdims=True)
    return hc * lax.rsqrt(var + 1e-5) * g + bvec


def _router_body(x_ref, mix_ref, ga_ref, shm_ref, scm_ref, lg_ref, lb_ref, wr_ref, br_ref,
                 x1_ref, up_ref, eidx_ref, rank_ref, gate_ref, cnt_ref, carry_ref,
                 *, alpha_res, ne, nsteps):
    i = pl.program_id(0)
    tm, d = x_ref.shape

    @pl.when(i == 0)
    def _():
        carry_ref[...] = jnp.zeros_like(carry_ref)

    x1 = _layer_norm(alpha_res * x_ref[...] + ga_ref[0] * mix_ref[...], lg_ref[...], lb_ref[...])
    x1_ref[...] = x1
    u = x1 * (1.0 + scm_ref[0]) + shm_ref[0]
    up_ref[...] = _pack_bf16_pair(u[:, :d // 2], u[:, d // 2:])

    logits = jnp.dot(u, wr_ref[...], preferred_element_type=F32,
                     precision=lax.Precision.HIGHEST) + br_ref[...]
    lane = lax.broadcasted_iota(I32, (tm, LANES), 1)
    lane_f = lane.astype(F32)
    work = jnp.where(lane < ne, logits, -jnp.inf)
    sel = jnp.zeros((tm, LANES), F32)
    idxs, vals, hits = [], [], []
    for _ in range(TOP_K):
        mx = jnp.max(work, axis=1, keepdims=True)
        ik = jnp.min(jnp.where(work == mx, lane_f, float(LANES)), axis=1, keepdims=True)
        hit = lane_f == ik
        sel = jnp.where(hit, 1.0, sel)
        work = jnp.where(hit, -jnp.inf, work)
        idxs.append(ik.astype(I32))
        vals.append(mx)
        hits.append(hit)
    exps = [jnp.exp(v - vals[0]) for v in vals]
    denom = exps[0] + exps[1] + exps[2] + exps[3]

    row = lax.broadcasted_iota(I32, (tm, tm), 0)
    col = lax.broadcasted_iota(I32, (tm, tm), 1)
    tri = jnp.where(col < row, 1.0, 0.0).astype(BF16)
    before = jnp.dot(tri, sel.astype(BF16), preferred_element_type=F32) + carry_ref[0:1, :]
    total = carry_ref[0:1, :] + jnp.sum(sel, axis=0, keepdims=True)
    carry_ref[0:1, :] = total

    eidx = jnp.zeros((tm, LANES), I32)
    rank = jnp.zeros((tm, LANES), I32)
    gate = jnp.zeros((tm, LANES), F32)
    for kk in range(TOP_K):
        rk = jnp.sum(jnp.where(hits[kk], before, 0.0), axis=1, keepdims=True)
        eidx = jnp.where(lane == kk, idxs[kk], eidx)
        rank = jnp.where(lane == kk, rk.astype(I32), rank)
        gate = jnp.where(lane == kk, exps[kk] / denom, gate)
    eidx_ref[...] = eidx
    rank_ref[...] = rank
    gate_ref[...] = gate
    cnt_ref[...] = jnp.broadcast_to(total, cnt_ref.shape).astype(I32)


def _router(x2d, mix, mod3, ln_g, ln_b, w_router_pad, b_router_pad, s, ne, alpha_res):
    n, d = x2d.shape
    tm = _tile(s, 256, 16)
    nsteps = n // tm
    per_b = s // tm
    row = lambda i: (i, 0)
    const = lambda i: (0, 0)
    modspec = lambda chunk: pl.BlockSpec((1, 1, d), lambda i: (i // per_b, 0, chunk))
    return pl.pallas_call(
        functools.partial(_router_body, alpha_res=alpha_res, ne=ne, nsteps=nsteps),
        grid=(nsteps,),
        in_specs=[pl.BlockSpec((tm, d), row), pl.BlockSpec((tm, d), row),
                  modspec(2), modspec(3), modspec(4),
                  pl.BlockSpec((1, d), const), pl.BlockSpec((1, d), const),
                  pl.BlockSpec((d, LANES), const), pl.BlockSpec((1, LANES), const)],
        out_specs=[pl.BlockSpec((tm, d), row), pl.BlockSpec((tm, d // 2), row),
                   pl.BlockSpec((tm, LANES), row), pl.BlockSpec((tm, LANES), row),
                   pl.BlockSpec((tm, LANES), row), pl.BlockSpec((8, LANES), const)],
        out_shape=[jax.ShapeDtypeStruct((n, d), F32), jax.ShapeDtypeStruct((n, d // 2), I32),
                   jax.ShapeDtypeStruct((n, LANES), I32), jax.ShapeDtypeStruct((n, LANES), I32),
                   jax.ShapeDtypeStruct((n, LANES), F32), jax.ShapeDtypeStruct((8, LANES), I32)],
        scratch_shapes=[pltpu.VMEM((8, LANES), F32)],
        compiler_params=_params(("arbitrary",)),
        name="ln1_router",
    )(x2d, mix, mod3, mod3, mod3, ln_g[None], ln_b[None], w_router_pad, b_router_pad)


def _dispatch_body(dest_ref, u_ref, xs_ref, sem):
    i = pl.program_id(0)
    tc = u_ref.shape[0]

    def start(j, cc):
        for kk in range(TOP_K):
            d = dest_ref[(i * tc + j) * TOP_K + kk]
            pltpu.make_async_copy(u_ref.at[pl.ds(j, 1)], xs_ref.at[pl.ds(d, 1)], sem.at[0]).start()
        return cc

    lax.fori_loop(0, tc, start, 0)
    for _ in range(TOP_K):
        pltpu.make_async_copy(u_ref, xs_ref.at[pl.ds(0, tc)], sem.at[0]).wait()


def _dispatch(dest_flat, u_packed, n_slots):
    n, dw = u_packed.shape
    tc = _tile(n, 256, 8)
    grid_spec = pltpu.PrefetchScalarGridSpec(
        num_scalar_prefetch=1, grid=(n // tc,),
        in_specs=[pl.BlockSpec((tc, dw), lambda i, dr: (i, 0))],
        out_specs=pl.BlockSpec(memory_space=pl.ANY),
        scratch_shapes=[pltpu.SemaphoreType.DMA((1,))])
    return pl.pallas_call(
        _dispatch_body,
        grid_spec=grid_spec,
        out_shape=jax.ShapeDtypeStruct((n_slots, dw), I32),
        compiler_params=pltpu.CompilerParams(dimension_semantics=("arbitrary",),
                                             has_side_effects=True),
        name="moe_dispatch",
    )(dest_flat, u_packed)


def _ffn_body(te_ref, nsub_ref, nused_ref, x_ref, w1g_ref, w1u_ref, b1g_ref, b1u_ref,
              w2lo_ref, w2hi_ref, b2lo_ref, b2hi_ref, o_ref, h_ref, *, nf, tf, ts, nr):
    i = pl.program_id(0)
    st = pl.program_id(1)
    tm, dh = x_ref.shape
    nsub = nsub_ref[i]

    def up_step(rows):
        lo, hi = _unpack_bf16_pair(x_ref[0:rows, :])
        lo, hi = lo.astype(BF16), hi.astype(BF16)

        def proj(w_ref, b_ref):
            return (jnp.dot(lo, w_ref[0, 0:dh, :].astype(BF16), preferred_element_type=F32)
                    + jnp.dot(hi, w_ref[0, dh:2 * dh, :].astype(BF16), preferred_element_type=F32)
                    + b_ref[0])
        gate = jnp.minimum(proj(w1g_ref, b1g_ref), SWIGLU_LIMIT)
        up = jnp.clip(proj(w1u_ref, b1u_ref), -SWIGLU_LIMIT, SWIGLU_LIMIT)
        act = (up + 1.0) * gate * jax.nn.sigmoid(SWIGLU_ALPHA * gate)
        h_ref[st, 0:rows, :] = act.astype(BF16)

    def down_step(rows):
        def proj(w_ref, b_ref):
            acc = jnp.broadcast_to(b_ref[0], (rows, o_ref.shape[1])).astype(F32)
            for f in range(nf):
                acc = acc + jnp.dot(h_ref[f, 0:rows, :],
                                    w_ref[0, f * tf:(f + 1) * tf, :].astype(BF16),
                                    preferred_element_type=F32)
            return acc
        o_ref[0:rows, :] = _pack_bf16_pair(proj(w2lo_ref, b2lo_ref), proj(w2hi_ref, b2hi_ref))
        if rows < tm:
            o_ref[rows:tm, :] = jnp.zeros((tm - rows, o_ref.shape[1]), I32)

    for r in range(1, nr + 1):
        @pl.when(jnp.logical_and(nsub == r, st < nf))
        def _(r=r):
            up_step(r * ts)

        @pl.when(jnp.logical_and(nsub == r, st >= nf))
        def _(r=r):
            down_step(r * ts)


def _expert_ffn(xs, w1, b1, w2, b2, tile_e, tile_nsub, n_used, *, tm, ts, n_tiles):
    ne, d, de2 = w1.shape
    de = de2 // 2
    dh = d // 2
    tf = _tile(de, 256)
    td = _tile(dh, 256)
    nf, nd = de // tf, dh // td
    nstep = nf + nd
    b1r = b1.reshape(ne, 1, de2)
    b2r = b2.reshape(ne, 1, d)

    def tile_idx(i, nu):
        return jnp.maximum(jnp.minimum(i, nu[0] - 1), 0)

    def step_idx(i, s, nu):
        return jnp.where(i < nu[0], s, nstep - 1)

    def up_map(off):
        return lambda i, s, te, ns, nu: (te[tile_idx(i, nu)], 0,
                                         off + jnp.minimum(step_idx(i, s, nu), nf - 1))

    def down_map(off):
        return lambda i, s, te, ns, nu: (te[tile_idx(i, nu)], 0,
                                         off + jnp.maximum(step_idx(i, s, nu) - nf, 0))

    grid_spec = pltpu.PrefetchScalarGridSpec(
        num_scalar_prefetch=3,
        grid=(n_tiles, nstep),
        in_specs=[pl.BlockSpec((tm, dh), lambda i, s, te, ns, nu: (tile_idx(i, nu), 0)),
                  pl.BlockSpec((1, d, tf), up_map(0)), pl.BlockSpec((1, d, tf), up_map(nf)),
                  pl.BlockSpec((1, 1, tf), up_map(0)), pl.BlockSpec((1, 1, tf), up_map(nf)),
                  pl.BlockSpec((1, de, td), down_map(0)), pl.BlockSpec((1, de, td), down_map(nd)),
                  pl.BlockSpec((1, 1, td), down_map(0)), pl.BlockSpec((1, 1, td), down_map(nd))],
        out_specs=pl.BlockSpec((tm, td), lambda i, s, te, ns, nu: (
            tile_idx(i, nu), jnp.maximum(step_idx(i, s, nu) - nf, 0))),
        scratch_shapes=[pltpu.VMEM((nf, tm, tf), BF16)],
    )
    return pl.pallas_call(
        functools.partial(_ffn_body, nf=nf, tf=tf, ts=ts, nr=tm // ts),
        grid_spec=grid_spec,
        out_shape=jax.ShapeDtypeStruct((n_tiles * tm, dh), I32),
        compiler_params=_params(("arbitrary", "arbitrary")),
        name="expert_ffn",
    )(tile_e, tile_nsub, n_used, xs, w1, w1, b1r, b1r, w2, w2, b2r, b2r)


def _combine_body(dest_ref, y_ref, gate_ref, x1_ref, gm_ref, lg_ref, lb_ref, o_ref, ybuf, sem,
                  *, alpha_res):
    i = pl.program_id(0)
    tc = x1_ref.shape[0]

    def start(j, cc):
        for kk in range(TOP_K):
            d = dest_ref[(i * tc + j) * TOP_K + kk]
            pltpu.make_async_copy(y_ref.at[pl.ds(d, 1)], ybuf.at[kk, pl.ds(j, 1)],
                                  sem.at[0]).start()
        return cc

    lax.fori_loop(0, tc, start, 0)
    for kk in range(TOP_K):
        pltpu.make_async_copy(y_ref.at[pl.ds(0, tc)], ybuf.at[kk], sem.at[0]).wait()
    g = gate_ref[...]
    y_lo = jnp.zeros((tc, ybuf.shape[2]), F32)
    y_hi = jnp.zeros((tc, ybuf.shape[2]), F32)
    for kk in range(TOP_K):
        lo, hi = _unpack_bf16_pair(ybuf[kk])
        y_lo = y_lo + g[:, kk:kk + 1] * lo
        y_hi = y_hi + g[:, kk:kk + 1] * hi
    y = jnp.concatenate([y_lo, y_hi], axis=1)
    o_ref[...] = _layer_norm(alpha_res * x1_ref[...] + gm_ref[0] * y, lg_ref[...], lb_ref[...])


def _combine(dest_flat, y_sorted, gate, x1, mod3, ln_g, ln_b, s, alpha_res):
    n, d = x1.shape
    tc = _tile(s, 128, 8)
    per_b = s // tc
    row = lambda i, dr: (i, 0)
    const = lambda i, dr: (0, 0)
    grid_spec = pltpu.PrefetchScalarGridSpec(
        num_scalar_prefetch=1, grid=(n // tc,),
        in_specs=[pl.BlockSpec(memory_space=pl.ANY),
                  pl.BlockSpec((tc, LANES), row), pl.BlockSpec((tc, d), row),
                  pl.BlockSpec((1, 1, d), lambda i, dr: (i // per_b, 0, 5)),
                  pl.BlockSpec((1, d), const), pl.BlockSpec((1, d), const)],
        out_specs=pl.BlockSpec((tc, d), row),
        scratch_shapes=[pltpu.VMEM((TOP_K, tc, d // 2), I32), pltpu.SemaphoreType.DMA((1,))])
    return pl.pallas_call(
        functools.partial(_combine_body, alpha_res=alpha_res),
        grid_spec=grid_spec,
        out_shape=jax.ShapeDtypeStruct((n, d), F32),
        compiler_params=_params(("arbitrary",)),
        name="moe_combine",
    )(dest_flat, y_sorted, gate, x1, mod3, ln_g[None], ln_b[None])


MOE_SUBTILES_PER_TILE = 5


def _moe_tiles(n_tok, n_exp):
    mean_load = n_tok * TOP_K // n_exp
    ts = 16
    while ts * 2 <= min(256, mean_load // 4):
        ts *= 2
    return ts, MOE_SUBTILES_PER_TILE * ts
def _layer(x, c_pad, w_ada, b_ada, w_in, b_forget, q_norm_g, kv_norm_g, kidx_ln_g, kidx_ln_b,
           w_uq, w_uk, w_uv, w_iq, fox_out_g, dsa_out_g, w_o, ln1_g, ln1_b,
           w_router, b_router, w1, b1, w2, b2, ln2_g, ln2_b, rel_bias, alpha_res):
    b, s, d = x.shape
    n = b * s
    fh = b_forget.shape[0]
    ql = q_norm_g.shape[0]
    kvl = kv_norm_g.shape[0]
    dh = w_uk.shape[1]
    ih = w_iq.shape[1] // IDX_DIM
    ne = w_router.shape[1]
    fw = fh * HEAD_DIM
    dw = dh * HEAD_DIM
    assert fw == dw, "head groups share the attention kernel's column-block width"
    assert fh + ih <= LANES and ne <= LANES

    mod = _matmul(c_pad, w_ada, 6 * d, F32, bias=b_ada[None], a_silu=True,
                  tm=16, tn=512, tk=d, name="adaln")
    mod3 = mod[:b].reshape(b, 1, 6 * d)
    u = _modulate(x, mod3, 0, 1).reshape(n, d)

    w_in_t = jnp.swapaxes(w_in, 0, 1)
    q_scale = HEAD_DIM ** -0.5 * LOG2E
    qkv = _matmul(u, w_in_t, 3 * fw, BF16, w_t=True, scale_cols=(fw, q_scale), tm=2048,
                  name="proj_qkv")
    tail_w = -(-(ql + kvl + IDX_DIM + fh + ih) // LANES) * LANES
    tail = _matmul(u, w_in_t, tail_w, F32, w_t=True, col0=3 * fw, name="proj_tail")
    cqn, ckvn, kin, small, cum_t = _tail_prep(tail, b, s, ql, kvl, fh, ih, q_norm_g, kv_norm_g,
                                              kidx_ln_g, kidx_ln_b, b_forget)

    t = 256 if s % 256 == 0 else 128
    fox = _attention("fox", b, s, fh, t, (qkv, 0), (qkv, 1), (qkv, 2), fox_out_g, cum_t=cum_t)

    q_d = _matmul(cqn, w_uq, dw, BF16, scale_cols=(dw, q_scale), tm=2048, name="proj_qd")
    q_i = _matmul(cqn, w_iq, ih * IDX_DIM, BF16, tm=2048, name="proj_qi")
    k_d = _matmul(ckvn, w_uk.reshape(kvl, dw), dw, BF16, tm=2048, name="expand_k")
    v_d = _matmul(ckvn, w_uv.reshape(kvl, dw), dw, BF16, tm=2048, name="expand_v")
    k_sel = min(TOPK_MAX, s // 4)
    mask = _indexer(q_i, kin, small, b, s, ih, fh, t, k_sel)
    bias_t = _bias_tiles(rel_bias, dh, t)
    dsa = _attention("dsa", b, s, dh, t, (q_d, 0), (k_d, 0), (v_d, 0), dsa_out_g,
                     mask=mask, bias_tiles=bias_t)

    mix = _matmul(fox, w_o, d, F32, a2=dsa, name="proj_out")

    wr_pad = jnp.zeros((d, LANES), F32).at[:, :ne].set(w_router)
    br_pad = jnp.zeros((1, LANES), F32).at[0, :ne].set(b_router)
    x1, u_packed, eidx, rank, gate, cnt = _router(x.reshape(n, d), mix, mod3, ln1_g, ln1_b,
                                                  wr_pad, br_pad, s, ne, alpha_res)
    ts, tm = _moe_tiles(n, ne)
    n_tiles = -(-(n * TOP_K) // tm) + ne
    counts = cnt[0, :ne]
    ntile = (counts + tm - 1) // tm
    tend = jnp.cumsum(ntile)
    tstart = tend - ntile
    n_used = tend[-1]
    dest = (tstart * tm)[eidx[:, :TOP_K]] + rank[:, :TOP_K]
    dest_flat = dest.reshape(n * TOP_K).astype(I32)
    tid = jnp.minimum(jnp.arange(n_tiles, dtype=I32), n_used - 1)
    tile_e = jnp.minimum(jnp.searchsorted(tend, tid, side="right"), ne - 1).astype(I32)
    valid = jnp.clip(counts[tile_e] - (tid - tstart[tile_e]) * tm, 0, tm)
    valid = jnp.where(jnp.arange(n_tiles) < n_used, valid, 0)
    tile_nsub = ((valid + ts - 1) // ts).astype(I32)
    xs = _dispatch(dest_flat, u_packed, n_tiles * tm)
    y_sorted = _expert_ffn(xs, w1, b1, w2, b2, tile_e, tile_nsub,
                           n_used.reshape(1).astype(I32), tm=tm, ts=ts, n_tiles=n_tiles)
    out = _combine(dest_flat, y_sorted, gate, x1, mod3, ln2_g, ln2_b, s, alpha_res)
    return out.reshape(b, s, d)


def kernel(x, c, w_ada, b_ada, w_in, b_forget, q_norm_g, kv_norm_g, kidx_ln_g, kidx_ln_b, w_uq, w_uk, w_uv, w_iq, fox_out_g, dsa_out_g, w_o, ln1_g, ln1_b, w_router, b_router, w1, b1, w2, b2, ln2_g, ln2_b, rel_bias):
    depth = w_ada.shape[0]
    alpha_res = (2 * depth) ** 0.25
    b, d = c.shape
    c_pad = jnp.zeros((16, d), F32).at[:b].set(c)
    for l in range(depth):
        x = _layer(x, c_pad, w_ada[l], b_ada[l], w_in[l], b_forget[l], q_norm_g[l], kv_norm_g[l],
                   kidx_ln_g[l], kidx_ln_b[l], w_uq[l], w_uk[l], w_uv[l], w_iq[l], fox_out_g[l],
                   dsa_out_g[l], w_o[l], ln1_g[l], ln1_b[l], w_router[l], b_router[l],
                   w1[l], b1[l], w2[l], b2[l], ln2_g[l], ln2_b[l], rel_bias, alpha_res)
    return x
```

```python
import functools
import math

import numpy as np
import jax
import jax.numpy as jnp
from jax import lax
from jax.experimental import pallas as pl
from jax.experimental.pallas import tpu as pltpu

F32 = jnp.float32
BF16 = jnp.bfloat16
I32 = jnp.int32

LANES = 128
HEAD_DIM = 128
IDX_DIM = 128
TOPK_MAX = 256
N_BUCKETS = 32
MAX_DISTANCE = 128
TOP_K = 4
SWIGLU_LIMIT = 7.0
SWIGLU_ALPHA = 1.702
NEG_BIG = -1e30
LOG2E = math.log2(math.e)
VMEM_LIMIT = 56 * 1024 * 1024
INT_MIN = -2 ** 31


def _tile(n, pref, unit=LANES):
    if n <= pref:
        return n
    t = (pref // unit) * unit
    while t > unit and n % t:
        t -= unit
    assert n % t == 0, (n, pref)
    return t


def _pack_bf16_pair(lo, hi):
    lo_bits = lax.bitcast_convert_type(lo.astype(BF16).astype(F32), I32)
    hi_bits = lax.bitcast_convert_type(hi.astype(BF16).astype(F32), I32)
    return lax.shift_right_logical(lo_bits, 16) | (hi_bits & jnp.int32(-65536))


def _unpack_bf16_pair(words):
    lo = lax.bitcast_convert_type(lax.shift_left(words, 16), F32)
    hi = lax.bitcast_convert_type(words & jnp.int32(-65536), F32)
    return lo, hi


def _params(sem, vmem=VMEM_LIMIT):
    return pltpu.CompilerParams(dimension_semantics=sem, vmem_limit_bytes=vmem)


def _mm_body(*refs, nk, nk1, a_silu, has_bias, col_block0, w_cols_if_overhang, w_t, out_scale):
    refs = list(refs)
    a_ref = refs.pop(0)
    a2_ref = refs.pop(0) if nk1 < nk else None
    w_ref = refs.pop(0)
    b_ref = refs.pop(0) if has_bias else None
    o_ref, acc_ref = refs
    k = pl.program_id(2)

    @pl.when(k == 0)
    def _():
        acc_ref[...] = jnp.zeros_like(acc_ref)

    def accumulate(src_ref):
        a = src_ref[...]
        if a_silu:
            a = a.astype(F32)
            a = a * jax.nn.sigmoid(a)
        w = w_ref[...]
        n_axis = 0 if w_t else 1
        if w_cols_if_overhang is not None:
            tn = w.shape[n_axis]
            col = lax.broadcasted_iota(I32, w.shape, n_axis) + (pl.program_id(1) + col_block0) * tn
            w = jnp.where(col < w_cols_if_overhang, w, 0.0)
        acc_ref[...] += lax.dot_general(a.astype(BF16), w.astype(BF16),
                                        (((1,), (1 if w_t else 0,)), ((), ())),
                                        preferred_element_type=F32)

    if a2_ref is None:
        accumulate(a_ref)
    else:
        pl.when(k < nk1)(lambda: accumulate(a_ref))
        pl.when(k >= nk1)(lambda: accumulate(a2_ref))

    @pl.when(k == nk - 1)
    def _():
        r = acc_ref[...]
        if has_bias:
            r = r + b_ref[...]
        if out_scale is not None:
            n_blocks, factor = out_scale
            r = r * jnp.where(pl.program_id(1) < n_blocks, factor, 1.0)
        o_ref[...] = r.astype(o_ref.dtype)


def _matmul(a, w, n_out, out_dtype, *, a2=None, bias=None, a_silu=False, col0=0, w_t=False,
            a_rows=None, scale_cols=None, tm=1024, tn=1024, tk=1024, name="matmul"):
    m, kdim = a.shape
    row0 = 0
    if a_rows is not None:
        row0, m = a_rows
    w_cols = w.shape[0] if w_t else w.shape[1]
    tm = _tile(math.gcd(m, row0), tm, 16)
    row_block0 = row0 // tm
    tn = _tile(math.gcd(n_out, col0, scale_cols[0] if scale_cols else 0), tn)
    tk = _tile(kdim, tk)
    nk1 = kdim // tk
    nk = nk1 if a2 is None else nk1 + a2.shape[1] // tk
    col_block0 = col0 // tn
    overhang = col0 + n_out > w_cols
    out_scale = None
    if scale_cols is not None:
        assert scale_cols[0] % tn == 0
        out_scale = (scale_cols[0] // tn, scale_cols[1])
    in_specs = [pl.BlockSpec((tm, tk), lambda i, j, k: (i + row_block0, jnp.minimum(k, nk1 - 1)))]
    args = [a]
    if a2 is not None:
        assert a2.shape[1] % tk == 0
        in_specs.append(pl.BlockSpec((tm, tk), lambda i, j, k: (i, jnp.maximum(k - nk1, 0))))
        args.append(a2)
    if w_t:
        in_specs.append(pl.BlockSpec((tn, tk), lambda i, j, k: (j + col_block0, k)))
    else:
        in_specs.append(pl.BlockSpec((tk, tn), lambda i, j, k: (k, j + col_block0)))
    args.append(w)
    if bias is not None:
        in_specs.append(pl.BlockSpec((1, tn), lambda i, j, k: (0, j)))
        args.append(bias)
    return pl.pallas_call(
        functools.partial(_mm_body, nk=nk, nk1=nk1, a_silu=a_silu, has_bias=bias is not None,
                          col_block0=col_block0, w_t=w_t, out_scale=out_scale,
                          w_cols_if_overhang=w_cols if overhang else None),
        grid=(m // tm, n_out // tn, nk),
        in_specs=in_specs,
        out_specs=pl.BlockSpec((tm, tn), lambda i, j, k: (i, j)),
        out_shape=jax.ShapeDtypeStruct((m, n_out), out_dtype),
        scratch_shapes=[pltpu.VMEM((tm, tn), F32)],
        compiler_params=_params(("parallel", "parallel", "arbitrary")),
        name=name,
    )(*args)


def _modulate_body(x_ref, sh_ref, sc_ref, o_ref):
    o_ref[0] = (x_ref[0] * (1.0 + sc_ref[0]) + sh_ref[0]).astype(o_ref.dtype)


def _modulate(x, mod3, shift_chunk, scale_chunk):
    b, s, d = x.shape
    ts = _tile(s, 512, 16)
    return pl.pallas_call(
        _modulate_body,
        grid=(b, s // ts),
        in_specs=[pl.BlockSpec((1, ts, d), lambda i, j: (i, j, 0)),
                  pl.BlockSpec((1, 1, d), lambda i, j: (i, 0, shift_chunk)),
                  pl.BlockSpec((1, 1, d), lambda i, j: (i, 0, scale_chunk))],
        out_specs=pl.BlockSpec((1, ts, d), lambda i, j: (i, j, 0)),
        out_shape=jax.ShapeDtypeStruct((b, s, d), BF16),
        compiler_params=_params(("parallel", "parallel")),
        name="modulate",
    )(x, mod3, mod3)


def _tail_body(t_ref, qg_ref, kvg_ref, lng_ref, lnb_ref, bf_ref,
               cq_ref, ckv_ref, kin_ref, small_ref, cum_ref, carry_ref, *, fh, ql, kvl, w_scale):
    j = pl.program_id(1)
    ts = t_ref.shape[0]

    @pl.when(j == 0)
    def _():
        carry_ref[...] = jnp.zeros_like(carry_ref)

    cq = t_ref[:, fh:fh + ql]
    cq_ref[...] = (cq * lax.rsqrt(jnp.mean(cq * cq, axis=-1, keepdims=True) + 1e-6)
                   * qg_ref[...]).astype(cq_ref.dtype)
    ckv = t_ref[:, fh + ql:fh + ql + kvl]
    ckv_ref[...] = (ckv * lax.rsqrt(jnp.mean(ckv * ckv, axis=-1, keepdims=True) + 1e-6)
                    * kvg_ref[...]).astype(ckv_ref.dtype)
    ki = t_ref[:, fh + ql + kvl:fh + ql + kvl + IDX_DIM]
    mu = jnp.mean(ki, axis=-1, keepdims=True)
    kc = ki - mu
    var = jnp.mean(kc * kc, axis=-1, keepdims=True)
    kin_ref[...] = (kc * lax.rsqrt(var + 1e-5) * lng_ref[...] + lnb_ref[...]).astype(kin_ref.dtype)

    small_ref[...] = t_ref[:, ql + kvl + IDX_DIM:ql + kvl + IDX_DIM + LANES] * w_scale
    z = t_ref[:, 0:LANES] + bf_ref[...]
    log_f = jnp.minimum(z, 0.0) - jnp.log(1.0 + jnp.exp(-jnp.abs(z)))
    row = lax.broadcasted_iota(I32, (ts, ts), 0)
    col = lax.broadcasted_iota(I32, (ts, ts), 1)
    tri = jnp.where(col <= row, 1.0, 0.0).astype(F32)
    incl = jnp.dot(tri, log_f, preferred_element_type=F32,
                   precision=lax.Precision.HIGHEST) + carry_ref[0:1, :]
    carry_ref[0:1, :] = incl[ts - 1:ts, :]
    cum_ref[...] = incl * (-LOG2E)


def _tail_prep(tail, b, s, ql, kvl, fh, ih, q_norm_g, kv_norm_g, ln_g, ln_b, b_forget):
    n = tail.shape[0]
    assert ql % LANES == 0 and kvl % LANES == 0
    ts = _tile(s, 256)
    nj = s // ts
    bf_pad = jnp.zeros((1, LANES), F32).at[0, :fh].set(b_forget)
    w_scale = (ih ** -0.5) * (IDX_DIM ** -0.5)
    row = lambda i, j: (i * nj + j, 0)
    const = lambda i, j: (0, 0)
    return pl.pallas_call(
        functools.partial(_tail_body, fh=fh, ql=ql, kvl=kvl, w_scale=w_scale),
        grid=(b, nj),
        in_specs=[pl.BlockSpec((ts, tail.shape[1]), row),
                  pl.BlockSpec((1, ql), const), pl.BlockSpec((1, kvl), const),
                  pl.BlockSpec((1, IDX_DIM), const), pl.BlockSpec((1, IDX_DIM), const),
                  pl.BlockSpec((1, LANES), const)],
        out_specs=[pl.BlockSpec((ts, ql), row), pl.BlockSpec((ts, kvl), row),
                   pl.BlockSpec((ts, IDX_DIM), row), pl.BlockSpec((ts, LANES), row),
                   pl.BlockSpec((ts, LANES), row)],
        out_shape=[jax.ShapeDtypeStruct((n, ql), BF16), jax.ShapeDtypeStruct((n, kvl), BF16),
                   jax.ShapeDtypeStruct((n, IDX_DIM), BF16), jax.ShapeDtypeStruct((n, LANES), F32),
                   jax.ShapeDtypeStruct((n, LANES), F32)],
        scratch_shapes=[pltpu.VMEM((8, LANES), F32)],
        compiler_params=_params(("parallel", "arbitrary")),
        name="tail_prep",
    )(tail, q_norm_g[None], kv_norm_g[None], ln_g[None], ln_b[None], bf_pad)


_MAX_EXACT = N_BUCKETS // 2
_BUCKET_THRESHOLDS = tuple(
    int(math.ceil(_MAX_EXACT * (MAX_DISTANCE / _MAX_EXACT) ** (k / (N_BUCKETS - _MAX_EXACT)) - 1e-9))
    for k in range(1, N_BUCKETS - _MAX_EXACT))


def _bias_body(rb_ref, o_ref, *, t):
    which = pl.program_id(0)
    h = pl.program_id(1)
    row = lax.broadcasted_iota(I32, (t, t), 0)
    col = lax.broadcasted_iota(I32, (t, t), 1)
    d = jnp.maximum(which * t + col - row, 0)
    large = jnp.full((t, t), _MAX_EXACT, I32)
    for thr in _BUCKET_THRESHOLDS:
        large = large + jnp.where(d >= thr, 1, 0)
    bucket = jnp.where(d < _MAX_EXACT, d, large)
    bias = jnp.zeros((t, t), F32)
    for bkt in range(N_BUCKETS):
        bias = jnp.where(bucket == bkt, rb_ref[bkt, h], bias)
    o_ref[0, 0] = (bias - rb_ref[N_BUCKETS - 1, h]) * LOG2E


def _bias_tiles(rel_bias, nh, t):
    assert 2 * t - (t - 1) >= _BUCKET_THRESHOLDS[-1], "key tiles two or more away are all last-bucket"
    return pl.pallas_call(
        functools.partial(_bias_body, t=t),
        grid=(2, nh),
        in_specs=[pl.BlockSpec(memory_space=pltpu.SMEM)],
        out_specs=pl.BlockSpec((1, 1, t, t), lambda w, h: (w, h, 0, 0)),
        out_shape=jax.ShapeDtypeStruct((2, nh, t, t), F32),
        compiler_params=_params(("parallel", "parallel")),
        name="bias_tiles",
    )(rel_bias)


ATTN_QCOLS = 128


def _attn_body(qi_ref, kj_ref, *refs, nh, t, mode):
    if mode == "fox":
        q_ref, k_ref, vt_ref, g_ref, cum_ref, o_ref, m_ref, l_ref, acc_ref, neg_ref = refs
    else:
        q_ref, k_ref, vt_ref, g_ref, mask_ref, bt_ref, o_ref, m_ref, l_ref, acc_ref, neg_ref = refs
    p = pl.program_id(1)
    qi = qi_ref[p]
    kj = kj_ref[p]
    nblk = t // ATTN_QCOLS

    @pl.when(kj == 0)
    def _():
        m_ref[...] = jnp.full_like(m_ref, NEG_BIG)
        l_ref[...] = jnp.zeros_like(l_ref)
        acc_ref[...] = jnp.zeros_like(acc_ref)

    def heads(addend):
        for h in range(nh):
            sl = slice(h * HEAD_DIM, (h + 1) * HEAD_DIM)
            for r in range(nblk):
                qs = slice(r * ATTN_QCOLS, (r + 1) * ATTN_QCOLS)
                idx = h * nblk + r
                s = lax.dot_general(k_ref[:, sl], q_ref[qs, sl], (((1,), (1,)), ((), ())),
                                    preferred_element_type=F32)
                s = s + addend(h, qs)
                m_prev = m_ref[idx]
                m_next = jnp.maximum(m_prev, jnp.max(s, axis=0, keepdims=True))
                alpha = jnp.exp2(m_prev - m_next)
                pr = jnp.exp2(s - m_next)
                l_ref[idx] = alpha * l_ref[idx] + jnp.sum(pr, axis=0, keepdims=True)
                acc_ref[idx] = acc_ref[idx] * alpha + jnp.dot(
                    vt_ref[sl, :], pr.astype(BF16), preferred_element_type=F32)
                m_ref[idx] = m_next

    key_i = lax.broadcasted_iota(I32, (t, t), 0)
    qry_i = lax.broadcasted_iota(I32, (t, t), 1)
    if mode == "fox":
        @pl.when(kj == qi)
        def _():
            neg_ref[...] = jnp.where(key_i <= qry_i, 0.0, NEG_BIG).astype(F32)
            heads(lambda h, qs: neg_ref[:, qs] + cum_ref[:, h:h + 1])

        @pl.when(kj != qi)
        def _():
            heads(lambda h, qs: cum_ref[:, h:h + 1])
    else:
        neg_ref[...] = ((mask_ref[0, 0].astype(F32) - 1.0) * (-NEG_BIG)).T

        @pl.when(qi - kj >= 2)
        def _():
            heads(lambda h, qs: neg_ref[:, qs])

        @pl.when(qi - kj < 2)
        def _():
            near = qi - kj
            heads(lambda h, qs: neg_ref[:, qs] + bt_ref[near, h, :, qs])

    @pl.when(kj == qi)
    def _():
        for r in range(nblk):
            qs = slice(r * ATTN_QCOLS, (r + 1) * ATTN_QCOLS)
            ss = jnp.zeros((1, ATTN_QCOLS), F32)
            for h in range(nh):
                o = acc_ref[h * nblk + r] / l_ref[h * nblk + r]
                acc_ref[h * nblk + r] = o
                ss = ss + jnp.sum(o * o, axis=0, keepdims=True)
            rinv = lax.rsqrt(ss / (nh * HEAD_DIM) + 1e-6)
            for h in range(nh):
                sl = slice(h * HEAD_DIM, (h + 1) * HEAD_DIM)
                o_ref[qs, sl] = ((acc_ref[h * nblk + r] * rinv).T * g_ref[:, sl]).astype(o_ref.dtype)


def _attention(mode, b, s, nh, t, q_src, k_src, vt, gain, *, cum=None, mask=None, bias_tiles=None):
    nq = s // t
    pairs = [(i, j) for i in range(nq) for j in range(i + 1)]
    qi_arr = jnp.asarray(np.array([pq for pq, _ in pairs], np.int32))
    kj_arr = jnp.asarray(np.array([pk for _, pk in pairs], np.int32))
    hw = nh * HEAD_DIM
    n = b * s
    nblk = t // ATTN_QCOLS

    def qmap(cb):
        return lambda bi, p, qi, kj: (bi * nq + qi[p], cb)

    def kmap(cb):
        return lambda bi, p, qi, kj: (bi * nq + kj[p], cb)

    in_specs = [pl.BlockSpec((t, hw), qmap(q_src[1])),
                pl.BlockSpec((t, hw), kmap(k_src[1])),
                pl.BlockSpec((hw, t), lambda bi, p, qi, kj: (0, bi * nq + kj[p])),
                pl.BlockSpec((1, hw), lambda bi, p, qi, kj: (0, 0))]
    args = [q_src[0], k_src[0], vt, gain[None]]
    if mode == "fox":
        in_specs.append(pl.BlockSpec((t, LANES), kmap(0)))
        args.append(cum)
    else:
        in_specs += [pl.BlockSpec((1, 1, t, t), lambda bi, p, qi, kj: (bi, kj[p], qi[p], 0)),
                     pl.BlockSpec(memory_space=pltpu.VMEM)]
        args += [mask, bias_tiles]
    grid_spec = pltpu.PrefetchScalarGridSpec(
        num_scalar_prefetch=2,
        grid=(b, len(pairs)),
        in_specs=in_specs,
        out_specs=pl.BlockSpec((t, hw), lambda bi, p, qi, kj: (bi * nq + qi[p], 0)),
        scratch_shapes=[pltpu.VMEM((nh * nblk, 1, ATTN_QCOLS), F32),
                        pltpu.VMEM((nh * nblk, 1, ATTN_QCOLS), F32),
                        pltpu.VMEM((nh * nblk, HEAD_DIM, ATTN_QCOLS), F32),
                        pltpu.VMEM((t, t), F32)],
    )
    return pl.pallas_call(
        functools.partial(_attn_body, nh=nh, t=t, mode=mode),
        grid_spec=grid_spec,
        out_shape=jax.ShapeDtypeStruct((n, hw), BF16),
        compiler_params=_params(("parallel", "arbitrary")),
        name=mode + "_attention",
    )(qi_arr, kj_arr, *args)


IDX_ROWS = 128


def _float_key(x):
    bits = lax.bitcast_convert_type(x, I32)
    return bits ^ (lax.shift_right_arithmetic(bits, 31) & 0x7FFFFFFF)


def _indexer_body(q_ref, k_ref, w_ref, o_ref, key_ref, *, ih, w_lane0, tq, ck, nchunk, k_sel):
    i = pl.program_id(1)
    n_valid = ((i + 1) * tq + ck - 1) // ck
    row = lax.broadcasted_iota(I32, (tq, ck), 0) + i * tq
    lane_col = lax.broadcasted_iota(I32, (tq, ck), 1)

    def score_chunk(c, carry):
        kc = k_ref[0, pl.ds(pl.multiple_of(c * ck, ck), ck), :]
        for r in range(tq // IDX_ROWS):
            rows = slice(r * IDX_ROWS, (r + 1) * IDX_ROWS)
            w_blk = w_ref[rows, :]
            acc = jnp.zeros((IDX_ROWS, ck), F32)
            for h in range(ih):
                sh = lax.dot_general(q_ref[rows, h * IDX_DIM:(h + 1) * IDX_DIM], kc,
                                     (((1,), (1,)), ((), ())), preferred_element_type=F32)
                acc = acc + w_blk[:, w_lane0 + h:w_lane0 + h + 1] * jnp.maximum(sh, 0.0)
            causal = (lax.broadcasted_iota(I32, (IDX_ROWS, ck), 1) + c * ck
                      <= lax.broadcasted_iota(I32, (IDX_ROWS, ck), 0) + (i * tq + r * IDX_ROWS))
            acc = jnp.where(causal, acc, -jnp.inf)
            key_ref[c, rows, :] = _float_key(acc)
        return carry

    lax.fori_loop(0, n_valid, score_chunk, 0)

    def count(pred_fn, level):
        parts = []
        for r in range(tq // IDX_ROWS):
            rows = slice(r * IDX_ROWS, (r + 1) * IDX_ROWS)

            level_b = jnp.broadcast_to(level[rows], (IDX_ROWS, LANES))

            def body(c, acc, rows=rows, level_b=level_b):
                for part in range(ck // LANES):
                    kv = key_ref[c, rows, part * LANES:(part + 1) * LANES]
                    acc = acc + jnp.where(pred_fn(kv, level_b), 1, 0)
                return acc
            acc = lax.fori_loop(0, n_valid, body, jnp.zeros((IDX_ROWS, LANES), I32))
            parts.append(jnp.sum(acc, axis=1, keepdims=True))
        return jnp.concatenate(parts, axis=0)

    ge = lambda kv, lv: kv >= lv
    t0 = jnp.full((tq, 1), INT_MIN, I32)
    zero = jnp.zeros((tq, 1), I32)
    thr = jnp.where(count(ge, zero) >= k_sel, zero, t0)

    def bit_step(bi, thr):
        cand = thr | lax.shift_left(jnp.int32(1), 30 - bi)
        return jnp.where(count(ge, cand) >= k_sel, cand, thr)

    thr = lax.fori_loop(0, 31, bit_step, thr)
    need = (k_sel - count(lambda kv, lv: kv > lv, thr)).astype(F32)

    urow = lax.broadcasted_iota(I32, (ck, ck), 0)
    ucol = lax.broadcasted_iota(I32, (ck, ck), 1)
    upper = jnp.where(urow <= ucol, 1.0, 0.0).astype(BF16)
    ones = jnp.ones((ck, LANES), BF16)

    def select_chunk(c, offset):
        kv = key_ref[c]
        eq = kv == thr
        eqb = jnp.where(eq, 1.0, 0.0).astype(BF16)
        rank = jnp.dot(eqb, upper, preferred_element_type=F32) + offset[:, 0:1]
        take = (kv > thr) | (eq & (rank <= need))
        take = take & (lane_col + c * ck <= row)
        o_ref[0, c] = jnp.where(take, 1, 0).astype(jnp.int8)
        return offset + jnp.dot(eqb, ones, preferred_element_type=F32)

    lax.fori_loop(0, n_valid, select_chunk, jnp.zeros((tq, LANES), F32))

    def zero_chunk(c, carry):
        o_ref[0, c] = jnp.zeros((tq, ck), jnp.int8)
        return carry

    lax.fori_loop(n_valid, nchunk, zero_chunk, 0)


def _indexer(q_idx, k_in, small, b, s, ih, w_lane0, ck, k_sel):
    tq = _tile(s, 256, IDX_ROWS)
    nq = s // tq
    nchunk = s // ck
    k3 = k_in.reshape(b, s, IDX_DIM)
    return pl.pallas_call(
        functools.partial(_indexer_body, ih=ih, w_lane0=w_lane0, tq=tq, ck=ck, nchunk=nchunk,
                          k_sel=k_sel),
        grid=(b, nq),
        in_specs=[pl.BlockSpec((tq, ih * IDX_DIM), lambda bi, i: (bi * nq + i, 0)),
                  pl.BlockSpec((1, s, IDX_DIM), lambda bi, i: (bi, 0, 0)),
                  pl.BlockSpec((tq, LANES), lambda bi, i: (bi * nq + i, 0))],
        out_specs=pl.BlockSpec((1, nchunk, tq, ck), lambda bi, i: (bi, 0, i, 0)),
        out_shape=jax.ShapeDtypeStruct((b, nchunk, s, ck), jnp.int8),
        scratch_shapes=[pltpu.VMEM((nchunk, tq, ck), I32)],
        compiler_params=_params(("parallel", "parallel")),
        name="indexer_topk",
    )(q_idx, k3, small)


def _layer_norm(h, g, bvec):
    mu = jnp.mean(h, axis=-1, keepdims=True)
    hc = h - mu
    var = jnp.mean(hc * hc, axis=-1, keepdims=True)
    return hc * lax.rsqrt(var + 1e-5) * g + bvec


def _router_body(x_ref, mix_ref, ga_ref, shm_ref, scm_ref, lg_ref, lb_ref, wr_ref, br_ref,
                 x1_ref, up_ref, eidx_ref, rank_ref, gate_ref, cnt_ref, carry_ref,
                 *, alpha_res, ne, nsteps):
    i = pl.program_id(0)
    tm, d = x_ref.shape

    @pl.when(i == 0)
    def _():
        carry_ref[...] = jnp.zeros_like(carry_ref)

    x1 = _layer_norm(alpha_res * x_ref[...] + ga_ref[0] * mix_ref[...], lg_ref[...], lb_ref[...])
    x1_ref[...] = x1
    u = x1 * (1.0 + scm_ref[0]) + shm_ref[0]
    up_ref[...] = _pack_bf16_pair(u[:, :d // 2], u[:, d // 2:])

    logits = jnp.dot(u, wr_ref[...], preferred_element_type=F32,
                     precision=lax.Precision.HIGHEST) + br_ref[...]
    lane = lax.broadcasted_iota(I32, (tm, LANES), 1)
    lane_f = lane.astype(F32)
    work = jnp.where(lane < ne, logits, -jnp.inf)
    sel = jnp.zeros((tm, LANES), F32)
    idxs, vals, hits = [], [], []
    for _ in range(TOP_K):
        mx = jnp.max(work, axis=1, keepdims=True)
        ik = jnp.min(jnp.where(work == mx, lane_f, float(LANES)), axis=1, keepdims=True)
        hit = lane_f == ik
        sel = jnp.where(hit, 1.0, sel)
        work = jnp.where(hit, -jnp.inf, work)
        idxs.append(ik.astype(I32))
        vals.append(mx)
        hits.append(hit)
    exps = [jnp.exp(v - vals[0]) for v in vals]
    denom = exps[0] + exps[1] + exps[2] + exps[3]

    row = lax.broadcasted_iota(I32, (tm, tm), 0)
    col = lax.broadcasted_iota(I32, (tm, tm), 1)
    tri = jnp.where(col < row, 1.0, 0.0).astype(BF16)
    before = jnp.dot(tri, sel.astype(BF16), preferred_element_type=F32) + carry_ref[0:1, :]
    total = carry_ref[0:1, :] + jnp.sum(sel, axis=0, keepdims=True)
    carry_ref[0:1, :] = total

    eidx = jnp.zeros((tm, LANES), I32)
    rank = jnp.zeros((tm, LANES), I32)
    gate = jnp.zeros((tm, LANES), F32)
    for kk in range(TOP_K):
        rk = jnp.sum(jnp.where(hits[kk], before, 0.0), axis=1, keepdims=True)
        eidx = jnp.where(lane == kk, idxs[kk], eidx)
        rank = jnp.where(lane == kk, rk.astype(I32), rank)
        gate = jnp.where(lane == kk, exps[kk] / denom, gate)
    eidx_ref[...] = eidx
    rank_ref[...] = rank
    gate_ref[...] = gate
    cnt_ref[...] = jnp.broadcast_to(total, cnt_ref.shape).astype(I32)


def _router(x2d, mix, mod3, ln_g, ln_b, w_router_pad, b_router_pad, s, ne, alpha_res):
    n, d = x2d.shape
    tm = _tile(s, 256, 16)
    nsteps = n // tm
    per_b = s // tm
    row = lambda i: (i, 0)
    const = lambda i: (0, 0)
    modspec = lambda chunk: pl.BlockSpec((1, 1, d), lambda i: (i // per_b, 0, chunk))
    return pl.pallas_call(
        functools.partial(_router_body, alpha_res=alpha_res, ne=ne, nsteps=nsteps),
        grid=(nsteps,),
        in_specs=[pl.BlockSpec((tm, d), row), pl.BlockSpec((tm, d), row),
                  modspec(2), modspec(3), modspec(4),
                  pl.BlockSpec((1, d), const), pl.BlockSpec((1, d), const),
                  pl.BlockSpec((d, LANES), const), pl.BlockSpec((1, LANES), const)],
        out_specs=[pl.BlockSpec((tm, d), row), pl.BlockSpec((tm, d // 2), row),
                   pl.BlockSpec((tm, LANES), row), pl.BlockSpec((tm, LANES), row),
                   pl.BlockSpec((tm, LANES), row), pl.BlockSpec((8, LANES), const)],
        out_shape=[jax.ShapeDtypeStruct((n, d), F32), jax.ShapeDtypeStruct((n, d // 2), I32),
                   jax.ShapeDtypeStruct((n, LANES), I32), jax.ShapeDtypeStruct((n, LANES), I32),
                   jax.ShapeDtypeStruct((n, LANES), F32), jax.ShapeDtypeStruct((8, LANES), I32)],
        scratch_shapes=[pltpu.VMEM((8, LANES), F32)],
        compiler_params=_params(("arbitrary",)),
        name="ln1_router",
    )(x2d, mix, mod3, mod3, mod3, ln_g[None], ln_b[None], w_router_pad, b_router_pad)


def _dispatch_body(dest_ref, u_ref, xs_ref, sem):
    i = pl.program_id(0)
    tc = u_ref.shape[0]

    def start(j, cc):
        for kk in range(TOP_K):
            d = dest_ref[(i * tc + j) * TOP_K + kk]
            pltpu.make_async_copy(u_ref.at[pl.ds(j, 1)], xs_ref.at[pl.ds(d, 1)], sem.at[0]).start()
        return cc

    lax.fori_loop(0, tc, start, 0)
    for _ in range(TOP_K):
        pltpu.make_async_copy(u_ref, xs_ref.at[pl.ds(0, tc)], sem.at[0]).wait()


def _dispatch(dest_flat, u_packed, n_slots):
    n, dw = u_packed.shape
    tc = _tile(n, 256, 8)
    grid_spec = pltpu.PrefetchScalarGridSpec(
        num_scalar_prefetch=1, grid=(n // tc,),
        in_specs=[pl.BlockSpec((tc, dw), lambda i, dr: (i, 0))],
        out_specs=pl.BlockSpec(memory_space=pl.ANY),
        scratch_shapes=[pltpu.SemaphoreType.DMA((1,))])
    return pl.pallas_call(
        _dispatch_body,
        grid_spec=grid_spec,
        out_shape=jax.ShapeDtypeStruct((n_slots, dw), I32),
        compiler_params=pltpu.CompilerParams(dimension_semantics=("arbitrary",),
                                             has_side_effects=True),
        name="moe_dispatch",
    )(dest_flat, u_packed)


def _ffn_body(te_ref, nsub_ref, nused_ref, x_ref, w1g_ref, w1u_ref, b1g_ref, b1u_ref,
              w2lo_ref, w2hi_ref, b2lo_ref, b2hi_ref, o_ref, h_ref, *, nf, tf, ts, nr):
    i = pl.program_id(0)
    st = pl.program_id(1)
    tm, dh = x_ref.shape
    nsub = nsub_ref[i]

    def up_step(rows):
        lo, hi = _unpack_bf16_pair(x_ref[0:rows, :])
        lo, hi = lo.astype(BF16), hi.astype(BF16)

        def proj(w_ref, b_ref):
            return (jnp.dot(lo, w_ref[0, 0:dh, :].astype(BF16), preferred_element_type=F32)
                    + jnp.dot(hi, w_ref[0, dh:2 * dh, :].astype(BF16), preferred_element_type=F32)
                    + b_ref[0])
        gate = jnp.minimum(proj(w1g_ref, b1g_ref), SWIGLU_LIMIT)
        up = jnp.clip(proj(w1u_ref, b1u_ref), -SWIGLU_LIMIT, SWIGLU_LIMIT)
        act = (up + 1.0) * gate * jax.nn.sigmoid(SWIGLU_ALPHA * gate)
        h_ref[st, 0:rows, :] = act.astype(BF16)

    def down_step(rows):
        def proj(w_ref, b_ref):
            acc = jnp.broadcast_to(b_ref[0], (rows, o_ref.shape[1])).astype(F32)
            for f in range(nf):
                acc = acc + jnp.dot(h_ref[f, 0:rows, :],
                                    w_ref[0, f * tf:(f + 1) * tf, :].astype(BF16),
                                    preferred_element_type=F32)
            return acc
        o_ref[0:rows, :] = _pack_bf16_pair(proj(w2lo_ref, b2lo_ref), proj(w2hi_ref, b2hi_ref))
        if rows < tm:
            o_ref[rows:tm, :] = jnp.zeros((tm - rows, o_ref.shape[1]), I32)

    for r in range(1, nr + 1):
        @pl.when(jnp.logical_and(nsub == r, st < nf))
        def _(r=r):
            up_step(r * ts)

        @pl.when(jnp.logical_and(nsub == r, st >= nf))
        def _(r=r):
            down_step(r * ts)


def _expert_ffn(xs, w1, b1, w2, b2, tile_e, tile_nsub, n_used, *, tm, ts, n_tiles):
    ne, d, de2 = w1.shape
    de = de2 // 2
    dh = d // 2
    tf = _tile(de, 256)
    td = _tile(dh, 256)
    nf, nd = de // tf, dh // td
    nstep = nf + nd
    b1r = b1.reshape(ne, 1, de2)
    b2r = b2.reshape(ne, 1, d)

    def tile_idx(i, nu):
        return jnp.maximum(jnp.minimum(i, nu[0] - 1), 0)

    def step_idx(i, s, nu):
        return jnp.where(i < nu[0], s, nstep - 1)

    def up_map(off):
        return lambda i, s, te, ns, nu: (te[tile_idx(i, nu)], 0,
                                         off + jnp.minimum(step_idx(i, s, nu), nf - 1))

    def down_map(off):
        return lambda i, s, te, ns, nu: (te[tile_idx(i, nu)], 0,
                                         off + jnp.maximum(step_idx(i, s, nu) - nf, 0))

    grid_spec = pltpu.PrefetchScalarGridSpec(
        num_scalar_prefetch=3,
        grid=(n_tiles, nstep),
        in_specs=[pl.BlockSpec((tm, dh), lambda i, s, te, ns, nu: (tile_idx(i, nu), 0)),
                  pl.BlockSpec((1, d, tf), up_map(0)), pl.BlockSpec((1, d, tf), up_map(nf)),
                  pl.BlockSpec((1, 1, tf), up_map(0)), pl.BlockSpec((1, 1, tf), up_map(nf)),
                  pl.BlockSpec((1, de, td), down_map(0)), pl.BlockSpec((1, de, td), down_map(nd)),
                  pl.BlockSpec((1, 1, td), down_map(0)), pl.BlockSpec((1, 1, td), down_map(nd))],
        out_specs=pl.BlockSpec((tm, td), lambda i, s, te, ns, nu: (
            tile_idx(i, nu), jnp.maximum(step_idx(i, s, nu) - nf, 0))),
        scratch_shapes=[pltpu.VMEM((nf, tm, tf), BF16)],
    )
    return pl.pallas_call(
        functools.partial(_ffn_body, nf=nf, tf=tf, ts=ts, nr=tm // ts),
        grid_spec=grid_spec,
        out_shape=jax.ShapeDtypeStruct((n_tiles * tm, dh), I32),
        compiler_params=_params(("arbitrary", "arbitrary")),
        name="expert_ffn",
    )(tile_e, tile_nsub, n_used, xs, w1, w1, b1r, b1r, w2, w2, b2r, b2r)


def _combine_body(dest_ref, y_ref, gate_ref, x1_ref, gm_ref, lg_ref, lb_ref, o_ref, ybuf, sem,
                  *, alpha_res):
    i = pl.program_id(0)
    tc = x1_ref.shape[0]

    def start(j, cc):
        for kk in range(TOP_K):
            d = dest_ref[(i * tc + j) * TOP_K + kk]
            pltpu.make_async_copy(y_ref.at[pl.ds(d, 1)], ybuf.at[kk, pl.ds(j, 1)],
                                  sem.at[0]).start()
        return cc

    lax.fori_loop(0, tc, start, 0)
    for kk in range(TOP_K):
        pltpu.make_async_copy(y_ref.at[pl.ds(0, tc)], ybuf.at[kk], sem.at[0]).wait()
    g = gate_ref[...]
    y_lo = jnp.zeros((tc, ybuf.shape[2]), F32)
    y_hi = jnp.zeros((tc, ybuf.shape[2]), F32)
    for kk in range(TOP_K):
        lo, hi = _unpack_bf16_pair(ybuf[kk])
        y_lo = y_lo + g[:, kk:kk + 1] * lo
        y_hi = y_hi + g[:, kk:kk + 1] * hi
    y = jnp.concatenate([y_lo, y_hi], axis=1)
    o_ref[...] = _layer_norm(alpha_res * x1_ref[...] + gm_ref[0] * y, lg_ref[...], lb_ref[...])


def _combine(dest_flat, y_sorted, gate, x1, mod3, ln_g, ln_b, s, alpha_res):
    n, d = x1.shape
    tc = _tile(s, 128, 8)
    per_b = s // tc
    row = lambda i, dr: (i, 0)
    const = lambda i, dr: (0, 0)
    grid_spec = pltpu.PrefetchScalarGridSpec(
        num_scalar_prefetch=1, grid=(n // tc,),
        in_specs=[pl.BlockSpec(memory_space=pl.ANY),
                  pl.BlockSpec((tc, LANES), row), pl.BlockSpec((tc, d), row),
                  pl.BlockSpec((1, 1, d), lambda i, dr: (i // per_b, 0, 5)),
                  pl.BlockSpec((1, d), const), pl.BlockSpec((1, d), const)],
        out_specs=pl.BlockSpec((tc, d), row),
        scratch_shapes=[pltpu.VMEM((TOP_K, tc, d // 2), I32), pltpu.SemaphoreType.DMA((1,))])
    return pl.pallas_call(
        functools.partial(_combine_body, alpha_res=alpha_res),
        grid_spec=grid_spec,
        out_shape=jax.ShapeDtypeStruct((n, d), F32),
        compiler_params=_params(("arbitrary",)),
        name="moe_combine",
    )(dest_flat, y_sorted, gate, x1, mod3, ln_g[None], ln_b[None])


MOE_SUBTILES_PER_TILE = 5


def _moe_tiles(n_tok, n_exp):
    mean_load = n_tok * TOP_K // n_exp
    ts = 16
    while ts * 2 <= min(256, mean_load // 4):
        ts *= 2
    return ts, MOE_SUBTILES_PER_TILE * ts
def _layer(x, c_pad, w_ada, b_ada, w_in, b_forget, q_norm_g, kv_norm_g, kidx_ln_g, kidx_ln_b,
           w_uq, w_uk, w_uv, w_iq, fox_out_g, dsa_out_g, w_o, ln1_g, ln1_b,
           w_router, b_router, w1, b1, w2, b2, ln2_g, ln2_b, rel_bias, alpha_res):
    b, s, d = x.shape
    n = b * s
    fh = b_forget.shape[0]
    ql = q_norm_g.shape[0]
    kvl = kv_norm_g.shape[0]
    dh = w_uk.shape[1]
    ih = w_iq.shape[1] // IDX_DIM
    ne = w_router.shape[1]
    fw = fh * HEAD_DIM
    dw = dh * HEAD_DIM
    assert fw == dw, "head groups share the attention kernel's column-block width"
    assert fh + ih <= LANES and ne <= LANES

    mod = _matmul(c_pad, w_ada, 6 * d, F32, bias=b_ada[None], a_silu=True,
                  tm=16, tn=512, tk=d, name="adaln")
    mod3 = mod[:b].reshape(b, 1, 6 * d)
    u = _modulate(x, mod3, 0, 1).reshape(n, d)

    w_in_t = jnp.swapaxes(w_in, 0, 1)
    q_scale = HEAD_DIM ** -0.5 * LOG2E
    qk = _matmul(u, w_in_t, 2 * fw, BF16, w_t=True, scale_cols=(fw, q_scale), tm=2048,
                 name="proj_qk")
    tail_w = -(-(ql + kvl + IDX_DIM + fh + ih) // LANES) * LANES
    tail = _matmul(u, w_in_t, tail_w, F32, w_t=True, col0=3 * fw, name="proj_tail")
    cqn, ckvn, kin, small, cum = _tail_prep(tail, b, s, ql, kvl, fh, ih, q_norm_g, kv_norm_g,
                                            kidx_ln_g, kidx_ln_b, b_forget)
    v_t = _matmul(w_in_t, u, n, BF16, w_t=True, a_rows=(2 * fw, fw), tm=2048, name="proj_vt")

    t = 256 if s % 256 == 0 else 128
    fox = _attention("fox", b, s, fh, t, (qk, 0), (qk, 1), v_t, fox_out_g, cum=cum)

    q_d = _matmul(cqn, w_uq, dw, BF16, scale_cols=(dw, q_scale), tm=2048, name="proj_qd")
    q_i = _matmul(cqn, w_iq, ih * IDX_DIM, BF16, tm=2048, name="proj_qi")
    k_d = _matmul(ckvn, w_uk.reshape(kvl, dw), dw, BF16, tm=2048, name="expand_k")
    vd_t = _matmul(jnp.swapaxes(w_uv.reshape(kvl, dw), 0, 1), ckvn, n, BF16, w_t=True, tm=2048,
                   name="expand_vt")
    k_sel = min(TOPK_MAX, s // 4)
    mask = _indexer(q_i, kin, small, b, s, ih, fh, t, k_sel)
    bias_t = _bias_tiles(rel_bias, dh, t)
    dsa = _attention("dsa", b, s, dh, t, (q_d, 0), (k_d, 0), vd_t, dsa_out_g,
                     mask=mask, bias_tiles=bias_t)

    mix = _matmul(fox, w_o, d, F32, a2=dsa, name="proj_out")

    wr_pad = jnp.zeros((d, LANES), F32).at[:, :ne].set(w_router)
    br_pad = jnp.zeros((1, LANES), F32).at[0, :ne].set(b_router)
    x1, u_packed, eidx, rank, gate, cnt = _router(x.reshape(n, d), mix, mod3, ln1_g, ln1_b,
                                                  wr_pad, br_pad, s, ne, alpha_res)
    ts, tm = _moe_tiles(n, ne)
    n_tiles = -(-(n * TOP_K) // tm) + ne
    counts = cnt[0, :ne]
    ntile = (counts + tm - 1) // tm
    tend = jnp.cumsum(ntile)
    tstart = tend - ntile
    n_used = tend[-1]
    per_tile = -(-counts // jnp.maximum(ntile, 1))
    per_tile = jnp.maximum((per_tile + ts - 1) // ts * ts, ts)
    e_sel, r_sel = eidx[:, :TOP_K], rank[:, :TOP_K]
    dest = (tstart[e_sel] + r_sel // per_tile[e_sel]) * tm + r_sel % per_tile[e_sel]
    dest_flat = dest.reshape(n * TOP_K).astype(I32)
    tid = jnp.minimum(jnp.arange(n_tiles, dtype=I32), n_used - 1)
    tile_e = jnp.minimum(jnp.searchsorted(tend, tid, side="right"), ne - 1).astype(I32)
    valid = jnp.clip(counts[tile_e] - (tid - tstart[tile_e]) * per_tile[tile_e], 0,
                     per_tile[tile_e])
    valid = jnp.where(jnp.arange(n_tiles) < n_used, valid, 0)
    tile_nsub = ((valid + ts - 1) // ts).astype(I32)
    xs = _dispatch(dest_flat, u_packed, n_tiles * tm)
    y_sorted = _expert_ffn(xs, w1, b1, w2, b2, tile_e, tile_nsub,
                           n_used.reshape(1).astype(I32), tm=tm, ts=ts, n_tiles=n_tiles)
    out = _combine(dest_flat, y_sorted, gate, x1, mod3, ln2_g, ln2_b, s, alpha_res)
    return out.reshape(b, s, d)


def kernel(x, c, w_ada, b_ada, w_in, b_forget, q_norm_g, kv_norm_g, kidx_ln_g, kidx_ln_b, w_uq, w_uk, w_uv, w_iq, fox_out_g, dsa_out_g, w_o, ln1_g, ln1_b, w_router, b_router, w1, b1, w2, b2, ln2_g, ln2_b, rel_bias):
    depth = w_ada.shape[0]
    alpha_res = (2 * depth) ** 0.25
    b, d = c.shape
    c_pad = jnp.zeros((16, d), F32).at[:b].set(c)
    for l in range(depth):
        x = _layer(x, c_pad, w_ada[l], b_ada[l], w_in[l], b_forget[l], q_norm_g[l], kv_norm_g[l],
                   kidx_ln_g[l], kidx_ln_b[l], w_uq[l], w_uk[l], w_uv[l], w_iq[l], fox_out_g[l],
                   dsa_out_g[l], w_o[l], ln1_g[l], ln1_b[l], w_router[l], b_router[l],
                   w1[l], b1[l], w2[l], b2[l], ln2_g[l], ln2_b[l], rel_bias, alpha_res)
    return x
```

```python
import functools
import math

import numpy as np
import jax
import jax.numpy as jnp
from jax import lax
from jax.experimental import pallas as pl
from jax.experimental.pallas import tpu as pltpu

F32 = jnp.float32
BF16 = jnp.bfloat16
I32 = jnp.int32

LANES = 128
HEAD_DIM = 128
IDX_DIM = 128
TOPK_MAX = 256
N_BUCKETS = 32
MAX_DISTANCE = 128
TOP_K = 4
SWIGLU_LIMIT = 7.0
SWIGLU_ALPHA = 1.702
NEG_BIG = -1e30
LOG2E = math.log2(math.e)
VMEM_LIMIT = 56 * 1024 * 1024
INT_MIN = -2 ** 31
DMA_ISSUE_UNROLL = 8


def _tile(n, pref, unit=LANES):
    if n <= pref:
        return n
    t = (pref // unit) * unit
    while t > unit and n % t:
        t -= unit
    assert n % t == 0, (n, pref)
    return t


def _pack_bf16_pair(lo, hi):
    lo_bits = lax.bitcast_convert_type(lo.astype(BF16).astype(F32), I32)
    hi_bits = lax.bitcast_convert_type(hi.astype(BF16).astype(F32), I32)
    return lax.shift_right_logical(lo_bits, 16) | (hi_bits & jnp.int32(-65536))


def _unpack_bf16_pair(words):
    lo = lax.bitcast_convert_type(lax.shift_left(words, 16), F32)
    hi = lax.bitcast_convert_type(words & jnp.int32(-65536), F32)
    return lo, hi


def _params(sem, vmem=VMEM_LIMIT):
    return pltpu.CompilerParams(dimension_semantics=sem, vmem_limit_bytes=vmem)


def _mm_body(*refs, nk, nk1, a_silu, has_bias, col_block0, w_cols_if_overhang, w_t, out_scale):
    refs = list(refs)
    a_ref = refs.pop(0)
    a2_ref = refs.pop(0) if nk1 < nk else None
    w_ref = refs.pop(0)
    b_ref = refs.pop(0) if has_bias else None
    o_ref, acc_ref = refs
    k = pl.program_id(2)

    @pl.when(k == 0)
    def _():
        acc_ref[...] = jnp.zeros_like(acc_ref)

    def accumulate(src_ref):
        a = src_ref[...]
        if a_silu:
            a = a.astype(F32)
            a = a * jax.nn.sigmoid(a)
        w = w_ref[...]
        n_axis = 0 if w_t else 1
        if w_cols_if_overhang is not None:
            tn = w.shape[n_axis]
            col = lax.broadcasted_iota(I32, w.shape, n_axis) + (pl.program_id(1) + col_block0) * tn
            w = jnp.where(col < w_cols_if_overhang, w, 0.0)
        acc_ref[...] += lax.dot_general(a.astype(BF16), w.astype(BF16),
                                        (((1,), (1 if w_t else 0,)), ((), ())),
                                        preferred_element_type=F32)

    if a2_ref is None:
        accumulate(a_ref)
    else:
        pl.when(k < nk1)(lambda: accumulate(a_ref))
        pl.when(k >= nk1)(lambda: accumulate(a2_ref))

    @pl.when(k == nk - 1)
    def _():
        r = acc_ref[...]
        if has_bias:
            r = r + b_ref[...]
        if out_scale is not None:
            n_blocks, factor = out_scale
            r = r * jnp.where(pl.program_id(1) < n_blocks, factor, 1.0)
        o_ref[...] = r.astype(o_ref.dtype)


def _matmul(a, w, n_out, out_dtype, *, a2=None, bias=None, a_silu=False, col0=0, w_t=False,
            a_rows=None, scale_cols=None, tm=1024, tn=1024, tk=1024, name="matmul"):
    m, kdim = a.shape
    row0 = 0
    if a_rows is not None:
        row0, m = a_rows
    w_cols = w.shape[0] if w_t else w.shape[1]
    tm = _tile(math.gcd(m, row0), tm, 16)
    row_block0 = row0 // tm
    tn = _tile(math.gcd(n_out, col0, scale_cols[0] if scale_cols else 0), tn)
    tk = _tile(kdim, tk)
    nk1 = kdim // tk
    nk = nk1 if a2 is None else nk1 + a2.shape[1] // tk
    col_block0 = col0 // tn
    overhang = col0 + n_out > w_cols
    out_scale = None
    if scale_cols is not None:
        assert scale_cols[0] % tn == 0
        out_scale = (scale_cols[0] // tn, scale_cols[1])
    in_specs = [pl.BlockSpec((tm, tk), lambda i, j, k: (i + row_block0, jnp.minimum(k, nk1 - 1)))]
    args = [a]
    if a2 is not None:
        assert a2.shape[1] % tk == 0
        in_specs.append(pl.BlockSpec((tm, tk), lambda i, j, k: (i, jnp.maximum(k - nk1, 0))))
        args.append(a2)
    if w_t:
        in_specs.append(pl.BlockSpec((tn, tk), lambda i, j, k: (j + col_block0, k)))
    else:
        in_specs.append(pl.BlockSpec((tk, tn), lambda i, j, k: (k, j + col_block0)))
    args.append(w)
    if bias is not None:
        in_specs.append(pl.BlockSpec((1, tn), lambda i, j, k: (0, j)))
        args.append(bias)
    return pl.pallas_call(
        functools.partial(_mm_body, nk=nk, nk1=nk1, a_silu=a_silu, has_bias=bias is not None,
                          col_block0=col_block0, w_t=w_t, out_scale=out_scale,
                          w_cols_if_overhang=w_cols if overhang else None),
        grid=(m // tm, n_out // tn, nk),
        in_specs=in_specs,
        out_specs=pl.BlockSpec((tm, tn), lambda i, j, k: (i, j)),
        out_shape=jax.ShapeDtypeStruct((m, n_out), out_dtype),
        scratch_shapes=[pltpu.VMEM((tm, tn), F32)],
        compiler_params=_params(("parallel", "parallel", "arbitrary")),
        name=name,
    )(*args)


def _modulate_body(x_ref, sh_ref, sc_ref, o_ref):
    o_ref[0] = (x_ref[0] * (1.0 + sc_ref[0]) + sh_ref[0]).astype(o_ref.dtype)


def _modulate(x, mod3, shift_chunk, scale_chunk):
    b, s, d = x.shape
    ts = _tile(s, 512, 16)
    return pl.pallas_call(
        _modulate_body,
        grid=(b, s // ts),
        in_specs=[pl.BlockSpec((1, ts, d), lambda i, j: (i, j, 0)),
                  pl.BlockSpec((1, 1, d), lambda i, j: (i, 0, shift_chunk)),
                  pl.BlockSpec((1, 1, d), lambda i, j: (i, 0, scale_chunk))],
        out_specs=pl.BlockSpec((1, ts, d), lambda i, j: (i, j, 0)),
        out_shape=jax.ShapeDtypeStruct((b, s, d), BF16),
        compiler_params=_params(("parallel", "parallel")),
        name="modulate",
    )(x, mod3, mod3)


def _tail_body(t_ref, qg_ref, kvg_ref, lng_ref, lnb_ref, bf_ref,
               cq_ref, ckv_ref, kin_ref, small_ref, cum_ref, carry_ref, *, fh, ql, kvl, w_scale):
    j = pl.program_id(1)
    ts = t_ref.shape[0]

    @pl.when(j == 0)
    def _():
        carry_ref[...] = jnp.zeros_like(carry_ref)

    cq = t_ref[:, fh:fh + ql]
    cq_ref[...] = (cq * lax.rsqrt(jnp.mean(cq * cq, axis=-1, keepdims=True) + 1e-6)
                   * qg_ref[...]).astype(cq_ref.dtype)
    ckv = t_ref[:, fh + ql:fh + ql + kvl]
    ckv_ref[...] = (ckv * lax.rsqrt(jnp.mean(ckv * ckv, axis=-1, keepdims=True) + 1e-6)
                    * kvg_ref[...]).astype(ckv_ref.dtype)
    ki = t_ref[:, fh + ql + kvl:fh + ql + kvl + IDX_DIM]
    mu = jnp.mean(ki, axis=-1, keepdims=True)
    kc = ki - mu
    var = jnp.mean(kc * kc, axis=-1, keepdims=True)
    kin_ref[...] = (kc * lax.rsqrt(var + 1e-5) * lng_ref[...] + lnb_ref[...]).astype(kin_ref.dtype)

    small_ref[...] = t_ref[:, ql + kvl + IDX_DIM:ql + kvl + IDX_DIM + LANES] * w_scale
    z = t_ref[:, 0:LANES] + bf_ref[...]
    log_f = jnp.minimum(z, 0.0) - jnp.log(1.0 + jnp.exp(-jnp.abs(z)))
    row = lax.broadcasted_iota(I32, (ts, ts), 0)
    col = lax.broadcasted_iota(I32, (ts, ts), 1)
    tri = jnp.where(col <= row, 1.0, 0.0).astype(F32)
    incl = jnp.dot(tri, log_f, preferred_element_type=F32,
                   precision=lax.Precision.HIGHEST) + carry_ref[0:1, :]
    carry_ref[0:1, :] = incl[ts - 1:ts, :]
    cum_ref[...] = incl * (-LOG2E)


def _tail_prep(tail, b, s, ql, kvl, fh, ih, q_norm_g, kv_norm_g, ln_g, ln_b, b_forget):
    n = tail.shape[0]
    assert ql % LANES == 0 and kvl % LANES == 0
    ts = _tile(s, 256)
    nj = s // ts
    bf_pad = jnp.zeros((1, LANES), F32).at[0, :fh].set(b_forget)
    w_scale = (ih ** -0.5) * (IDX_DIM ** -0.5)
    row = lambda i, j: (i * nj + j, 0)
    const = lambda i, j: (0, 0)
    return pl.pallas_call(
        functools.partial(_tail_body, fh=fh, ql=ql, kvl=kvl, w_scale=w_scale),
        grid=(b, nj),
        in_specs=[pl.BlockSpec((ts, tail.shape[1]), row),
                  pl.BlockSpec((1, ql), const), pl.BlockSpec((1, kvl), const),
                  pl.BlockSpec((1, IDX_DIM), const), pl.BlockSpec((1, IDX_DIM), const),
                  pl.BlockSpec((1, LANES), const)],
        out_specs=[pl.BlockSpec((ts, ql), row), pl.BlockSpec((ts, kvl), row),
                   pl.BlockSpec((ts, IDX_DIM), row), pl.BlockSpec((ts, LANES), row),
                   pl.BlockSpec((ts, LANES), row)],
        out_shape=[jax.ShapeDtypeStruct((n, ql), BF16), jax.ShapeDtypeStruct((n, kvl), BF16),
                   jax.ShapeDtypeStruct((n, IDX_DIM), BF16), jax.ShapeDtypeStruct((n, LANES), F32),
                   jax.ShapeDtypeStruct((n, LANES), F32)],
        scratch_shapes=[pltpu.VMEM((8, LANES), F32)],
        compiler_params=_params(("parallel", "arbitrary")),
        name="tail_prep",
    )(tail, q_norm_g[None], kv_norm_g[None], ln_g[None], ln_b[None], bf_pad)


_MAX_EXACT = N_BUCKETS // 2
_BUCKET_THRESHOLDS = tuple(
    int(math.ceil(_MAX_EXACT * (MAX_DISTANCE / _MAX_EXACT) ** (k / (N_BUCKETS - _MAX_EXACT)) - 1e-9))
    for k in range(1, N_BUCKETS - _MAX_EXACT))


def _bias_body(rb_ref, o_ref, *, t):
    which = pl.program_id(0)
    h = pl.program_id(1)
    row = lax.broadcasted_iota(I32, (t, t), 0)
    col = lax.broadcasted_iota(I32, (t, t), 1)
    d = jnp.maximum(which * t + col - row, 0)
    large = jnp.full((t, t), _MAX_EXACT, I32)
    for thr in _BUCKET_THRESHOLDS:
        large = large + jnp.where(d >= thr, 1, 0)
    bucket = jnp.where(d < _MAX_EXACT, d, large)
    bias = jnp.zeros((t, t), F32)
    for bkt in range(N_BUCKETS):
        bias = jnp.where(bucket == bkt, rb_ref[bkt, h], bias)
    o_ref[0, 0] = (bias - rb_ref[N_BUCKETS - 1, h]) * LOG2E


def _bias_tiles(rel_bias, nh, t):
    assert 2 * t - (t - 1) >= _BUCKET_THRESHOLDS[-1], "key tiles two or more away are all last-bucket"
    return pl.pallas_call(
        functools.partial(_bias_body, t=t),
        grid=(2, nh),
        in_specs=[pl.BlockSpec(memory_space=pltpu.SMEM)],
        out_specs=pl.BlockSpec((1, 1, t, t), lambda w, h: (w, h, 0, 0)),
        out_shape=jax.ShapeDtypeStruct((2, nh, t, t), F32),
        compiler_params=_params(("parallel", "parallel")),
        name="bias_tiles",
    )(rel_bias)


ATTN_QCOLS = 128


def _attn_body(qi_ref, kj_ref, *refs, nh, t, mode):
    if mode == "fox":
        q_ref, k_ref, vt_ref, g_ref, cum_ref, o_ref, m_ref, l_ref, acc_ref, neg_ref = refs
    else:
        q_ref, k_ref, vt_ref, g_ref, mask_ref, bt_ref, o_ref, m_ref, l_ref, acc_ref, neg_ref = refs
    p = pl.program_id(1)
    qi = qi_ref[p]
    kj = kj_ref[p]
    nblk = t // ATTN_QCOLS

    @pl.when(kj == 0)
    def _():
        m_ref[...] = jnp.full_like(m_ref, NEG_BIG)
        l_ref[...] = jnp.zeros_like(l_ref)
        acc_ref[...] = jnp.zeros_like(acc_ref)

    def heads(addend):
        for h in range(nh):
            sl = slice(h * HEAD_DIM, (h + 1) * HEAD_DIM)
            for r in range(nblk):
                qs = slice(r * ATTN_QCOLS, (r + 1) * ATTN_QCOLS)
                idx = h * nblk + r
                s = lax.dot_general(k_ref[:, sl], q_ref[qs, sl], (((1,), (1,)), ((), ())),
                                    preferred_element_type=F32)
                s = s + addend(h, qs)
                m_prev = m_ref[idx]
                m_next = jnp.maximum(m_prev, jnp.max(s, axis=0, keepdims=True))
                alpha = jnp.exp2(m_prev - m_next)
                pr = jnp.exp2(s - m_next)
                l_ref[idx] = alpha * l_ref[idx] + jnp.sum(pr, axis=0, keepdims=True)
                acc_ref[idx] = acc_ref[idx] * alpha + jnp.dot(
                    vt_ref[sl, :], pr.astype(BF16), preferred_element_type=F32)
                m_ref[idx] = m_next

    key_i = lax.broadcasted_iota(I32, (t, t), 0)
    qry_i = lax.broadcasted_iota(I32, (t, t), 1)
    if mode == "fox":
        @pl.when(kj == qi)
        def _():
            neg_ref[...] = jnp.where(key_i <= qry_i, 0.0, NEG_BIG).astype(F32)
            heads(lambda h, qs: neg_ref[:, qs] + cum_ref[:, h:h + 1])

        @pl.when(kj != qi)
        def _():
            heads(lambda h, qs: cum_ref[:, h:h + 1])
    else:
        neg_ref[...] = ((mask_ref[0, 0].astype(F32) - 1.0) * (-NEG_BIG)).T

        @pl.when(qi - kj >= 2)
        def _():
            heads(lambda h, qs: neg_ref[:, qs])

        @pl.when(qi - kj < 2)
        def _():
            near = qi - kj
            heads(lambda h, qs: neg_ref[:, qs] + bt_ref[near, h, :, qs])

    @pl.when(kj == qi)
    def _():
        for r in range(nblk):
            qs = slice(r * ATTN_QCOLS, (r + 1) * ATTN_QCOLS)
            ss = jnp.zeros((1, ATTN_QCOLS), F32)
            for h in range(nh):
                o = acc_ref[h * nblk + r] / l_ref[h * nblk + r]
                acc_ref[h * nblk + r] = o
                ss = ss + jnp.sum(o * o, axis=0, keepdims=True)
            rinv = lax.rsqrt(ss / (nh * HEAD_DIM) + 1e-6)
            for h in range(nh):
                sl = slice(h * HEAD_DIM, (h + 1) * HEAD_DIM)
                o_ref[qs, sl] = ((acc_ref[h * nblk + r] * rinv).T * g_ref[:, sl]).astype(o_ref.dtype)


def _attention(mode, b, s, nh, t, q_src, k_src, vt, gain, *, cum=None, mask=None, bias_tiles=None):
    nq = s // t
    pairs = [(i, j) for i in range(nq) for j in range(i + 1)]
    qi_arr = jnp.asarray(np.array([pq for pq, _ in pairs], np.int32))
    kj_arr = jnp.asarray(np.array([pk for _, pk in pairs], np.int32))
    hw = nh * HEAD_DIM
    n = b * s
    nblk = t // ATTN_QCOLS

    def qmap(cb):
        return lambda bi, p, qi, kj: (bi * nq + qi[p], cb)

    def kmap(cb):
        return lambda bi, p, qi, kj: (bi * nq + kj[p], cb)

    in_specs = [pl.BlockSpec((t, hw), qmap(q_src[1])),
                pl.BlockSpec((t, hw), kmap(k_src[1])),
                pl.BlockSpec((hw, t), lambda bi, p, qi, kj: (0, bi * nq + kj[p])),
                pl.BlockSpec((1, hw), lambda bi, p, qi, kj: (0, 0))]
    args = [q_src[0], k_src[0], vt, gain[None]]
    if mode == "fox":
        in_specs.append(pl.BlockSpec((t, LANES), kmap(0)))
        args.append(cum)
    else:
        in_specs += [pl.BlockSpec((1, 1, t, t), lambda bi, p, qi, kj: (bi, kj[p], qi[p], 0)),
                     pl.BlockSpec(memory_space=pltpu.VMEM)]
        args += [mask, bias_tiles]
    grid_spec = pltpu.PrefetchScalarGridSpec(
        num_scalar_prefetch=2,
        grid=(b, len(pairs)),
        in_specs=in_specs,
        out_specs=pl.BlockSpec((t, hw), lambda bi, p, qi, kj: (bi * nq + qi[p], 0)),
        scratch_shapes=[pltpu.VMEM((nh * nblk, 1, ATTN_QCOLS), F32),
                        pltpu.VMEM((nh * nblk, 1, ATTN_QCOLS), F32),
                        pltpu.VMEM((nh * nblk, HEAD_DIM, ATTN_QCOLS), F32),
                        pltpu.VMEM((t, t), F32)],
    )
    return pl.pallas_call(
        functools.partial(_attn_body, nh=nh, t=t, mode=mode),
        grid_spec=grid_spec,
        out_shape=jax.ShapeDtypeStruct((n, hw), BF16),
        compiler_params=_params(("parallel", "arbitrary")),
        name=mode + "_attention",
    )(qi_arr, kj_arr, *args)


IDX_ROWS = 128


def _float_key(x):
    bits = lax.bitcast_convert_type(x, I32)
    return bits ^ (lax.shift_right_arithmetic(bits, 31) & 0x7FFFFFFF)


def _indexer_body(q_ref, k_ref, w_ref, o_ref, key_ref, *, ih, w_lane0, tq, ck, nchunk, k_sel):
    i = pl.program_id(1)
    n_valid = ((i + 1) * tq + ck - 1) // ck
    row = lax.broadcasted_iota(I32, (tq, ck), 0) + i * tq
    lane_col = lax.broadcasted_iota(I32, (tq, ck), 1)

    def score_chunk(c, carry):
        kc = k_ref[0, pl.ds(pl.multiple_of(c * ck, ck), ck), :]
        for r in range(tq // IDX_ROWS):
            rows = slice(r * IDX_ROWS, (r + 1) * IDX_ROWS)
            w_blk = w_ref[rows, :]
            acc = jnp.zeros((IDX_ROWS, ck), F32)
            for h in range(ih):
                sh = lax.dot_general(q_ref[rows, h * IDX_DIM:(h + 1) * IDX_DIM], kc,
                                     (((1,), (1,)), ((), ())), preferred_element_type=F32)
                acc = acc + w_blk[:, w_lane0 + h:w_lane0 + h + 1] * jnp.maximum(sh, 0.0)
            causal = (lax.broadcasted_iota(I32, (IDX_ROWS, ck), 1) + c * ck
                      <= lax.broadcasted_iota(I32, (IDX_ROWS, ck), 0) + (i * tq + r * IDX_ROWS))
            acc = jnp.where(causal, acc, -jnp.inf)
            key_ref[c, rows, :] = _float_key(acc)
        return carry

    lax.fori_loop(0, n_valid, score_chunk, 0)

    def count(pred_fn, level):
        def body(c, acc):
            ind = jnp.where(pred_fn(key_ref[c], level), 1, 0)
            for part in range(ck // LANES):
                acc = acc + ind[:, part * LANES:(part + 1) * LANES]
            return acc
        acc = lax.fori_loop(0, n_valid, body, jnp.zeros((tq, LANES), I32))
        return jnp.sum(acc, axis=1, keepdims=True)

    ge = lambda kv, lv: kv >= lv
    t0 = jnp.full((tq, 1), INT_MIN, I32)
    zero = jnp.zeros((tq, 1), I32)
    thr = jnp.where(count(ge, zero) >= k_sel, zero, t0)

    def bit_step(bi, thr):
        cand = thr | lax.shift_left(jnp.int32(1), 30 - bi)
        return jnp.where(count(ge, cand) >= k_sel, cand, thr)

    thr = lax.fori_loop(0, 31, bit_step, thr)
    need = (k_sel - count(lambda kv, lv: kv > lv, thr)).astype(F32)

    urow = lax.broadcasted_iota(I32, (ck, ck), 0)
    ucol = lax.broadcasted_iota(I32, (ck, ck), 1)
    upper = jnp.where(urow <= ucol, 1.0, 0.0).astype(BF16)
    ones = jnp.ones((ck, LANES), BF16)

    def select_chunk(c, offset):
        kv = key_ref[c]
        eq = kv == thr
        eqb = jnp.where(eq, 1.0, 0.0).astype(BF16)
        rank = jnp.dot(eqb, upper, preferred_element_type=F32) + offset[:, 0:1]
        take = (kv > thr) | (eq & (rank <= need))
        take = take & (lane_col + c * ck <= row)
        o_ref[0, c] = jnp.where(take, 1, 0).astype(jnp.int8)
        return offset + jnp.dot(eqb, ones, preferred_element_type=F32)

    lax.fori_loop(0, n_valid, select_chunk, jnp.zeros((tq, LANES), F32))

    def zero_chunk(c, carry):
        o_ref[0, c] = jnp.zeros((tq, ck), jnp.int8)
        return carry

    lax.fori_loop(n_valid, nchunk, zero_chunk, 0)


def _indexer(q_idx, k_in, small, b, s, ih, w_lane0, ck, k_sel):
    tq = _tile(s, 256, IDX_ROWS)
    nq = s // tq
    nchunk = s // ck
    k3 = k_in.reshape(b, s, IDX_DIM)
    return pl.pallas_call(
        functools.partial(_indexer_body, ih=ih, w_lane0=w_lane0, tq=tq, ck=ck, nchunk=nchunk,
                          k_sel=k_sel),
        grid=(b, nq),
        in_specs=[pl.BlockSpec((tq, ih * IDX_DIM), lambda bi, i: (bi * nq + i, 0)),
                  pl.BlockSpec((1, s, IDX_DIM), lambda bi, i: (bi, 0, 0)),
                  pl.BlockSpec((tq, LANES), lambda bi, i: (bi * nq + i, 0))],
        out_specs=pl.BlockSpec((1, nchunk, tq, ck), lambda bi, i: (bi, 0, i, 0)),
        out_shape=jax.ShapeDtypeStruct((b, nchunk, s, ck), jnp.int8),
        scratch_shapes=[pltpu.VMEM((nchunk, tq, ck), I32)],
        compiler_params=_params(("parallel", "parallel")),
        name="indexer_topk",
    )(q_idx, k3, small)


def _layer_norm(h, g, bvec):
    mu = jnp.mean(h, axis=-1, keepdims=True)
    hc = h - mu
    var = jnp.mean(hc * hc, axis=-1, keepdims=True)
    return hc * lax.rsqrt(var + 1e-5) * g + bvec


def _router_body(x_ref, mix_ref, ga_ref, shm_ref, scm_ref, lg_ref, lb_ref, wr_ref, br_ref,
                 x1_ref, up_ref, eidx_ref, rank_ref, gate_ref, cnt_ref, carry_ref,
                 *, alpha_res, ne, nsteps):
    i = pl.program_id(0)
    tm, d = x_ref.shape

    @pl.when(i == 0)
    def _():
        carry_ref[...] = jnp.zeros_like(carry_ref)

    x1 = _layer_norm(alpha_res * x_ref[...] + ga_ref[0] * mix_ref[...], lg_ref[...], lb_ref[...])
    x1_ref[...] = x1
    u = x1 * (1.0 + scm_ref[0]) + shm_ref[0]
    up_ref[...] = _pack_bf16_pair(u[:, :d // 2], u[:, d // 2:])

    logits = jnp.dot(u, wr_ref[...], preferred_element_type=F32,
                     precision=lax.Precision.HIGHEST) + br_ref[...]
    lane = lax.broadcasted_iota(I32, (tm, LANES), 1)
    lane_f = lane.astype(F32)
    work = jnp.where(lane < ne, logits, -jnp.inf)
    sel = jnp.zeros((tm, LANES), F32)
    idxs, vals, hits = [], [], []
    for _ in range(TOP_K):
        mx = jnp.max(work, axis=1, keepdims=True)
        ik = jnp.min(jnp.where(work == mx, lane_f, float(LANES)), axis=1, keepdims=True)
        hit = lane_f == ik
        sel = jnp.where(hit, 1.0, sel)
        work = jnp.where(hit, -jnp.inf, work)
        idxs.append(ik.astype(I32))
        vals.append(mx)
        hits.append(hit)
    exps = [jnp.exp(v - vals[0]) for v in vals]
    denom = exps[0] + exps[1] + exps[2] + exps[3]

    row = lax.broadcasted_iota(I32, (tm, tm), 0)
    col = lax.broadcasted_iota(I32, (tm, tm), 1)
    tri = jnp.where(col < row, 1.0, 0.0).astype(BF16)
    before = jnp.dot(tri, sel.astype(BF16), preferred_element_type=F32) + carry_ref[0:1, :]
    total = carry_ref[0:1, :] + jnp.sum(sel, axis=0, keepdims=True)
    carry_ref[0:1, :] = total

    eidx = jnp.zeros((tm, LANES), I32)
    rank = jnp.zeros((tm, LANES), I32)
    gate = jnp.zeros((tm, LANES), F32)
    for kk in range(TOP_K):
        rk = jnp.sum(jnp.where(hits[kk], before, 0.0), axis=1, keepdims=True)
        eidx = jnp.where(lane == kk, idxs[kk], eidx)
        rank = jnp.where(lane == kk, rk.astype(I32), rank)
        gate = jnp.where(lane == kk, exps[kk] / denom, gate)
    eidx_ref[...] = eidx
    rank_ref[...] = rank
    gate_ref[...] = gate
    cnt_ref[...] = jnp.broadcast_to(total, cnt_ref.shape).astype(I32)


def _router(x2d, mix, mod3, ln_g, ln_b, w_router_pad, b_router_pad, s, ne, alpha_res):
    n, d = x2d.shape
    tm = _tile(s, 256, 16)
    nsteps = n // tm
    per_b = s // tm
    row = lambda i: (i, 0)
    const = lambda i: (0, 0)
    modspec = lambda chunk: pl.BlockSpec((1, 1, d), lambda i: (i // per_b, 0, chunk))
    return pl.pallas_call(
        functools.partial(_router_body, alpha_res=alpha_res, ne=ne, nsteps=nsteps),
        grid=(nsteps,),
        in_specs=[pl.BlockSpec((tm, d), row), pl.BlockSpec((tm, d), row),
                  modspec(2), modspec(3), modspec(4),
                  pl.BlockSpec((1, d), const), pl.BlockSpec((1, d), const),
                  pl.BlockSpec((d, LANES), const), pl.BlockSpec((1, LANES), const)],
        out_specs=[pl.BlockSpec((tm, d), row), pl.BlockSpec((tm, d // 2), row),
                   pl.BlockSpec((tm, LANES), row), pl.BlockSpec((tm, LANES), row),
                   pl.BlockSpec((tm, LANES), row), pl.BlockSpec((8, LANES), const)],
        out_shape=[jax.ShapeDtypeStruct((n, d), F32), jax.ShapeDtypeStruct((n, d // 2), I32),
                   jax.ShapeDtypeStruct((n, LANES), I32), jax.ShapeDtypeStruct((n, LANES), I32),
                   jax.ShapeDtypeStruct((n, LANES), F32), jax.ShapeDtypeStruct((8, LANES), I32)],
        scratch_shapes=[pltpu.VMEM((8, LANES), F32)],
        compiler_params=_params(("arbitrary",)),
        name="ln1_router",
    )(x2d, mix, mod3, mod3, mod3, ln_g[None], ln_b[None], w_router_pad, b_router_pad)


def _dispatch_body(dest_ref, u_ref, xs_ref, sem):
    i = pl.program_id(0)
    tc = u_ref.shape[0]

    def start(j, cc):
        for kk in range(TOP_K):
            d = dest_ref[(i * tc + j) * TOP_K + kk]
            pltpu.make_async_copy(u_ref.at[pl.ds(j, 1)], xs_ref.at[pl.ds(d, 1)], sem.at[0]).start()
        return cc

    lax.fori_loop(0, tc, start, 0, unroll=DMA_ISSUE_UNROLL)
    for _ in range(TOP_K):
        pltpu.make_async_copy(u_ref, xs_ref.at[pl.ds(0, tc)], sem.at[0]).wait()


def _dispatch(dest_flat, u_packed, n_slots):
    n, dw = u_packed.shape
    tc = _tile(n, 256, 8)
    grid_spec = pltpu.PrefetchScalarGridSpec(
        num_scalar_prefetch=1, grid=(n // tc,),
        in_specs=[pl.BlockSpec((tc, dw), lambda i, dr: (i, 0))],
        out_specs=pl.BlockSpec(memory_space=pl.ANY),
        scratch_shapes=[pltpu.SemaphoreType.DMA((1,))])
    return pl.pallas_call(
        _dispatch_body,
        grid_spec=grid_spec,
        out_shape=jax.ShapeDtypeStruct((n_slots, dw), I32),
        compiler_params=pltpu.CompilerParams(dimension_semantics=("arbitrary",),
                                             has_side_effects=True),
        name="moe_dispatch",
    )(dest_flat, u_packed)


def _ffn_body(te_ref, nsub_ref, nused_ref, x_ref, w1g_ref, w1u_ref, b1g_ref, b1u_ref,
              w2lo_ref, w2hi_ref, b2lo_ref, b2hi_ref, o_ref, h_ref, *, nf, tf, ts, nr):
    i = pl.program_id(0)
    st = pl.program_id(1)
    tm, dh = x_ref.shape
    nsub = nsub_ref[i]

    def up_step(rows):
        lo, hi = _unpack_bf16_pair(x_ref[0:rows, :])
        lo, hi = lo.astype(BF16), hi.astype(BF16)

        def proj(w_ref, b_ref):
            return (jnp.dot(lo, w_ref[0, 0:dh, :].astype(BF16), preferred_element_type=F32)
                    + jnp.dot(hi, w_ref[0, dh:2 * dh, :].astype(BF16), preferred_element_type=F32)
                    + b_ref[0])
        gate = jnp.minimum(proj(w1g_ref, b1g_ref), SWIGLU_LIMIT)
        up = jnp.clip(proj(w1u_ref, b1u_ref), -SWIGLU_LIMIT, SWIGLU_LIMIT)
        act = (up + 1.0) * gate * jax.nn.sigmoid(SWIGLU_ALPHA * gate)
        h_ref[st, 0:rows, :] = act.astype(BF16)

    def down_step(rows):
        def proj(w_ref, b_ref):
            acc = jnp.broadcast_to(b_ref[0], (rows, o_ref.shape[1])).astype(F32)
            for f in range(nf):
                acc = acc + jnp.dot(h_ref[f, 0:rows, :],
                                    w_ref[0, f * tf:(f + 1) * tf, :].astype(BF16),
                                    preferred_element_type=F32)
            return acc
        o_ref[0:rows, :] = _pack_bf16_pair(proj(w2lo_ref, b2lo_ref), proj(w2hi_ref, b2hi_ref))
        if rows < tm:
            o_ref[rows:tm, :] = jnp.zeros((tm - rows, o_ref.shape[1]), I32)

    for r in range(1, nr + 1):
        @pl.when(jnp.logical_and(nsub == r, st < nf))
        def _(r=r):
            up_step(r * ts)

        @pl.when(jnp.logical_and(nsub == r, st >= nf))
        def _(r=r):
            down_step(r * ts)


def _expert_ffn(xs, w1, b1, w2, b2, tile_e, tile_nsub, n_used, *, tm, ts, n_tiles):
    ne, d, de2 = w1.shape
    de = de2 // 2
    dh = d // 2
    tf = _tile(de, 256)
    td = _tile(dh, 256)
    nf, nd = de // tf, dh // td
    nstep = nf + nd
    b1r = b1.reshape(ne, 1, de2)
    b2r = b2.reshape(ne, 1, d)

    def tile_idx(i, nu):
        return jnp.maximum(jnp.minimum(i, nu[0] - 1), 0)

    def step_idx(i, s, nu):
        return jnp.where(i < nu[0], s, nstep - 1)

    def up_map(off):
        return lambda i, s, te, ns, nu: (te[tile_idx(i, nu)], 0,
                                         off + jnp.minimum(step_idx(i, s, nu), nf - 1))

    def down_map(off):
        return lambda i, s, te, ns, nu: (te[tile_idx(i, nu)], 0,
                                         off + jnp.maximum(step_idx(i, s, nu) - nf, 0))

    grid_spec = pltpu.PrefetchScalarGridSpec(
        num_scalar_prefetch=3,
        grid=(n_used[0], nstep),
        in_specs=[pl.BlockSpec((tm, dh), lambda i, s, te, ns, nu: (tile_idx(i, nu), 0)),
                  pl.BlockSpec((1, d, tf), up_map(0)), pl.BlockSpec((1, d, tf), up_map(nf)),
                  pl.BlockSpec((1, 1, tf), up_map(0)), pl.BlockSpec((1, 1, tf), up_map(nf)),
                  pl.BlockSpec((1, de, td), down_map(0)), pl.BlockSpec((1, de, td), down_map(nd)),
                  pl.BlockSpec((1, 1, td), down_map(0)), pl.BlockSpec((1, 1, td), down_map(nd))],
        out_specs=pl.BlockSpec((tm, td), lambda i, s, te, ns, nu: (
            tile_idx(i, nu), jnp.maximum(step_idx(i, s, nu) - nf, 0))),
        scratch_shapes=[pltpu.VMEM((nf, tm, tf), BF16)],
    )
    return pl.pallas_call(
        functools.partial(_ffn_body, nf=nf, tf=tf, ts=ts, nr=tm // ts),
        grid_spec=grid_spec,
        out_shape=jax.ShapeDtypeStruct((n_tiles * tm, dh), I32),
        compiler_params=_params(("arbitrary", "arbitrary")),
        name="expert_ffn",
    )(tile_e, tile_nsub, n_used, xs, w1, w1, b1r, b1r, w2, w2, b2r, b2r)


def _combine_body(dest_ref, y_ref, gate_ref, x1_ref, gm_ref, lg_ref, lb_ref, o_ref, ybuf, sem,
                  *, alpha_res):
    i = pl.program_id(0)
    tc = x1_ref.shape[0]

    def start(j, cc):
        for kk in range(TOP_K):
            d = dest_ref[(i * tc + j) * TOP_K + kk]
            pltpu.make_async_copy(y_ref.at[pl.ds(d, 1)], ybuf.at[kk, pl.ds(j, 1)],
                                  sem.at[0]).start()
        return cc

    lax.fori_loop(0, tc, start, 0, unroll=DMA_ISSUE_UNROLL)
    for kk in range(TOP_K):
        pltpu.make_async_copy(y_ref.at[pl.ds(0, tc)], ybuf.at[kk], sem.at[0]).wait()
    g = gate_ref[...]
    y_lo = jnp.zeros((tc, ybuf.shape[2]), F32)
    y_hi = jnp.zeros((tc, ybuf.shape[2]), F32)
    for kk in range(TOP_K):
        lo, hi = _unpack_bf16_pair(ybuf[kk])
        y_lo = y_lo + g[:, kk:kk + 1] * lo
        y_hi = y_hi + g[:, kk:kk + 1] * hi
    y = jnp.concatenate([y_lo, y_hi], axis=1)
    o_ref[...] = _layer_norm(alpha_res * x1_ref[...] + gm_ref[0] * y, lg_ref[...], lb_ref[...])


def _combine(dest_flat, y_sorted, gate, x1, mod3, ln_g, ln_b, s, alpha_res):
    n, d = x1.shape
    tc = _tile(s, 128, 8)
    per_b = s // tc
    row = lambda i, dr: (i, 0)
    const = lambda i, dr: (0, 0)
    grid_spec = pltpu.PrefetchScalarGridSpec(
        num_scalar_prefetch=1, grid=(n // tc,),
        in_specs=[pl.BlockSpec(memory_space=pl.ANY),
                  pl.BlockSpec((tc, LANES), row), pl.BlockSpec((tc, d), row),
                  pl.BlockSpec((1, 1, d), lambda i, dr: (i // per_b, 0, 5)),
                  pl.BlockSpec((1, d), const), pl.BlockSpec((1, d), const)],
        out_specs=pl.BlockSpec((tc, d), row),
        scratch_shapes=[pltpu.VMEM((TOP_K, tc, d // 2), I32), pltpu.SemaphoreType.DMA((1,))])
    return pl.pallas_call(
        functools.partial(_combine_body, alpha_res=alpha_res),
        grid_spec=grid_spec,
        out_shape=jax.ShapeDtypeStruct((n, d), F32),
        compiler_params=_params(("arbitrary",)),
        name="moe_combine",
    )(dest_flat, y_sorted, gate, x1, mod3, ln_g[None], ln_b[None])


MOE_SUBTILES_PER_TILE = 10


def _moe_tiles(n_tok, n_exp):
    mean_load = n_tok * TOP_K // n_exp
    ts = 16
    while ts * 2 <= min(128, mean_load // 8):
        ts *= 2
    return ts, MOE_SUBTILES_PER_TILE * ts


def _layer(x, c_pad, w_ada, b_ada, w_in, b_forget, q_norm_g, kv_norm_g, kidx_ln_g, kidx_ln_b,
           w_uq, w_uk, w_uv, w_iq, fox_out_g, dsa_out_g, w_o, ln1_g, ln1_b,
           w_router, b_router, w1, b1, w2, b2, ln2_g, ln2_b, rel_bias, alpha_res):
    b, s, d = x.shape
    n = b * s
    fh = b_forget.shape[0]
    ql = q_norm_g.shape[0]
    kvl = kv_norm_g.shape[0]
    dh = w_uk.shape[1]
    ih = w_iq.shape[1] // IDX_DIM
    ne = w_router.shape[1]
    fw = fh * HEAD_DIM
    dw = dh * HEAD_DIM
    assert fw == dw, "head groups share the attention kernel's column-block width"
    assert fh + ih <= LANES and ne <= LANES

    mod = _matmul(c_pad, w_ada, 6 * d, F32, bias=b_ada[None], a_silu=True,
                  tm=16, tn=512, tk=d, name="adaln")
    mod3 = mod[:b].reshape(b, 1, 6 * d)
    u = _modulate(x, mod3, 0, 1).reshape(n, d)

    w_in_t = jnp.swapaxes(w_in, 0, 1)
    q_scale = HEAD_DIM ** -0.5 * LOG2E
    qk = _matmul(u, w_in_t, 2 * fw, BF16, w_t=True, scale_cols=(fw, q_scale), tm=2048,
                 name="proj_qk")
    tail_w = -(-(ql + kvl + IDX_DIM + fh + ih) // LANES) * LANES
    tail = _matmul(u, w_in_t, tail_w, F32, w_t=True, col0=3 * fw, tm=2048, name="proj_tail")
    cqn, ckvn, kin, small, cum = _tail_prep(tail, b, s, ql, kvl, fh, ih, q_norm_g, kv_norm_g,
                                            kidx_ln_g, kidx_ln_b, b_forget)
    v_t = _matmul(w_in_t, u, n, BF16, w_t=True, a_rows=(2 * fw, fw), tm=2048, name="proj_vt")

    t = 256 if s % 256 == 0 else 128
    fox = _attention("fox", b, s, fh, t, (qk, 0), (qk, 1), v_t, fox_out_g, cum=cum)

    q_d = _matmul(cqn, w_uq, dw, BF16, scale_cols=(dw, q_scale), tm=2048, name="proj_qd")
    q_i = _matmul(cqn, w_iq, ih * IDX_DIM, BF16, tm=2048, name="proj_qi")
    k_d = _matmul(ckvn, w_uk.reshape(kvl, dw), dw, BF16, tm=2048, name="expand_k")
    vd_t = _matmul(jnp.swapaxes(w_uv.reshape(kvl, dw), 0, 1), ckvn, n, BF16, w_t=True, tm=2048,
                   name="expand_vt")
    k_sel = min(TOPK_MAX, s // 4)
    mask = _indexer(q_i, kin, small, b, s, ih, fh, t, k_sel)
    bias_t = _bias_tiles(rel_bias, dh, t)
    dsa = _attention("dsa", b, s, dh, t, (q_d, 0), (k_d, 0), vd_t, dsa_out_g,
                     mask=mask, bias_tiles=bias_t)

    mix = _matmul(fox, w_o, d, F32, a2=dsa, tm=2048, name="proj_out")

    wr_pad = jnp.zeros((d, LANES), F32).at[:, :ne].set(w_router)
    br_pad = jnp.zeros((1, LANES), F32).at[0, :ne].set(b_router)
    x1, u_packed, eidx, rank, gate, cnt = _router(x.reshape(n, d), mix, mod3, ln1_g, ln1_b,
                                                  wr_pad, br_pad, s, ne, alpha_res)
    ts, tm = _moe_tiles(n, ne)
    n_tiles = -(-(n * TOP_K) // tm) + ne
    counts = cnt[0, :ne]
    ntile = (counts + tm - 1) // tm
    tend = jnp.cumsum(ntile)
    tstart = tend - ntile
    n_used = tend[-1]
    dest = (tstart * tm)[eidx[:, :TOP_K]] + rank[:, :TOP_K]
    dest_flat = dest.reshape(n * TOP_K).astype(I32)
    tid = jnp.minimum(jnp.arange(n_tiles, dtype=I32), n_used - 1)
    tile_e = jnp.minimum(jnp.searchsorted(tend, tid, side="right"), ne - 1).astype(I32)
    valid = jnp.clip(counts[tile_e] - (tid - tstart[tile_e]) * tm, 0, tm)
    valid = jnp.where(jnp.arange(n_tiles) < n_used, valid, 0)
    tile_nsub = ((valid + ts - 1) // ts).astype(I32)
    xs = _dispatch(dest_flat, u_packed, n_tiles * tm)
    y_sorted = _expert_ffn(xs, w1, b1, w2, b2, tile_e, tile_nsub,
                           n_used.reshape(1).astype(I32), tm=tm, ts=ts, n_tiles=n_tiles)
    out = _combine(dest_flat, y_sorted, gate, x1, mod3, ln2_g, ln2_b, s, alpha_res)
    return out.reshape(b, s, d)


def kernel(x, c, w_ada, b_ada, w_in, b_forget, q_norm_g, kv_norm_g, kidx_ln_g, kidx_ln_b, w_uq, w_uk, w_uv, w_iq, fox_out_g, dsa_out_g, w_o, ln1_g, ln1_b, w_router, b_router, w1, b1, w2, b2, ln2_g, ln2_b, rel_bias):
    depth = w_ada.shape[0]
    alpha_res = (2 * depth) ** 0.25
    b, d = c.shape
    c_pad = jnp.zeros((16, d), F32).at[:b].set(c)
    for l in range(depth):
        x = _layer(x, c_pad, w_ada[l], b_ada[l], w_in[l], b_forget[l], q_norm_g[l], kv_norm_g[l],
                   kidx_ln_g[l], kidx_ln_b[l], w_uq[l], w_uk[l], w_uv[l], w_iq[l], fox_out_g[l],
                   dsa_out_g[l], w_o[l], ln1_g[l], ln1_b[l], w_router[l], b_router[l],
                   w1[l], b1[l], w2[l], b2[l], ln2_g[l], ln2_b[l], rel_bias, alpha_res)
    return x
```

```python
import functools
import math

import numpy as np
import jax
import jax.numpy as jnp
from jax import lax
from jax.experimental import pallas as pl
from jax.experimental.pallas import tpu as pltpu

F32 = jnp.float32
BF16 = jnp.bfloat16
I32 = jnp.int32

LANES = 128
HEAD_DIM = 128
IDX_DIM = 128
TOPK_MAX = 256
N_BUCKETS = 32
MAX_DISTANCE = 128
TOP_K = 4
SWIGLU_LIMIT = 7.0
SWIGLU_ALPHA = 1.702
NEG_BIG = -1e30
LOG2E = math.log2(math.e)
VMEM_LIMIT = 56 * 1024 * 1024
INT_MIN = -2 ** 31
DMA_ISSUE_UNROLL = 8


def _tile(n, pref, unit=LANES):
    if n <= pref:
        return n
    t = (pref // unit) * unit
    while t > unit and n % t:
        t -= unit
    assert n % t == 0, (n, pref)
    return t


def _pack_bf16_pair(lo, hi):
    lo_bits = lax.bitcast_convert_type(lo.astype(BF16).astype(F32), I32)
    hi_bits = lax.bitcast_convert_type(hi.astype(BF16).astype(F32), I32)
    return lax.shift_right_logical(lo_bits, 16) | (hi_bits & jnp.int32(-65536))


def _unpack_bf16_pair(words):
    lo = lax.bitcast_convert_type(lax.shift_left(words, 16), F32)
    hi = lax.bitcast_convert_type(words & jnp.int32(-65536), F32)
    return lo, hi


def _params(sem, vmem=VMEM_LIMIT):
    return pltpu.CompilerParams(dimension_semantics=sem, vmem_limit_bytes=vmem)


def _mm_body(*refs, nk, nk1, a_silu, has_bias, col_block0, w_cols_if_overhang, w_t, out_scale):
    refs = list(refs)
    a_ref = refs.pop(0)
    a2_ref = refs.pop(0) if nk1 < nk else None
    w_ref = refs.pop(0)
    b_ref = refs.pop(0) if has_bias else None
    o_ref, acc_ref = refs
    k = pl.program_id(2)

    @pl.when(k == 0)
    def _():
        acc_ref[...] = jnp.zeros_like(acc_ref)

    def accumulate(src_ref):
        a = src_ref[...]
        if a_silu:
            a = a.astype(F32)
            a = a * jax.nn.sigmoid(a)
        w = w_ref[...]
        n_axis = 0 if w_t else 1
        if w_cols_if_overhang is not None:
            tn = w.shape[n_axis]
            col = lax.broadcasted_iota(I32, w.shape, n_axis) + (pl.program_id(1) + col_block0) * tn
            w = jnp.where(col < w_cols_if_overhang, w, 0.0)
        acc_ref[...] += lax.dot_general(a.astype(BF16), w.astype(BF16),
                                        (((1,), (1 if w_t else 0,)), ((), ())),
                                        preferred_element_type=F32)

    if a2_ref is None:
        accumulate(a_ref)
    else:
        pl.when(k < nk1)(lambda: accumulate(a_ref))
        pl.when(k >= nk1)(lambda: accumulate(a2_ref))

    @pl.when(k == nk - 1)
    def _():
        r = acc_ref[...]
        if has_bias:
            r = r + b_ref[...]
        if out_scale is not None:
            n_blocks, factor = out_scale
            r = r * jnp.where(pl.program_id(1) < n_blocks, factor, 1.0)
        o_ref[...] = r.astype(o_ref.dtype)


def _matmul(a, w, n_out, out_dtype, *, a2=None, bias=None, a_silu=False, col0=0, w_t=False,
            a_rows=None, scale_cols=None, tm=1024, tn=1024, tk=1024, name="matmul"):
    m, kdim = a.shape
    row0 = 0
    if a_rows is not None:
        row0, m = a_rows
    w_cols = w.shape[0] if w_t else w.shape[1]
    tm = _tile(math.gcd(m, row0), tm, 16)
    row_block0 = row0 // tm
    tn = _tile(math.gcd(n_out, col0, scale_cols[0] if scale_cols else 0), tn)
    tk = _tile(kdim, tk)
    nk1 = kdim // tk
    nk = nk1 if a2 is None else nk1 + a2.shape[1] // tk
    col_block0 = col0 // tn
    overhang = col0 + n_out > w_cols
    out_scale = None
    if scale_cols is not None:
        assert scale_cols[0] % tn == 0
        out_scale = (scale_cols[0] // tn, scale_cols[1])
    in_specs = [pl.BlockSpec((tm, tk), lambda i, j, k: (i + row_block0, jnp.minimum(k, nk1 - 1)))]
    args = [a]
    if a2 is not None:
        assert a2.shape[1] % tk == 0
        in_specs.append(pl.BlockSpec((tm, tk), lambda i, j, k: (i, jnp.maximum(k - nk1, 0))))
        args.append(a2)
    if w_t:
        in_specs.append(pl.BlockSpec((tn, tk), lambda i, j, k: (j + col_block0, k)))
    else:
        in_specs.append(pl.BlockSpec((tk, tn), lambda i, j, k: (k, j + col_block0)))
    args.append(w)
    if bias is not None:
        in_specs.append(pl.BlockSpec((1, tn), lambda i, j, k: (0, j)))
        args.append(bias)
    return pl.pallas_call(
        functools.partial(_mm_body, nk=nk, nk1=nk1, a_silu=a_silu, has_bias=bias is not None,
                          col_block0=col_block0, w_t=w_t, out_scale=out_scale,
                          w_cols_if_overhang=w_cols if overhang else None),
        grid=(m // tm, n_out // tn, nk),
        in_specs=in_specs,
        out_specs=pl.BlockSpec((tm, tn), lambda i, j, k: (i, j)),
        out_shape=jax.ShapeDtypeStruct((m, n_out), out_dtype),
        scratch_shapes=[pltpu.VMEM((tm, tn), F32)],
        compiler_params=_params(("parallel", "parallel", "arbitrary")),
        name=name,
    )(*args)


def _modulate_body(x_ref, sh_ref, sc_ref, o_ref):
    o_ref[0] = (x_ref[0] * (1.0 + sc_ref[0]) + sh_ref[0]).astype(o_ref.dtype)


def _modulate(x, mod3, shift_chunk, scale_chunk):
    b, s, d = x.shape
    ts = _tile(s, 512, 16)
    return pl.pallas_call(
        _modulate_body,
        grid=(b, s // ts),
        in_specs=[pl.BlockSpec((1, ts, d), lambda i, j: (i, j, 0)),
                  pl.BlockSpec((1, 1, d), lambda i, j: (i, 0, shift_chunk)),
                  pl.BlockSpec((1, 1, d), lambda i, j: (i, 0, scale_chunk))],
        out_specs=pl.BlockSpec((1, ts, d), lambda i, j: (i, j, 0)),
        out_shape=jax.ShapeDtypeStruct((b, s, d), BF16),
        compiler_params=_params(("parallel", "parallel")),
        name="modulate",
    )(x, mod3, mod3)


def _tail_body(t_ref, qg_ref, kvg_ref, lng_ref, lnb_ref, bf_ref,
               cq_ref, ckv_ref, kin_ref, small_ref, cum_ref, carry_ref, *, fh, ql, kvl, w_scale):
    j = pl.program_id(1)
    ts = t_ref.shape[0]

    @pl.when(j == 0)
    def _():
        carry_ref[...] = jnp.zeros_like(carry_ref)

    cq = t_ref[:, fh:fh + ql]
    cq_ref[...] = (cq * lax.rsqrt(jnp.mean(cq * cq, axis=-1, keepdims=True) + 1e-6)
                   * qg_ref[...]).astype(cq_ref.dtype)
    ckv = t_ref[:, fh + ql:fh + ql + kvl]
    ckv_ref[...] = (ckv * lax.rsqrt(jnp.mean(ckv * ckv, axis=-1, keepdims=True) + 1e-6)
                    * kvg_ref[...]).astype(ckv_ref.dtype)
    ki = t_ref[:, fh + ql + kvl:fh + ql + kvl + IDX_DIM]
    mu = jnp.mean(ki, axis=-1, keepdims=True)
    kc = ki - mu
    var = jnp.mean(kc * kc, axis=-1, keepdims=True)
    kin_ref[...] = (kc * lax.rsqrt(var + 1e-5) * lng_ref[...] + lnb_ref[...]).astype(kin_ref.dtype)

    small_ref[...] = t_ref[:, ql + kvl + IDX_DIM:ql + kvl + IDX_DIM + LANES] * w_scale
    z = t_ref[:, 0:LANES] + bf_ref[...]
    log_f = jnp.minimum(z, 0.0) - jnp.log(1.0 + jnp.exp(-jnp.abs(z)))
    row = lax.broadcasted_iota(I32, (ts, ts), 0)
    col = lax.broadcasted_iota(I32, (ts, ts), 1)
    tri = jnp.where(col <= row, 1.0, 0.0).astype(F32)
    incl = jnp.dot(tri, log_f, preferred_element_type=F32,
                   precision=lax.Precision.HIGHEST) + carry_ref[0:1, :]
    carry_ref[0:1, :] = incl[ts - 1:ts, :]
    cum_ref[...] = incl * (-LOG2E)


def _tail_prep(tail, b, s, ql, kvl, fh, ih, q_norm_g, kv_norm_g, ln_g, ln_b, b_forget):
    n = tail.shape[0]
    assert ql % LANES == 0 and kvl % LANES == 0
    ts = _tile(s, 256)
    nj = s // ts
    bf_pad = jnp.zeros((1, LANES), F32).at[0, :fh].set(b_forget)
    w_scale = (ih ** -0.5) * (IDX_DIM ** -0.5)
    row = lambda i, j: (i * nj + j, 0)
    const = lambda i, j: (0, 0)
    return pl.pallas_call(
        functools.partial(_tail_body, fh=fh, ql=ql, kvl=kvl, w_scale=w_scale),
        grid=(b, nj),
        in_specs=[pl.BlockSpec((ts, tail.shape[1]), row),
                  pl.BlockSpec((1, ql), const), pl.BlockSpec((1, kvl), const),
                  pl.BlockSpec((1, IDX_DIM), const), pl.BlockSpec((1, IDX_DIM), const),
                  pl.BlockSpec((1, LANES), const)],
        out_specs=[pl.BlockSpec((ts, ql), row), pl.BlockSpec((ts, kvl), row),
                   pl.BlockSpec((ts, IDX_DIM), row), pl.BlockSpec((ts, LANES), row),
                   pl.BlockSpec((ts, LANES), row)],
        out_shape=[jax.ShapeDtypeStruct((n, ql), BF16), jax.ShapeDtypeStruct((n, kvl), BF16),
                   jax.ShapeDtypeStruct((n, IDX_DIM), BF16), jax.ShapeDtypeStruct((n, LANES), F32),
                   jax.ShapeDtypeStruct((n, LANES), F32)],
        scratch_shapes=[pltpu.VMEM((8, LANES), F32)],
        compiler_params=_params(("parallel", "arbitrary")),
        name="tail_prep",
    )(tail, q_norm_g[None], kv_norm_g[None], ln_g[None], ln_b[None], bf_pad)


_MAX_EXACT = N_BUCKETS // 2
_BUCKET_THRESHOLDS = tuple(
    int(math.ceil(_MAX_EXACT * (MAX_DISTANCE / _MAX_EXACT) ** (k / (N_BUCKETS - _MAX_EXACT)) - 1e-9))
    for k in range(1, N_BUCKETS - _MAX_EXACT))


def _bias_body(rb_ref, o_ref, *, t):
    which = pl.program_id(0)
    h = pl.program_id(1)
    row = lax.broadcasted_iota(I32, (t, t), 0)
    col = lax.broadcasted_iota(I32, (t, t), 1)
    d = jnp.maximum(which * t + col - row, 0)
    large = jnp.full((t, t), _MAX_EXACT, I32)
    for thr in _BUCKET_THRESHOLDS:
        large = large + jnp.where(d >= thr, 1, 0)
    bucket = jnp.where(d < _MAX_EXACT, d, large)
    bias = jnp.zeros((t, t), F32)
    for bkt in range(N_BUCKETS):
        bias = jnp.where(bucket == bkt, rb_ref[bkt, h], bias)
    o_ref[0, 0] = (bias - rb_ref[N_BUCKETS - 1, h]) * LOG2E


def _bias_tiles(rel_bias, nh, t):
    assert 2 * t - (t - 1) >= _BUCKET_THRESHOLDS[-1], "key tiles two or more away are all last-bucket"
    return pl.pallas_call(
        functools.partial(_bias_body, t=t),
        grid=(2, nh),
        in_specs=[pl.BlockSpec(memory_space=pltpu.SMEM)],
        out_specs=pl.BlockSpec((1, 1, t, t), lambda w, h: (w, h, 0, 0)),
        out_shape=jax.ShapeDtypeStruct((2, nh, t, t), F32),
        compiler_params=_params(("parallel", "parallel")),
        name="bias_tiles",
    )(rel_bias)


ATTN_QCOLS = 128


def _attn_body(qi_ref, kj_ref, *refs, nh, t, mode):
    if mode == "fox":
        q_ref, k_ref, vt_ref, g_ref, cum_ref, o_ref, m_ref, l_ref, acc_ref, neg_ref = refs
    else:
        q_ref, k_ref, vt_ref, g_ref, mask_ref, bt_ref, o_ref, m_ref, l_ref, acc_ref, neg_ref = refs
    p = pl.program_id(1)
    qi = qi_ref[p]
    kj = kj_ref[p]
    nblk = t // ATTN_QCOLS

    @pl.when(kj == 0)
    def _():
        m_ref[...] = jnp.full_like(m_ref, NEG_BIG)
        l_ref[...] = jnp.zeros_like(l_ref)
        acc_ref[...] = jnp.zeros_like(acc_ref)

    def heads(addend):
        for h in range(nh):
            sl = slice(h * HEAD_DIM, (h + 1) * HEAD_DIM)
            for r in range(nblk):
                qs = slice(r * ATTN_QCOLS, (r + 1) * ATTN_QCOLS)
                idx = h * nblk + r
                s = lax.dot_general(k_ref[:, sl], q_ref[qs, sl], (((1,), (1,)), ((), ())),
                                    preferred_element_type=F32)
                s = s + addend(h, qs)
                m_prev = m_ref[idx]
                m_next = jnp.maximum(m_prev, jnp.max(s, axis=0, keepdims=True))
                alpha = jnp.exp2(m_prev - m_next)
                pr = jnp.exp2(s - m_next)
                l_ref[idx] = alpha * l_ref[idx] + jnp.sum(pr, axis=0, keepdims=True)
                acc_ref[idx] = acc_ref[idx] * alpha + jnp.dot(
                    vt_ref[sl, :], pr.astype(BF16), preferred_element_type=F32)
                m_ref[idx] = m_next

    key_i = lax.broadcasted_iota(I32, (t, t), 0)
    qry_i = lax.broadcasted_iota(I32, (t, t), 1)
    if mode == "fox":
        @pl.when(kj == qi)
        def _():
            neg_ref[...] = jnp.where(key_i <= qry_i, 0.0, NEG_BIG).astype(F32)
            heads(lambda h, qs: neg_ref[:, qs] + cum_ref[:, h:h + 1])

        @pl.when(kj != qi)
        def _():
            heads(lambda h, qs: cum_ref[:, h:h + 1])
    else:
        neg_ref[...] = ((mask_ref[0, 0].astype(F32) - 1.0) * (-NEG_BIG)).T

        @pl.when(qi - kj >= 2)
        def _():
            heads(lambda h, qs: neg_ref[:, qs])

        @pl.when(qi - kj < 2)
        def _():
            near = qi - kj
            heads(lambda h, qs: neg_ref[:, qs] + bt_ref[near, h, :, qs])

    @pl.when(kj == qi)
    def _():
        for r in range(nblk):
            qs = slice(r * ATTN_QCOLS, (r + 1) * ATTN_QCOLS)
            ss = jnp.zeros((1, ATTN_QCOLS), F32)
            for h in range(nh):
                o = acc_ref[h * nblk + r] / l_ref[h * nblk + r]
                acc_ref[h * nblk + r] = o
                ss = ss + jnp.sum(o * o, axis=0, keepdims=True)
            rinv = lax.rsqrt(ss / (nh * HEAD_DIM) + 1e-6)
            for h in range(nh):
                sl = slice(h * HEAD_DIM, (h + 1) * HEAD_DIM)
                o_ref[qs, sl] = ((acc_ref[h * nblk + r] * rinv).T * g_ref[:, sl]).astype(o_ref.dtype)


def _attention(mode, b, s, nh, t, q_src, k_src, vt, gain, *, cum=None, mask=None, bias_tiles=None):
    nq = s // t
    pairs = [(i, j) for i in range(nq) for j in range(i + 1)]
    qi_arr = jnp.asarray(np.array([pq for pq, _ in pairs], np.int32))
    kj_arr = jnp.asarray(np.array([pk for _, pk in pairs], np.int32))
    hw = nh * HEAD_DIM
    n = b * s
    nblk = t // ATTN_QCOLS

    def qmap(cb):
        return lambda bi, p, qi, kj: (bi * nq + qi[p], cb)

    def kmap(cb):
        return lambda bi, p, qi, kj: (bi * nq + kj[p], cb)

    in_specs = [pl.BlockSpec((t, hw), qmap(q_src[1])),
                pl.BlockSpec((t, hw), kmap(k_src[1])),
                pl.BlockSpec((hw, t), lambda bi, p, qi, kj: (0, bi * nq + kj[p])),
                pl.BlockSpec((1, hw), lambda bi, p, qi, kj: (0, 0))]
    args = [q_src[0], k_src[0], vt, gain[None]]
    if mode == "fox":
        in_specs.append(pl.BlockSpec((t, LANES), kmap(0)))
        args.append(cum)
    else:
        in_specs += [pl.BlockSpec((1, 1, t, t), lambda bi, p, qi, kj: (bi, kj[p], qi[p], 0)),
                     pl.BlockSpec(memory_space=pltpu.VMEM)]
        args += [mask, bias_tiles]
    grid_spec = pltpu.PrefetchScalarGridSpec(
        num_scalar_prefetch=2,
        grid=(b, len(pairs)),
        in_specs=in_specs,
        out_specs=pl.BlockSpec((t, hw), lambda bi, p, qi, kj: (bi * nq + qi[p], 0)),
        scratch_shapes=[pltpu.VMEM((nh * nblk, 1, ATTN_QCOLS), F32),
                        pltpu.VMEM((nh * nblk, 1, ATTN_QCOLS), F32),
                        pltpu.VMEM((nh * nblk, HEAD_DIM, ATTN_QCOLS), F32),
                        pltpu.VMEM((t, t), F32)],
    )
    return pl.pallas_call(
        functools.partial(_attn_body, nh=nh, t=t, mode=mode),
        grid_spec=grid_spec,
        out_shape=jax.ShapeDtypeStruct((n, hw), BF16),
        compiler_params=_params(("parallel", "arbitrary")),
        name=mode + "_attention",
    )(qi_arr, kj_arr, *args)


IDX_ROWS = 128


def _float_key(x):
    bits = lax.bitcast_convert_type(x, I32)
    return bits ^ (lax.shift_right_arithmetic(bits, 31) & 0x7FFFFFFF)


def _indexer_body(q_ref, k_ref, w_ref, o_ref, key_ref, *, ih, w_lane0, tq, ck, nchunk, k_sel):
    i = pl.program_id(1)
    n_valid = ((i + 1) * tq + ck - 1) // ck
    row = lax.broadcasted_iota(I32, (tq, ck), 0) + i * tq
    lane_col = lax.broadcasted_iota(I32, (tq, ck), 1)

    def score_chunk(c, carry):
        kc = k_ref[0, pl.ds(pl.multiple_of(c * ck, ck), ck), :]
        for r in range(tq // IDX_ROWS):
            rows = slice(r * IDX_ROWS, (r + 1) * IDX_ROWS)
            w_blk = w_ref[rows, :]
            acc = jnp.zeros((IDX_ROWS, ck), F32)
            for h in range(ih):
                sh = lax.dot_general(q_ref[rows, h * IDX_DIM:(h + 1) * IDX_DIM], kc,
                                     (((1,), (1,)), ((), ())), preferred_element_type=F32)
                acc = acc + w_blk[:, w_lane0 + h:w_lane0 + h + 1] * jnp.maximum(sh, 0.0)
            causal = (lax.broadcasted_iota(I32, (IDX_ROWS, ck), 1) + c * ck
                      <= lax.broadcasted_iota(I32, (IDX_ROWS, ck), 0) + (i * tq + r * IDX_ROWS))
            acc = jnp.where(causal, acc, -jnp.inf)
            key_ref[c, rows, :] = _float_key(acc)
        return carry

    lax.fori_loop(0, n_valid, score_chunk, 0)

    def count(pred_fn, level):
        def body(c, acc):
            ind = jnp.where(pred_fn(key_ref[c], level), 1, 0)
            for part in range(ck // LANES):
                acc = acc + ind[:, part * LANES:(part + 1) * LANES]
            return acc
        acc = lax.fori_loop(0, n_valid, body, jnp.zeros((tq, LANES), I32))
        return jnp.sum(acc, axis=1, keepdims=True)

    ge = lambda kv, lv: kv >= lv
    t0 = jnp.full((tq, 1), INT_MIN, I32)
    zero = jnp.zeros((tq, 1), I32)
    thr = jnp.where(count(ge, zero) >= k_sel, zero, t0)

    def bit_step(bi, thr):
        cand = thr | lax.shift_left(jnp.int32(1), 30 - bi)
        return jnp.where(count(ge, cand) >= k_sel, cand, thr)

    thr = lax.fori_loop(0, 31, bit_step, thr)
    need = (k_sel - count(lambda kv, lv: kv > lv, thr)).astype(F32)

    urow = lax.broadcasted_iota(I32, (ck, ck), 0)
    ucol = lax.broadcasted_iota(I32, (ck, ck), 1)
    upper = jnp.where(urow <= ucol, 1.0, 0.0).astype(BF16)
    ones = jnp.ones((ck, LANES), BF16)

    def select_chunk(c, offset):
        kv = key_ref[c]
        eq = kv == thr
        eqb = jnp.where(eq, 1.0, 0.0).astype(BF16)
        rank = jnp.dot(eqb, upper, preferred_element_type=F32) + offset[:, 0:1]
        take = (kv > thr) | (eq & (rank <= need))
        take = take & (lane_col + c * ck <= row)
        o_ref[0, c] = jnp.where(take, 1, 0).astype(jnp.int8)
        return offset + jnp.dot(eqb, ones, preferred_element_type=F32)

    lax.fori_loop(0, n_valid, select_chunk, jnp.zeros((tq, LANES), F32))

    def zero_chunk(c, carry):
        o_ref[0, c] = jnp.zeros((tq, ck), jnp.int8)
        return carry

    lax.fori_loop(n_valid, nchunk, zero_chunk, 0)


def _indexer(q_idx, k_in, small, b, s, ih, w_lane0, ck, k_sel):
    tq = _tile(s, 256, IDX_ROWS)
    nq = s // tq
    nchunk = s // ck
    k3 = k_in.reshape(b, s, IDX_DIM)
    return pl.pallas_call(
        functools.partial(_indexer_body, ih=ih, w_lane0=w_lane0, tq=tq, ck=ck, nchunk=nchunk,
                          k_sel=k_sel),
        grid=(b, nq),
        in_specs=[pl.BlockSpec((tq, ih * IDX_DIM), lambda bi, i: (bi * nq + i, 0)),
                  pl.BlockSpec((1, s, IDX_DIM), lambda bi, i: (bi, 0, 0)),
                  pl.BlockSpec((tq, LANES), lambda bi, i: (bi * nq + i, 0))],
        out_specs=pl.BlockSpec((1, nchunk, tq, ck), lambda bi, i: (bi, 0, i, 0)),
        out_shape=jax.ShapeDtypeStruct((b, nchunk, s, ck), jnp.int8),
        scratch_shapes=[pltpu.VMEM((nchunk, tq, ck), I32)],
        compiler_params=_params(("parallel", "parallel")),
        name="indexer_topk",
    )(q_idx, k3, small)


def _layer_norm(h, g, bvec):
    mu = jnp.mean(h, axis=-1, keepdims=True)
    hc = h - mu
    var = jnp.mean(hc * hc, axis=-1, keepdims=True)
    return hc * lax.rsqrt(var + 1e-5) * g + bvec


def _router_body(x_ref, mix_ref, ga_ref, shm_ref, scm_ref, lg_ref, lb_ref, wr_ref, br_ref,
                 x1_ref, up_ref, eidx_ref, rank_ref, gate_ref, cnt_ref, carry_ref,
                 *, alpha_res, ne, nsteps):
    i = pl.program_id(0)
    tm, d = x_ref.shape

    @pl.when(i == 0)
    def _():
        carry_ref[...] = jnp.zeros_like(carry_ref)

    x1 = _layer_norm(alpha_res * x_ref[...] + ga_ref[0] * mix_ref[...], lg_ref[...], lb_ref[...])
    x1_ref[...] = x1
    u = x1 * (1.0 + scm_ref[0]) + shm_ref[0]
    up_ref[...] = _pack_bf16_pair(u[:, :d // 2], u[:, d // 2:])

    logits = jnp.dot(u, wr_ref[...], preferred_element_type=F32,
                     precision=lax.Precision.HIGHEST) + br_ref[...]
    lane = lax.broadcasted_iota(I32, (tm, LANES), 1)
    lane_f = lane.astype(F32)
    work = jnp.where(lane < ne, logits, -jnp.inf)
    sel = jnp.zeros((tm, LANES), F32)
    idxs, vals, hits = [], [], []
    for _ in range(TOP_K):
        mx = jnp.max(work, axis=1, keepdims=True)
        ik = jnp.min(jnp.where(work == mx, lane_f, float(LANES)), axis=1, keepdims=True)
        hit = lane_f == ik
        sel = jnp.where(hit, 1.0, sel)
        work = jnp.where(hit, -jnp.inf, work)
        idxs.append(ik.astype(I32))
        vals.append(mx)
        hits.append(hit)
    exps = [jnp.exp(v - vals[0]) for v in vals]
    denom = exps[0] + exps[1] + exps[2] + exps[3]

    row = lax.broadcasted_iota(I32, (tm, tm), 0)
    col = lax.broadcasted_iota(I32, (tm, tm), 1)
    tri = jnp.where(col < row, 1.0, 0.0).astype(BF16)
    before = jnp.dot(tri, sel.astype(BF16), preferred_element_type=F32) + carry_ref[0:1, :]
    total = carry_ref[0:1, :] + jnp.sum(sel, axis=0, keepdims=True)
    carry_ref[0:1, :] = total

    eidx = jnp.zeros((tm, LANES), I32)
    rank = jnp.zeros((tm, LANES), I32)
    gate = jnp.zeros((tm, LANES), F32)
    for kk in range(TOP_K):
        rk = jnp.sum(jnp.where(hits[kk], before, 0.0), axis=1, keepdims=True)
        eidx = jnp.where(lane == kk, idxs[kk], eidx)
        rank = jnp.where(lane == kk, rk.astype(I32), rank)
        gate = jnp.where(lane == kk, exps[kk] / denom, gate)
    eidx_ref[...] = eidx
    rank_ref[...] = rank
    gate_ref[...] = gate
    cnt_ref[...] = jnp.broadcast_to(total, cnt_ref.shape).astype(I32)


def _router(x2d, mix, mod3, ln_g, ln_b, w_router_pad, b_router_pad, s, ne, alpha_res):
    n, d = x2d.shape
    tm = _tile(s, 256, 16)
    nsteps = n // tm
    per_b = s // tm
    row = lambda i: (i, 0)
    const = lambda i: (0, 0)
    modspec = lambda chunk: pl.BlockSpec((1, 1, d), lambda i: (i // per_b, 0, chunk))
    return pl.pallas_call(
        functools.partial(_router_body, alpha_res=alpha_res, ne=ne, nsteps=nsteps),
        grid=(nsteps,),
        in_specs=[pl.BlockSpec((tm, d), row), pl.BlockSpec((tm, d), row),
                  modspec(2), modspec(3), modspec(4),
                  pl.BlockSpec((1, d), const), pl.BlockSpec((1, d), const),
                  pl.BlockSpec((d, LANES), const), pl.BlockSpec((1, LANES), const)],
        out_specs=[pl.BlockSpec((tm, d), row), pl.BlockSpec((tm, d // 2), row),
                   pl.BlockSpec((tm, LANES), row), pl.BlockSpec((tm, LANES), row),
                   pl.BlockSpec((tm, LANES), row), pl.BlockSpec((8, LANES), const)],
        out_shape=[jax.ShapeDtypeStruct((n, d), F32), jax.ShapeDtypeStruct((n, d // 2), I32),
                   jax.ShapeDtypeStruct((n, LANES), I32), jax.ShapeDtypeStruct((n, LANES), I32),
                   jax.ShapeDtypeStruct((n, LANES), F32), jax.ShapeDtypeStruct((8, LANES), I32)],
        scratch_shapes=[pltpu.VMEM((8, LANES), F32)],
        compiler_params=_params(("arbitrary",)),
        name="ln1_router",
    )(x2d, mix, mod3, mod3, mod3, ln_g[None], ln_b[None], w_router_pad, b_router_pad)


def _dispatch_body(dest_ref, u_ref, xs_ref, sem):
    i = pl.program_id(0)
    tc = u_ref.shape[0]

    def start(j, cc):
        for kk in range(TOP_K):
            d = dest_ref[(i * tc + j) * TOP_K + kk]
            pltpu.make_async_copy(u_ref.at[pl.ds(j, 1)], xs_ref.at[pl.ds(d, 1)], sem.at[0]).start()
        return cc

    lax.fori_loop(0, tc, start, 0, unroll=DMA_ISSUE_UNROLL)
    for _ in range(TOP_K):
        pltpu.make_async_copy(u_ref, xs_ref.at[pl.ds(0, tc)], sem.at[0]).wait()


def _dispatch(dest_flat, u_packed, n_slots):
    n, dw = u_packed.shape
    tc = _tile(n, 256, 8)
    grid_spec = pltpu.PrefetchScalarGridSpec(
        num_scalar_prefetch=1, grid=(n // tc,),
        in_specs=[pl.BlockSpec((tc, dw), lambda i, dr: (i, 0))],
        out_specs=pl.BlockSpec(memory_space=pl.ANY),
        scratch_shapes=[pltpu.SemaphoreType.DMA((1,))])
    return pl.pallas_call(
        _dispatch_body,
        grid_spec=grid_spec,
        out_shape=jax.ShapeDtypeStruct((n_slots, dw), I32),
        compiler_params=pltpu.CompilerParams(dimension_semantics=("arbitrary",),
                                             has_side_effects=True),
        name="moe_dispatch",
    )(dest_flat, u_packed)


def _ffn_body(te_ref, nsub_ref, nused_ref, x_ref, w1g_ref, w1u_ref, b1g_ref, b1u_ref,
              w2lo_ref, w2hi_ref, b2lo_ref, b2hi_ref, o_ref, h_ref, xbuf, xsem, *, nf, tf, ts, nr):
    i = pl.program_id(0)
    st = pl.program_id(1)
    _, tm, dh = xbuf.shape
    nsub = nsub_ref[i]
    slot = i % 2

    def x_copy(tile, to_slot):
        return pltpu.make_async_copy(x_ref.at[pl.ds(pl.multiple_of(tile * tm, tm), tm)],
                                     xbuf.at[to_slot], xsem.at[to_slot])

    @pl.when(jnp.logical_and(i == 0, st == 0))
    def _():
        x_copy(0, 0).start()

    @pl.when(st == 0)
    def _():
        x_copy(i, slot).wait()

    @pl.when(jnp.logical_and(st == nf, i + 1 < nused_ref[0]))
    def _():
        x_copy(i + 1, 1 - slot).start()

    def up_step(rows):
        lo, hi = _unpack_bf16_pair(xbuf[slot, 0:rows, :])
        lo, hi = lo.astype(BF16), hi.astype(BF16)

        def proj(w_ref, b_ref):
            return (jnp.dot(lo, w_ref[0, 0:dh, :].astype(BF16), preferred_element_type=F32)
                    + jnp.dot(hi, w_ref[0, dh:2 * dh, :].astype(BF16), preferred_element_type=F32)
                    + b_ref[0])
        gate = jnp.minimum(proj(w1g_ref, b1g_ref), SWIGLU_LIMIT)
        up = jnp.clip(proj(w1u_ref, b1u_ref), -SWIGLU_LIMIT, SWIGLU_LIMIT)
        act = (up + 1.0) * gate * jax.nn.sigmoid(SWIGLU_ALPHA * gate)
        h_ref[st, 0:rows, :] = act.astype(BF16)

    def down_step(rows):
        def proj(w_ref, b_ref):
            acc = jnp.broadcast_to(b_ref[0], (rows, o_ref.shape[1])).astype(F32)
            for f in range(nf):
                acc = acc + jnp.dot(h_ref[f, 0:rows, :],
                                    w_ref[0, f * tf:(f + 1) * tf, :].astype(BF16),
                                    preferred_element_type=F32)
            return acc
        o_ref[0:rows, :] = _pack_bf16_pair(proj(w2lo_ref, b2lo_ref), proj(w2hi_ref, b2hi_ref))
        if rows < tm:
            o_ref[rows:tm, :] = jnp.zeros((tm - rows, o_ref.shape[1]), I32)

    for r in range(1, nr + 1):
        @pl.when(jnp.logical_and(nsub == r, st < nf))
        def _(r=r):
            up_step(r * ts)

        @pl.when(jnp.logical_and(nsub == r, st >= nf))
        def _(r=r):
            down_step(r * ts)


def _expert_ffn(xs, w1, b1, w2, b2, tile_e, tile_nsub, n_used, *, tm, ts, n_tiles):
    ne, d, de2 = w1.shape
    de = de2 // 2
    dh = d // 2
    tf = _tile(de, 256)
    td = _tile(dh, 256)
    nf, nd = de // tf, dh // td
    nstep = nf + nd
    b1r = b1.reshape(ne, 1, de2)
    b2r = b2.reshape(ne, 1, d)

    def tile_idx(i, nu):
        return jnp.maximum(jnp.minimum(i, nu[0] - 1), 0)

    def step_idx(i, s, nu):
        return jnp.where(i < nu[0], s, nstep - 1)

    def up_map(off):
        return lambda i, s, te, ns, nu: (te[tile_idx(i, nu)], 0,
                                         off + jnp.minimum(step_idx(i, s, nu), nf - 1))

    def down_map(off):
        return lambda i, s, te, ns, nu: (te[tile_idx(i, nu)], 0,
                                         off + jnp.maximum(step_idx(i, s, nu) - nf, 0))

    grid_spec = pltpu.PrefetchScalarGridSpec(
        num_scalar_prefetch=3,
        grid=(n_used[0], nstep),
        in_specs=[pl.BlockSpec(memory_space=pl.ANY),
                  pl.BlockSpec((1, d, tf), up_map(0)), pl.BlockSpec((1, d, tf), up_map(nf)),
                  pl.BlockSpec((1, 1, tf), up_map(0)), pl.BlockSpec((1, 1, tf), up_map(nf)),
                  pl.BlockSpec((1, de, td), down_map(0)), pl.BlockSpec((1, de, td), down_map(nd)),
                  pl.BlockSpec((1, 1, td), down_map(0)), pl.BlockSpec((1, 1, td), down_map(nd))],
        out_specs=pl.BlockSpec((tm, td), lambda i, s, te, ns, nu: (
            tile_idx(i, nu), jnp.maximum(step_idx(i, s, nu) - nf, 0))),
        scratch_shapes=[pltpu.VMEM((nf, tm, tf), BF16), pltpu.VMEM((2, tm, dh), I32),
                        pltpu.SemaphoreType.DMA((2,))],
    )
    return pl.pallas_call(
        functools.partial(_ffn_body, nf=nf, tf=tf, ts=ts, nr=tm // ts),
        grid_spec=grid_spec,
        out_shape=jax.ShapeDtypeStruct((n_tiles * tm, dh), I32),
        compiler_params=_params(("arbitrary", "arbitrary")),
        name="expert_ffn",
    )(tile_e, tile_nsub, n_used, xs, w1, w1, b1r, b1r, w2, w2, b2r, b2r)


def _combine_body(dest_ref, y_ref, gate_ref, x1_ref, gm_ref, lg_ref, lb_ref, o_ref, ybuf, sem,
                  *, alpha_res):
    i = pl.program_id(0)
    nsteps = pl.num_programs(0)
    tc = x1_ref.shape[0]

    def gather_tile(tile, slot):
        def start(j, cc):
            for kk in range(TOP_K):
                d = dest_ref[(tile * tc + j) * TOP_K + kk]
                pltpu.make_async_copy(y_ref.at[pl.ds(d, 1)], ybuf.at[slot, kk, pl.ds(j, 1)],
                                      sem.at[slot]).start()
            return cc
        lax.fori_loop(0, tc, start, 0, unroll=DMA_ISSUE_UNROLL)

    slot = i % 2

    @pl.when(i == 0)
    def _():
        gather_tile(0, 0)

    @pl.when(i + 1 < nsteps)
    def _():
        gather_tile(i + 1, 1 - slot)

    for kk in range(TOP_K):
        pltpu.make_async_copy(y_ref.at[pl.ds(0, tc)], ybuf.at[slot, kk], sem.at[slot]).wait()
    g = gate_ref[...]
    y_lo = jnp.zeros((tc, ybuf.shape[3]), F32)
    y_hi = jnp.zeros((tc, ybuf.shape[3]), F32)
    for kk in range(TOP_K):
        lo, hi = _unpack_bf16_pair(ybuf[slot, kk])
        y_lo = y_lo + g[:, kk:kk + 1] * lo
        y_hi = y_hi + g[:, kk:kk + 1] * hi
    y = jnp.concatenate([y_lo, y_hi], axis=1)
    o_ref[...] = _layer_norm(alpha_res * x1_ref[...] + gm_ref[0] * y, lg_ref[...], lb_ref[...])


def _combine(dest_flat, y_sorted, gate, x1, mod3, ln_g, ln_b, s, alpha_res):
    n, d = x1.shape
    tc = _tile(s, 128, 8)
    per_b = s // tc
    row = lambda i, dr: (i, 0)
    const = lambda i, dr: (0, 0)
    grid_spec = pltpu.PrefetchScalarGridSpec(
        num_scalar_prefetch=1, grid=(n // tc,),
        in_specs=[pl.BlockSpec(memory_space=pl.ANY),
                  pl.BlockSpec((tc, LANES), row), pl.BlockSpec((tc, d), row),
                  pl.BlockSpec((1, 1, d), lambda i, dr: (i // per_b, 0, 5)),
                  pl.BlockSpec((1, d), const), pl.BlockSpec((1, d), const)],
        out_specs=pl.BlockSpec((tc, d), row),
        scratch_shapes=[pltpu.VMEM((2, TOP_K, tc, d // 2), I32), pltpu.SemaphoreType.DMA((2,))])
    return pl.pallas_call(
        functools.partial(_combine_body, alpha_res=alpha_res),
        grid_spec=grid_spec,
        out_shape=jax.ShapeDtypeStruct((n, d), F32),
        compiler_params=_params(("arbitrary",)),
        name="moe_combine",
    )(dest_flat, y_sorted, gate, x1, mod3, ln_g[None], ln_b[None])


MOE_SUBTILES_PER_TILE = 5


def _moe_tiles(n_tok, n_exp):
    mean_load = n_tok * TOP_K // n_exp
    ts = 16
    while ts * 2 <= min(256, mean_load // 4):
        ts *= 2
    return ts, MOE_SUBTILES_PER_TILE * ts


def _layer(x, c_pad, w_ada, b_ada, w_in, b_forget, q_norm_g, kv_norm_g, kidx_ln_g, kidx_ln_b,
           w_uq, w_uk, w_uv, w_iq, fox_out_g, dsa_out_g, w_o, ln1_g, ln1_b,
           w_router, b_router, w1, b1, w2, b2, ln2_g, ln2_b, rel_bias, alpha_res):
    b, s, d = x.shape
    n = b * s
    fh = b_forget.shape[0]
    ql = q_norm_g.shape[0]
    kvl = kv_norm_g.shape[0]
    dh = w_uk.shape[1]
    ih = w_iq.shape[1] // IDX_DIM
    ne = w_router.shape[1]
    fw = fh * HEAD_DIM
    dw = dh * HEAD_DIM
    assert fw == dw, "head groups share the attention kernel's column-block width"
    assert fh + ih <= LANES and ne <= LANES

    mod = _matmul(c_pad, w_ada, 6 * d, F32, bias=b_ada[None], a_silu=True,
                  tm=16, tn=512, tk=d, name="adaln")
    mod3 = mod[:b].reshape(b, 1, 6 * d)
    u = _modulate(x, mod3, 0, 1).reshape(n, d)

    w_in_t = jnp.swapaxes(w_in, 0, 1)
    q_scale = HEAD_DIM ** -0.5 * LOG2E
    qk = _matmul(u, w_in_t, 2 * fw, BF16, w_t=True, scale_cols=(fw, q_scale), tm=2048,
                 name="proj_qk")
    tail_w = -(-(ql + kvl + IDX_DIM + fh + ih) // LANES) * LANES
    tail = _matmul(u, w_in_t, tail_w, F32, w_t=True, col0=3 * fw, tm=2048, name="proj_tail")
    cqn, ckvn, kin, small, cum = _tail_prep(tail, b, s, ql, kvl, fh, ih, q_norm_g, kv_norm_g,
                                            kidx_ln_g, kidx_ln_b, b_forget)
    v_t = _matmul(w_in_t, u, n, BF16, w_t=True, a_rows=(2 * fw, fw), tm=2048, name="proj_vt")

    t = 256 if s % 256 == 0 else 128
    fox = _attention("fox", b, s, fh, t, (qk, 0), (qk, 1), v_t, fox_out_g, cum=cum)

    q_d = _matmul(cqn, w_uq, dw, BF16, scale_cols=(dw, q_scale), tm=2048, name="proj_qd")
    q_i = _matmul(cqn, w_iq, ih * IDX_DIM, BF16, tm=2048, name="proj_qi")
    k_d = _matmul(ckvn, w_uk.reshape(kvl, dw), dw, BF16, tm=2048, name="expand_k")
    vd_t = _matmul(jnp.swapaxes(w_uv.reshape(kvl, dw), 0, 1), ckvn, n, BF16, w_t=True, tm=2048,
                   name="expand_vt")
    k_sel = min(TOPK_MAX, s // 4)
    mask = _indexer(q_i, kin, small, b, s, ih, fh, t, k_sel)
    bias_t = _bias_tiles(rel_bias, dh, t)
    dsa = _attention("dsa", b, s, dh, t, (q_d, 0), (k_d, 0), vd_t, dsa_out_g,
                     mask=mask, bias_tiles=bias_t)

    mix = _matmul(fox, w_o, d, F32, a2=dsa, tm=2048, name="proj_out")

    wr_pad = jnp.zeros((d, LANES), F32).at[:, :ne].set(w_router)
    br_pad = jnp.zeros((1, LANES), F32).at[0, :ne].set(b_router)
    x1, u_packed, eidx, rank, gate, cnt = _router(x.reshape(n, d), mix, mod3, ln1_g, ln1_b,
                                                  wr_pad, br_pad, s, ne, alpha_res)
    ts, tm = _moe_tiles(n, ne)
    n_tiles = -(-(n * TOP_K) // tm) + ne
    counts = cnt[0, :ne]
    ntile = (counts + tm - 1) // tm
    tend = jnp.cumsum(ntile)
    tstart = tend - ntile
    n_used = tend[-1]
    dest = (tstart * tm)[eidx[:, :TOP_K]] + rank[:, :TOP_K]
    dest_flat = dest.reshape(n * TOP_K).astype(I32)
    tid = jnp.minimum(jnp.arange(n_tiles, dtype=I32), n_used - 1)
    tile_e = jnp.minimum(jnp.searchsorted(tend, tid, side="right"), ne - 1).astype(I32)
    valid = jnp.clip(counts[tile_e] - (tid - tstart[tile_e]) * tm, 0, tm)
    valid = jnp.where(jnp.arange(n_tiles) < n_used, valid, 0)
    tile_nsub = ((valid + ts - 1) // ts).astype(I32)
    xs = _dispatch(dest_flat, u_packed, n_tiles * tm)
    y_sorted = _expert_ffn(xs, w1, b1, w2, b2, tile_e, tile_nsub,
                           n_used.reshape(1).astype(I32), tm=tm, ts=ts, n_tiles=n_tiles)
    out = _combine(dest_flat, y_sorted, gate, x1, mod3, ln2_g, ln2_b, s, alpha_res)
    return out.reshape(b, s, d)


def kernel(x, c, w_ada, b_ada, w_in, b_forget, q_norm_g, kv_norm_g, kidx_ln_g, kidx_ln_b, w_uq, w_uk, w_uv, w_iq, fox_out_g, dsa_out_g, w_o, ln1_g, ln1_b, w_router, b_router, w1, b1, w2, b2, ln2_g, ln2_b, rel_bias):
    depth = w_ada.shape[0]
    alpha_res = (2 * depth) ** 0.25
    b, d = c.shape
    c_pad = jnp.zeros((16, d), F32).at[:b].set(c)
    for l in range(depth):
        x = _layer(x, c_pad, w_ada[l], b_ada[l], w_in[l], b_forget[l], q_norm_g[l], kv_norm_g[l],
                   kidx_ln_g[l], kidx_ln_b[l], w_uq[l], w_uk[l], w_uv[l], w_iq[l], fox_out_g[l],
                   dsa_out_g[l], w_o[l], ln1_g[l], ln1_b[l], w_router[l], b_router[l],
                   w1[l], b1[l], w2[l], b2[l], ln2_g[l], ln2_b[l], rel_bias, alpha_res)
    return x
```

```python
import functools
import math

import numpy as np
import jax
import jax.numpy as jnp
from jax import lax
from jax.experimental import pallas as pl
from jax.experimental.pallas import tpu as pltpu

F32 = jnp.float32
BF16 = jnp.bfloat16
I32 = jnp.int32

LANES = 128
HEAD_DIM = 128
IDX_DIM = 128
TOPK_MAX = 256
N_BUCKETS = 32
MAX_DISTANCE = 128
TOP_K = 4
SWIGLU_LIMIT = 7.0
SWIGLU_ALPHA = 1.702
NEG_BIG = -1e30
LOG2E = math.log2(math.e)
VMEM_LIMIT = 56 * 1024 * 1024
INT_MIN = -2 ** 31
DMA_ISSUE_UNROLL = 8


def _tile(n, pref, unit=LANES):
    if n <= pref:
        return n
    t = (pref // unit) * unit
    while t > unit and n % t:
        t -= unit
    assert n % t == 0, (n, pref)
    return t


def _pack_bf16_pair(lo, hi):
    lo_bits = lax.bitcast_convert_type(lo.astype(BF16).astype(F32), I32)
    hi_bits = lax.bitcast_convert_type(hi.astype(BF16).astype(F32), I32)
    return lax.shift_right_logical(lo_bits, 16) | (hi_bits & jnp.int32(-65536))


def _unpack_bf16_pair(words):
    lo = lax.bitcast_convert_type(lax.shift_left(words, 16), F32)
    hi = lax.bitcast_convert_type(words & jnp.int32(-65536), F32)
    return lo, hi


def _params(sem, vmem=VMEM_LIMIT):
    return pltpu.CompilerParams(dimension_semantics=sem, vmem_limit_bytes=vmem)


def _mm_body(*refs, nk, nk1, a_silu, has_bias, col_block0, w_cols_if_overhang, w_t, out_scale):
    refs = list(refs)
    a_ref = refs.pop(0)
    a2_ref = refs.pop(0) if nk1 < nk else None
    w_ref = refs.pop(0)
    b_ref = refs.pop(0) if has_bias else None
    o_ref, acc_ref = refs
    k = pl.program_id(2)

    @pl.when(k == 0)
    def _():
        acc_ref[...] = jnp.zeros_like(acc_ref)

    def accumulate(src_ref):
        a = src_ref[...]
        if a_silu:
            a = a.astype(F32)
            a = a * jax.nn.sigmoid(a)
        w = w_ref[...]
        n_axis = 0 if w_t else 1
        if w_cols_if_overhang is not None:
            tn = w.shape[n_axis]
            col = lax.broadcasted_iota(I32, w.shape, n_axis) + (pl.program_id(1) + col_block0) * tn
            w = jnp.where(col < w_cols_if_overhang, w, 0.0)
        acc_ref[...] += lax.dot_general(a.astype(BF16), w.astype(BF16),
                                        (((1,), (1 if w_t else 0,)), ((), ())),
                                        preferred_element_type=F32)

    if a2_ref is None:
        accumulate(a_ref)
    else:
        pl.when(k < nk1)(lambda: accumulate(a_ref))
        pl.when(k >= nk1)(lambda: accumulate(a2_ref))

    @pl.when(k == nk - 1)
    def _():
        r = acc_ref[...]
        if has_bias:
            r = r + b_ref[...]
        if out_scale is not None:
            n_blocks, factor = out_scale
            r = r * jnp.where(pl.program_id(1) < n_blocks, factor, 1.0)
        o_ref[...] = r.astype(o_ref.dtype)


def _matmul(a, w, n_out, out_dtype, *, a2=None, bias=None, a_silu=False, col0=0, w_t=False,
            a_rows=None, scale_cols=None, tm=1024, tn=1024, tk=1024, name="matmul"):
    m, kdim = a.shape
    row0 = 0
    if a_rows is not None:
        row0, m = a_rows
    w_cols = w.shape[0] if w_t else w.shape[1]
    tm = _tile(math.gcd(m, row0), tm, 16)
    row_block0 = row0 // tm
    tn = _tile(math.gcd(n_out, col0, scale_cols[0] if scale_cols else 0), tn)
    tk = _tile(kdim, tk)
    nk1 = kdim // tk
    nk = nk1 if a2 is None else nk1 + a2.shape[1] // tk
    col_block0 = col0 // tn
    overhang = col0 + n_out > w_cols
    out_scale = None
    if scale_cols is not None:
        assert scale_cols[0] % tn == 0
        out_scale = (scale_cols[0] // tn, scale_cols[1])
    in_specs = [pl.BlockSpec((tm, tk), lambda i, j, k: (i + row_block0, jnp.minimum(k, nk1 - 1)))]
    args = [a]
    if a2 is not None:
        assert a2.shape[1] % tk == 0
        in_specs.append(pl.BlockSpec((tm, tk), lambda i, j, k: (i, jnp.maximum(k - nk1, 0))))
        args.append(a2)
    if w_t:
        in_specs.append(pl.BlockSpec((tn, tk), lambda i, j, k: (j + col_block0, k)))
    else:
        in_specs.append(pl.BlockSpec((tk, tn), lambda i, j, k: (k, j + col_block0)))
    args.append(w)
    if bias is not None:
        in_specs.append(pl.BlockSpec((1, tn), lambda i, j, k: (0, j)))
        args.append(bias)
    return pl.pallas_call(
        functools.partial(_mm_body, nk=nk, nk1=nk1, a_silu=a_silu, has_bias=bias is not None,
                          col_block0=col_block0, w_t=w_t, out_scale=out_scale,
                          w_cols_if_overhang=w_cols if overhang else None),
        grid=(m // tm, n_out // tn, nk),
        in_specs=in_specs,
        out_specs=pl.BlockSpec((tm, tn), lambda i, j, k: (i, j)),
        out_shape=jax.ShapeDtypeStruct((m, n_out), out_dtype),
        scratch_shapes=[pltpu.VMEM((tm, tn), F32)],
        compiler_params=_params(("parallel", "parallel", "arbitrary")),
        name=name,
    )(*args)


def _modulate_body(x_ref, sh_ref, sc_ref, o_ref):
    o_ref[0] = (x_ref[0] * (1.0 + sc_ref[0]) + sh_ref[0]).astype(o_ref.dtype)


def _modulate(x, mod3, shift_chunk, scale_chunk):
    b, s, d = x.shape
    ts = _tile(s, 512, 16)
    return pl.pallas_call(
        _modulate_body,
        grid=(b, s // ts),
        in_specs=[pl.BlockSpec((1, ts, d), lambda i, j: (i, j, 0)),
                  pl.BlockSpec((1, 1, d), lambda i, j: (i, 0, shift_chunk)),
                  pl.BlockSpec((1, 1, d), lambda i, j: (i, 0, scale_chunk))],
        out_specs=pl.BlockSpec((1, ts, d), lambda i, j: (i, j, 0)),
        out_shape=jax.ShapeDtypeStruct((b, s, d), BF16),
        compiler_params=_params(("parallel", "parallel")),
        name="modulate",
    )(x, mod3, mod3)


def _tail_body(t_ref, qg_ref, kvg_ref, lng_ref, lnb_ref, bf_ref,
               cq_ref, ckv_ref, kin_ref, small_ref, cum_ref, carry_ref, *, fh, ql, kvl, w_scale):
    j = pl.program_id(1)
    ts = t_ref.shape[0]

    @pl.when(j == 0)
    def _():
        carry_ref[...] = jnp.zeros_like(carry_ref)

    cq = t_ref[:, fh:fh + ql]
    cq_ref[...] = (cq * lax.rsqrt(jnp.mean(cq * cq, axis=-1, keepdims=True) + 1e-6)
                   * qg_ref[...]).astype(cq_ref.dtype)
    ckv = t_ref[:, fh + ql:fh + ql + kvl]
    ckv_ref[...] = (ckv * lax.rsqrt(jnp.mean(ckv * ckv, axis=-1, keepdims=True) + 1e-6)
                    * kvg_ref[...]).astype(ckv_ref.dtype)
    ki = t_ref[:, fh + ql + kvl:fh + ql + kvl + IDX_DIM]
    mu = jnp.mean(ki, axis=-1, keepdims=True)
    kc = ki - mu
    var = jnp.mean(kc * kc, axis=-1, keepdims=True)
    kin_ref[...] = (kc * lax.rsqrt(var + 1e-5) * lng_ref[...] + lnb_ref[...]).astype(kin_ref.dtype)

    small_ref[...] = t_ref[:, ql + kvl + IDX_DIM:ql + kvl + IDX_DIM + LANES] * w_scale
    z = t_ref[:, 0:LANES] + bf_ref[...]
    log_f = jnp.minimum(z, 0.0) - jnp.log(1.0 + jnp.exp(-jnp.abs(z)))
    row = lax.broadcasted_iota(I32, (ts, ts), 0)
    col = lax.broadcasted_iota(I32, (ts, ts), 1)
    tri = jnp.where(col <= row, 1.0, 0.0).astype(F32)
    incl = jnp.dot(tri, log_f, preferred_element_type=F32,
                   precision=lax.Precision.HIGHEST) + carry_ref[0:1, :]
    carry_ref[0:1, :] = incl[ts - 1:ts, :]
    cum_ref[...] = incl * (-LOG2E)


def _tail_prep(tail, b, s, ql, kvl, fh, ih, q_norm_g, kv_norm_g, ln_g, ln_b, b_forget):
    n = tail.shape[0]
    assert ql % LANES == 0 and kvl % LANES == 0
    ts = _tile(s, 256)
    nj = s // ts
    bf_pad = jnp.zeros((1, LANES), F32).at[0, :fh].set(b_forget)
    w_scale = (ih ** -0.5) * (IDX_DIM ** -0.5)
    row = lambda i, j: (i * nj + j, 0)
    const = lambda i, j: (0, 0)
    return pl.pallas_call(
        functools.partial(_tail_body, fh=fh, ql=ql, kvl=kvl, w_scale=w_scale),
        grid=(b, nj),
        in_specs=[pl.BlockSpec((ts, tail.shape[1]), row),
                  pl.BlockSpec((1, ql), const), pl.BlockSpec((1, kvl), const),
                  pl.BlockSpec((1, IDX_DIM), const), pl.BlockSpec((1, IDX_DIM), const),
                  pl.BlockSpec((1, LANES), const)],
        out_specs=[pl.BlockSpec((ts, ql), row), pl.BlockSpec((ts, kvl), row),
                   pl.BlockSpec((ts, IDX_DIM), row), pl.BlockSpec((ts, LANES), row),
                   pl.BlockSpec((ts, LANES), row)],
        out_shape=[jax.ShapeDtypeStruct((n, ql), BF16), jax.ShapeDtypeStruct((n, kvl), BF16),
                   jax.ShapeDtypeStruct((n, IDX_DIM), BF16), jax.ShapeDtypeStruct((n, LANES), F32),
                   jax.ShapeDtypeStruct((n, LANES), F32)],
        scratch_shapes=[pltpu.VMEM((8, LANES), F32)],
        compiler_params=_params(("parallel", "arbitrary")),
        name="tail_prep",
    )(tail, q_norm_g[None], kv_norm_g[None], ln_g[None], ln_b[None], bf_pad)


_MAX_EXACT = N_BUCKETS // 2
_BUCKET_THRESHOLDS = tuple(
    int(math.ceil(_MAX_EXACT * (MAX_DISTANCE / _MAX_EXACT) ** (k / (N_BUCKETS - _MAX_EXACT)) - 1e-9))
    for k in range(1, N_BUCKETS - _MAX_EXACT))


def _bias_body(rb_ref, o_ref, *, t):
    which = pl.program_id(0)
    h = pl.program_id(1)
    row = lax.broadcasted_iota(I32, (t, t), 0)
    col = lax.broadcasted_iota(I32, (t, t), 1)
    d = jnp.maximum(which * t + col - row, 0)
    large = jnp.full((t, t), _MAX_EXACT, I32)
    for thr in _BUCKET_THRESHOLDS:
        large = large + jnp.where(d >= thr, 1, 0)
    bucket = jnp.where(d < _MAX_EXACT, d, large)
    bias = jnp.zeros((t, t), F32)
    for bkt in range(N_BUCKETS):
        bias = jnp.where(bucket == bkt, rb_ref[bkt, h], bias)
    o_ref[0, 0] = (bias - rb_ref[N_BUCKETS - 1, h]) * LOG2E


def _bias_tiles(rel_bias, nh, t):
    assert 2 * t - (t - 1) >= _BUCKET_THRESHOLDS[-1], "key tiles two or more away are all last-bucket"
    return pl.pallas_call(
        functools.partial(_bias_body, t=t),
        grid=(2, nh),
        in_specs=[pl.BlockSpec(memory_space=pltpu.SMEM)],
        out_specs=pl.BlockSpec((1, 1, t, t), lambda w, h: (w, h, 0, 0)),
        out_shape=jax.ShapeDtypeStruct((2, nh, t, t), F32),
        compiler_params=_params(("parallel", "parallel")),
        name="bias_tiles",
    )(rel_bias)


ATTN_QCOLS = 128


def _attn_body(qi_ref, kj_ref, *refs, nh, t, mode):
    if mode == "fox":
        q_ref, k_ref, vt_ref, g_ref, cum_ref, o_ref, m_ref, l_ref, acc_ref, neg_ref = refs
    else:
        q_ref, k_ref, vt_ref, g_ref, mask_ref, bt_ref, o_ref, m_ref, l_ref, acc_ref, neg_ref = refs
    p = pl.program_id(1)
    qi = qi_ref[p]
    kj = kj_ref[p]
    nblk = t // ATTN_QCOLS

    @pl.when(kj == 0)
    def _():
        m_ref[...] = jnp.full_like(m_ref, NEG_BIG)
        l_ref[...] = jnp.zeros_like(l_ref)
        acc_ref[...] = jnp.zeros_like(acc_ref)

    def heads(addend):
        for h in range(nh):
            sl = slice(h * HEAD_DIM, (h + 1) * HEAD_DIM)
            for r in range(nblk):
                qs = slice(r * ATTN_QCOLS, (r + 1) * ATTN_QCOLS)
                idx = h * nblk + r
                s = lax.dot_general(k_ref[:, sl], q_ref[qs, sl], (((1,), (1,)), ((), ())),
                                    preferred_element_type=F32)
                s = s + addend(h, qs)
                m_prev = m_ref[idx]
                m_next = jnp.maximum(m_prev, jnp.max(s, axis=0, keepdims=True))
                alpha = jnp.exp2(m_prev - m_next)
                pr = jnp.exp2(s - m_next)
                l_ref[idx] = alpha * l_ref[idx] + jnp.sum(pr, axis=0, keepdims=True)
                acc_ref[idx] = acc_ref[idx] * alpha + jnp.dot(
                    vt_ref[sl, :], pr.astype(BF16), preferred_element_type=F32)
                m_ref[idx] = m_next

    key_i = lax.broadcasted_iota(I32, (t, t), 0)
    qry_i = lax.broadcasted_iota(I32, (t, t), 1)
    if mode == "fox":
        @pl.when(kj == qi)
        def _():
            neg_ref[...] = jnp.where(key_i <= qry_i, 0.0, NEG_BIG).astype(F32)
            heads(lambda h, qs: neg_ref[:, qs] + cum_ref[:, h:h + 1])

        @pl.when(kj != qi)
        def _():
            heads(lambda h, qs: cum_ref[:, h:h + 1])
    else:
        neg_ref[...] = (mask_ref[0, 0].astype(F32) - 1.0) * (-NEG_BIG)

        @pl.when(qi - kj >= 2)
        def _():
            heads(lambda h, qs: neg_ref[:, qs])

        @pl.when(qi - kj < 2)
        def _():
            near = qi - kj
            heads(lambda h, qs: neg_ref[:, qs] + bt_ref[near, h, :, qs])

    @pl.when(kj == qi)
    def _():
        for r in range(nblk):
            qs = slice(r * ATTN_QCOLS, (r + 1) * ATTN_QCOLS)
            ss = jnp.zeros((1, ATTN_QCOLS), F32)
            for h in range(nh):
                o = acc_ref[h * nblk + r] / l_ref[h * nblk + r]
                acc_ref[h * nblk + r] = o
                ss = ss + jnp.sum(o * o, axis=0, keepdims=True)
            rinv = lax.rsqrt(ss / (nh * HEAD_DIM) + 1e-6)
            for h in range(nh):
                sl = slice(h * HEAD_DIM, (h + 1) * HEAD_DIM)
                o_ref[qs, sl] = ((acc_ref[h * nblk + r] * rinv).T * g_ref[:, sl]).astype(o_ref.dtype)


def _attention(mode, b, s, nh, t, q_src, k_src, vt, gain, *, cum=None, mask=None, bias_tiles=None):
    nq = s // t
    pairs = [(i, j) for i in range(nq) for j in range(i + 1)]
    qi_arr = jnp.asarray(np.array([pq for pq, _ in pairs], np.int32))
    kj_arr = jnp.asarray(np.array([pk for _, pk in pairs], np.int32))
    hw = nh * HEAD_DIM
    n = b * s
    nblk = t // ATTN_QCOLS

    def qmap(cb):
        return lambda bi, p, qi, kj: (bi * nq + qi[p], cb)

    def kmap(cb):
        return lambda bi, p, qi, kj: (bi * nq + kj[p], cb)

    in_specs = [pl.BlockSpec((t, hw), qmap(q_src[1])),
                pl.BlockSpec((t, hw), kmap(k_src[1])),
                pl.BlockSpec((hw, t), lambda bi, p, qi, kj: (0, bi * nq + kj[p])),
                pl.BlockSpec((1, hw), lambda bi, p, qi, kj: (0, 0))]
    args = [q_src[0], k_src[0], vt, gain[None]]
    if mode == "fox":
        in_specs.append(pl.BlockSpec((t, LANES), kmap(0)))
        args.append(cum)
    else:
        in_specs += [pl.BlockSpec((1, 1, t, t), lambda bi, p, qi, kj: (bi, kj[p], 0, qi[p])),
                     pl.BlockSpec(memory_space=pltpu.VMEM)]
        args += [mask, bias_tiles]
    grid_spec = pltpu.PrefetchScalarGridSpec(
        num_scalar_prefetch=2,
        grid=(b, len(pairs)),
        in_specs=in_specs,
        out_specs=pl.BlockSpec((t, hw), lambda bi, p, qi, kj: (bi * nq + qi[p], 0)),
        scratch_shapes=[pltpu.VMEM((nh * nblk, 1, ATTN_QCOLS), F32),
                        pltpu.VMEM((nh * nblk, 1, ATTN_QCOLS), F32),
                        pltpu.VMEM((nh * nblk, HEAD_DIM, ATTN_QCOLS), F32),
                        pltpu.VMEM((t, t), F32)],
    )
    return pl.pallas_call(
        functools.partial(_attn_body, nh=nh, t=t, mode=mode),
        grid_spec=grid_spec,
        out_shape=jax.ShapeDtypeStruct((n, hw), BF16),
        compiler_params=_params(("parallel", "arbitrary")),
        name=mode + "_attention",
    )(qi_arr, kj_arr, *args)


IDX_QCOLS = 128
SUBLANES = 8


def _float_key(x):
    bits = lax.bitcast_convert_type(x, I32)
    return bits ^ (lax.shift_right_arithmetic(bits, 31) & 0x7FFFFFFF)


def _indexer_body(q_ref, k_ref, w_ref, o_ref, key_ref, *, ih, w_lane0, tq, ck, nchunk, k_sel):
    i = pl.program_id(1)
    n_valid = ((i + 1) * tq + ck - 1) // ck
    key_i = lax.broadcasted_iota(I32, (ck, tq), 0)
    qry_i = lax.broadcasted_iota(I32, (ck, tq), 1) + i * tq
    w_t = w_ref[...].T

    def score_chunk(c, carry):
        kc = k_ref[0, pl.ds(pl.multiple_of(c * ck, ck), ck), :]
        for r in range(tq // IDX_QCOLS):
            qs = slice(r * IDX_QCOLS, (r + 1) * IDX_QCOLS)
            acc = jnp.zeros((ck, IDX_QCOLS), F32)
            for h in range(ih):
                sh = lax.dot_general(kc, q_ref[qs, h * IDX_DIM:(h + 1) * IDX_DIM],
                                     (((1,), (1,)), ((), ())), preferred_element_type=F32)
                acc = acc + w_t[w_lane0 + h:w_lane0 + h + 1, qs] * jnp.maximum(sh, 0.0)
            causal = (lax.broadcasted_iota(I32, (ck, IDX_QCOLS), 0) + c * ck
                      <= lax.broadcasted_iota(I32, (ck, IDX_QCOLS), 1) + (i * tq + r * IDX_QCOLS))
            key_ref[c, :, qs] = _float_key(jnp.where(causal, acc, -jnp.inf))
        return carry

    lax.fori_loop(0, n_valid, score_chunk, 0)

    def count(pred_fn, level):
        def body(c, acc):
            ind = jnp.where(pred_fn(key_ref[c], level), 1, 0)
            return acc + jnp.sum(ind.reshape(ck // SUBLANES, SUBLANES, tq), axis=0)
        acc = lax.fori_loop(0, n_valid, body, jnp.zeros((SUBLANES, tq), I32))
        return jnp.sum(acc, axis=0, keepdims=True)

    ge = lambda kv, lv: kv >= lv
    t0 = jnp.full((1, tq), INT_MIN, I32)
    zero = jnp.zeros((1, tq), I32)
    thr = jnp.where(count(ge, zero) >= k_sel, zero, t0)

    def bit_step(bi, thr):
        cand = thr | lax.shift_left(jnp.int32(1), 30 - bi)
        return jnp.where(count(ge, cand) >= k_sel, cand, thr)

    thr = lax.fori_loop(0, 31, bit_step, thr)
    need = (k_sel - count(lambda kv, lv: kv > lv, thr)).astype(F32)

    lrow = lax.broadcasted_iota(I32, (ck, ck), 0)
    lcol = lax.broadcasted_iota(I32, (ck, ck), 1)
    lower = jnp.where(lcol <= lrow, 1.0, 0.0).astype(BF16)
    ones = jnp.ones((SUBLANES, ck), BF16)

    def select_chunk(c, offset):
        kv = key_ref[c]
        eq = kv == thr
        eqb = jnp.where(eq, 1.0, 0.0).astype(BF16)
        rank = jnp.dot(lower, eqb, preferred_element_type=F32) + offset
        take = (kv > thr) | (eq & (rank <= need))
        take = take & (key_i + c * ck <= qry_i)
        o_ref[0, c] = jnp.where(take, 1, 0).astype(jnp.int8)
        return offset + jnp.dot(ones, eqb, preferred_element_type=F32)[0:1, :]

    lax.fori_loop(0, n_valid, select_chunk, jnp.zeros((1, tq), F32))

    def zero_chunk(c, carry):
        o_ref[0, c] = jnp.zeros((ck, tq), jnp.int8)
        return carry

    lax.fori_loop(n_valid, nchunk, zero_chunk, 0)


def _indexer(q_idx, k_in, small, b, s, ih, w_lane0, ck, k_sel):
    tq = _tile(s, 256, IDX_QCOLS)
    nq = s // tq
    nchunk = s // ck
    k3 = k_in.reshape(b, s, IDX_DIM)
    return pl.pallas_call(
        functools.partial(_indexer_body, ih=ih, w_lane0=w_lane0, tq=tq, ck=ck, nchunk=nchunk,
                          k_sel=k_sel),
        grid=(b, nq),
        in_specs=[pl.BlockSpec((tq, ih * IDX_DIM), lambda bi, i: (bi * nq + i, 0)),
                  pl.BlockSpec((1, s, IDX_DIM), lambda bi, i: (bi, 0, 0)),
                  pl.BlockSpec((tq, LANES), lambda bi, i: (bi * nq + i, 0))],
        out_specs=pl.BlockSpec((1, nchunk, ck, tq), lambda bi, i: (bi, 0, 0, i)),
        out_shape=jax.ShapeDtypeStruct((b, nchunk, ck, s), jnp.int8),
        scratch_shapes=[pltpu.VMEM((nchunk, ck, tq), I32)],
        compiler_params=_params(("parallel", "parallel")),
        name="indexer_topk",
    )(q_idx, k3, small)


def _layer_norm(h, g, bvec):
    mu = jnp.mean(h, axis=-1, keepdims=True)
    hc = h - mu
    var = jnp.mean(hc * hc, axis=-1, keepdims=True)
    return hc * lax.rsqrt(var + 1e-5) * g + bvec


def _router_body(x_ref, mix_ref, ga_ref, shm_ref, scm_ref, lg_ref, lb_ref, wr_ref, br_ref,
                 x1_ref, up_ref, eidx_ref, rank_ref, gate_ref, cnt_ref, carry_ref,
                 *, alpha_res, ne, nsteps):
    i = pl.program_id(0)
    tm, d = x_ref.shape

    @pl.when(i == 0)
    def _():
        carry_ref[...] = jnp.zeros_like(carry_ref)

    x1 = _layer_norm(alpha_res * x_ref[...] + ga_ref[0] * mix_ref[...].astype(F32),
                     lg_ref[...], lb_ref[...])
    x1_ref[...] = x1
    u = x1 * (1.0 + scm_ref[0]) + shm_ref[0]
    up_ref[...] = _pack_bf16_pair(u[:, :d // 2], u[:, d // 2:])

    logits = jnp.dot(u, wr_ref[...], preferred_element_type=F32,
                     precision=lax.Precision.HIGHEST) + br_ref[...]
    lane = lax.broadcasted_iota(I32, (tm, LANES), 1)
    lane_f = lane.astype(F32)
    work = jnp.where(lane < ne, logits, -jnp.inf)
    sel = jnp.zeros((tm, LANES), F32)
    idxs, vals, hits = [], [], []
    for _ in range(TOP_K):
        mx = jnp.max(work, axis=1, keepdims=True)
        ik = jnp.min(jnp.where(work == mx, lane_f, float(LANES)), axis=1, keepdims=True)
        hit = lane_f == ik
        sel = jnp.where(hit, 1.0, sel)
        work = jnp.where(hit, -jnp.inf, work)
        idxs.append(ik.astype(I32))
        vals.append(mx)
        hits.append(hit)
    exps = [jnp.exp(v - vals[0]) for v in vals]
    denom = exps[0] + exps[1] + exps[2] + exps[3]

    row = lax.broadcasted_iota(I32, (tm, tm), 0)
    col = lax.broadcasted_iota(I32, (tm, tm), 1)
    tri = jnp.where(col < row, 1.0, 0.0).astype(BF16)
    before = jnp.dot(tri, sel.astype(BF16), preferred_element_type=F32) + carry_ref[0:1, :]
    total = carry_ref[0:1, :] + jnp.sum(sel, axis=0, keepdims=True)
    carry_ref[0:1, :] = total

    eidx = jnp.zeros((tm, LANES), I32)
    rank = jnp.zeros((tm, LANES), I32)
    gate = jnp.zeros((tm, LANES), F32)
    for kk in range(TOP_K):
        rk = jnp.sum(jnp.where(hits[kk], before, 0.0), axis=1, keepdims=True)
        eidx = jnp.where(lane == kk, idxs[kk], eidx)
        rank = jnp.where(lane == kk, rk.astype(I32), rank)
        gate = jnp.where(lane == kk, exps[kk] / denom, gate)
    eidx_ref[...] = eidx
    rank_ref[...] = rank
    gate_ref[...] = gate
    cnt_ref[...] = jnp.broadcast_to(total, cnt_ref.shape).astype(I32)


def _router(x2d, mix, mod3, ln_g, ln_b, w_router_pad, b_router_pad, s, ne, alpha_res):
    n, d = x2d.shape
    tm = _tile(s, 256, 16)
    nsteps = n // tm
    per_b = s // tm
    row = lambda i: (i, 0)
    const = lambda i: (0, 0)
    modspec = lambda chunk: pl.BlockSpec((1, 1, d), lambda i: (i // per_b, 0, chunk))
    return pl.pallas_call(
        functools.partial(_router_body, alpha_res=alpha_res, ne=ne, nsteps=nsteps),
        grid=(nsteps,),
        in_specs=[pl.BlockSpec((tm, d), row), pl.BlockSpec((tm, d), row),
                  modspec(2), modspec(3), modspec(4),
                  pl.BlockSpec((1, d), const), pl.BlockSpec((1, d), const),
                  pl.BlockSpec((d, LANES), const), pl.BlockSpec((1, LANES), const)],
        out_specs=[pl.BlockSpec((tm, d), row), pl.BlockSpec((tm, d // 2), row),
                   pl.BlockSpec((tm, LANES), row), pl.BlockSpec((tm, LANES), row),
                   pl.BlockSpec((tm, LANES), row), pl.BlockSpec((8, LANES), const)],
        out_shape=[jax.ShapeDtypeStruct((n, d), F32), jax.ShapeDtypeStruct((n, d // 2), I32),
                   jax.ShapeDtypeStruct((n, LANES), I32), jax.ShapeDtypeStruct((n, LANES), I32),
                   jax.ShapeDtypeStruct((n, LANES), F32), jax.ShapeDtypeStruct((8, LANES), I32)],
        scratch_shapes=[pltpu.VMEM((8, LANES), F32)],
        compiler_params=_params(("arbitrary",)),
        name="ln1_router",
    )(x2d, mix, mod3, mod3, mod3, ln_g[None], ln_b[None], w_router_pad, b_router_pad)


def _dispatch_body(dest_ref, u_ref, xs_ref, sem):
    i = pl.program_id(0)
    tc = u_ref.shape[0]

    def start(j, cc):
        for kk in range(TOP_K):
            d = dest_ref[(i * tc + j) * TOP_K + kk]
            pltpu.make_async_copy(u_ref.at[pl.ds(j, 1)], xs_ref.at[pl.ds(d, 1)], sem.at[0]).start()
        return cc

    lax.fori_loop(0, tc, start, 0, unroll=DMA_ISSUE_UNROLL)
    for _ in range(TOP_K):
        pltpu.make_async_copy(u_ref, xs_ref.at[pl.ds(0, tc)], sem.at[0]).wait()


def _dispatch(dest_flat, u_packed, n_slots):
    n, dw = u_packed.shape
    tc = _tile(n, 256, 8)
    grid_spec = pltpu.PrefetchScalarGridSpec(
        num_scalar_prefetch=1, grid=(n // tc,),
        in_specs=[pl.BlockSpec((tc, dw), lambda i, dr: (i, 0))],
        out_specs=pl.BlockSpec(memory_space=pl.ANY),
        scratch_shapes=[pltpu.SemaphoreType.DMA((1,))])
    return pl.pallas_call(
        _dispatch_body,
        grid_spec=grid_spec,
        out_shape=jax.ShapeDtypeStruct((n_slots, dw), I32),
        compiler_params=pltpu.CompilerParams(dimension_semantics=("arbitrary",),
                                             has_side_effects=True),
        name="moe_dispatch",
    )(dest_flat, u_packed)


def _ffn_body(te_ref, nsub_ref, nused_ref, x_ref, w1g_ref, w1u_ref, b1g_ref, b1u_ref,
              w2lo_ref, w2hi_ref, b2lo_ref, b2hi_ref, o_ref, h_ref, xbuf, xsem, *, nf, tf, ts, nr):
    i = pl.program_id(0)
    st = pl.program_id(1)
    _, tm, dh = xbuf.shape
    nsub = nsub_ref[i]
    slot = i % 2

    def x_copy(tile, to_slot):
        return pltpu.make_async_copy(x_ref.at[pl.ds(pl.multiple_of(tile * tm, tm), tm)],
                                     xbuf.at[to_slot], xsem.at[to_slot])

    @pl.when(jnp.logical_and(i == 0, st == 0))
    def _():
        x_copy(0, 0).start()

    @pl.when(st == 0)
    def _():
        x_copy(i, slot).wait()

    @pl.when(jnp.logical_and(st == nf, i + 1 < nused_ref[0]))
    def _():
        x_copy(i + 1, 1 - slot).start()

    def up_step(rows):
        lo, hi = _unpack_bf16_pair(xbuf[slot, 0:rows, :])
        lo, hi = lo.astype(BF16), hi.astype(BF16)

        def proj(w_ref, b_ref):
            return (jnp.dot(lo, w_ref[0, 0:dh, :].astype(BF16), preferred_element_type=F32)
                    + jnp.dot(hi, w_ref[0, dh:2 * dh, :].astype(BF16), preferred_element_type=F32)
                    + b_ref[0])
        gate = jnp.minimum(proj(w1g_ref, b1g_ref), SWIGLU_LIMIT)
        up = jnp.clip(proj(w1u_ref, b1u_ref), -SWIGLU_LIMIT, SWIGLU_LIMIT)
        act = (up + 1.0) * gate * jax.nn.sigmoid(SWIGLU_ALPHA * gate)
        h_ref[st, 0:rows, :] = act.astype(BF16)

    def down_step(rows):
        def proj(w_ref, b_ref):
            acc = jnp.broadcast_to(b_ref[0], (rows, o_ref.shape[1])).astype(F32)
            for f in range(nf):
                acc = acc + jnp.dot(h_ref[f, 0:rows, :],
                                    w_ref[0, f * tf:(f + 1) * tf, :].astype(BF16),
                                    preferred_element_type=F32)
            return acc
        o_ref[0:rows, :] = _pack_bf16_pair(proj(w2lo_ref, b2lo_ref), proj(w2hi_ref, b2hi_ref))
        if rows < tm:
            o_ref[rows:tm, :] = jnp.zeros((tm - rows, o_ref.shape[1]), I32)

    for r in range(1, nr + 1):
        @pl.when(jnp.logical_and(nsub == r, st < nf))
        def _(r=r):
            up_step(r * ts)

        @pl.when(jnp.logical_and(nsub == r, st >= nf))
        def _(r=r):
            down_step(r * ts)


def _expert_ffn(xs, w1, b1, w2, b2, tile_e, tile_nsub, n_used, *, tm, ts, n_tiles):
    ne, d, de2 = w1.shape
    de = de2 // 2
    dh = d // 2
    tf = _tile(de, 256)
    td = _tile(dh, 256)
    nf, nd = de // tf, dh // td
    nstep = nf + nd
    b1r = b1.reshape(ne, 1, de2)
    b2r = b2.reshape(ne, 1, d)

    def tile_idx(i, nu):
        return jnp.maximum(jnp.minimum(i, nu[0] - 1), 0)

    def step_idx(i, s, nu):
        return jnp.where(i < nu[0], s, nstep - 1)

    def up_map(off):
        return lambda i, s, te, ns, nu: (te[tile_idx(i, nu)], 0,
                                         off + jnp.minimum(step_idx(i, s, nu), nf - 1))

    def down_map(off):
        return lambda i, s, te, ns, nu: (te[tile_idx(i, nu)], 0,
                                         off + jnp.maximum(step_idx(i, s, nu) - nf, 0))

    grid_spec = pltpu.PrefetchScalarGridSpec(
        num_scalar_prefetch=3,
        grid=(n_used[0], nstep),
        in_specs=[pl.BlockSpec(memory_space=pl.ANY),
                  pl.BlockSpec((1, d, tf), up_map(0)), pl.BlockSpec((1, d, tf), up_map(nf)),
                  pl.BlockSpec((1, 1, tf), up_map(0)), pl.BlockSpec((1, 1, tf), up_map(nf)),
                  pl.BlockSpec((1, de, td), down_map(0)), pl.BlockSpec((1, de, td), down_map(nd)),
                  pl.BlockSpec((1, 1, td), down_map(0)), pl.BlockSpec((1, 1, td), down_map(nd))],
        out_specs=pl.BlockSpec((tm, td), lambda i, s, te, ns, nu: (
            tile_idx(i, nu), jnp.maximum(step_idx(i, s, nu) - nf, 0))),
        scratch_shapes=[pltpu.VMEM((nf, tm, tf), BF16), pltpu.VMEM((2, tm, dh), I32),
                        pltpu.SemaphoreType.DMA((2,))],
    )
    return pl.pallas_call(
        functools.partial(_ffn_body, nf=nf, tf=tf, ts=ts, nr=tm // ts),
        grid_spec=grid_spec,
        out_shape=jax.ShapeDtypeStruct((n_tiles * tm, dh), I32),
        compiler_params=_params(("arbitrary", "arbitrary")),
        name="expert_ffn",
    )(tile_e, tile_nsub, n_used, xs, w1, w1, b1r, b1r, w2, w2, b2r, b2r)


def _combine_body(dest_ref, y_ref, gate_ref, x1_ref, gm_ref, lg_ref, lb_ref, o_ref, ybuf, sem,
                  *, alpha_res):
    i = pl.program_id(0)
    nsteps = pl.num_programs(0)
    tc = x1_ref.shape[0]

    def gather_tile(tile, slot):
        def start(j, cc):
            for kk in range(TOP_K):
                d = dest_ref[(tile * tc + j) * TOP_K + kk]
                pltpu.make_async_copy(y_ref.at[pl.ds(d, 1)], ybuf.at[slot, kk, pl.ds(j, 1)],
                                      sem.at[slot]).start()
            return cc
        lax.fori_loop(0, tc, start, 0, unroll=DMA_ISSUE_UNROLL)

    slot = i % 2

    @pl.when(i == 0)
    def _():
        gather_tile(0, 0)

    @pl.when(i + 1 < nsteps)
    def _():
        gather_tile(i + 1, 1 - slot)

    for kk in range(TOP_K):
        pltpu.make_async_copy(y_ref.at[pl.ds(0, tc)], ybuf.at[slot, kk], sem.at[slot]).wait()
    g = gate_ref[...]
    y_lo = jnp.zeros((tc, ybuf.shape[3]), F32)
    y_hi = jnp.zeros((tc, ybuf.shape[3]), F32)
    for kk in range(TOP_K):
        lo, hi = _unpack_bf16_pair(ybuf[slot, kk])
        y_lo = y_lo + g[:, kk:kk + 1] * lo
        y_hi = y_hi + g[:, kk:kk + 1] * hi
    y = jnp.concatenate([y_lo, y_hi], axis=1)
    o_ref[...] = _layer_norm(alpha_res * x1_ref[...] + gm_ref[0] * y, lg_ref[...], lb_ref[...])


def _combine(dest_flat, y_sorted, gate, x1, mod3, ln_g, ln_b, s, alpha_res):
    n, d = x1.shape
    tc = _tile(s, 128, 8)
    per_b = s // tc
    row = lambda i, dr: (i, 0)
    const = lambda i, dr: (0, 0)
    grid_spec = pltpu.PrefetchScalarGridSpec(
        num_scalar_prefetch=1, grid=(n // tc,),
        in_specs=[pl.BlockSpec(memory_space=pl.ANY),
                  pl.BlockSpec((tc, LANES), row), pl.BlockSpec((tc, d), row),
                  pl.BlockSpec((1, 1, d), lambda i, dr: (i // per_b, 0, 5)),
                  pl.BlockSpec((1, d), const), pl.BlockSpec((1, d), const)],
        out_specs=pl.BlockSpec((tc, d), row),
        scratch_shapes=[pltpu.VMEM((2, TOP_K, tc, d // 2), I32), pltpu.SemaphoreType.DMA((2,))])
    return pl.pallas_call(
        functools.partial(_combine_body, alpha_res=alpha_res),
        grid_spec=grid_spec,
        out_shape=jax.ShapeDtypeStruct((n, d), F32),
        compiler_params=_params(("arbitrary",)),
        name="moe_combine",
    )(dest_flat, y_sorted, gate, x1, mod3, ln_g[None], ln_b[None])


MOE_SUBTILES_PER_TILE = 5


def _moe_tiles(n_tok, n_exp):
    mean_load = n_tok * TOP_K // n_exp
    ts = 16
    while ts * 2 <= min(256, mean_load // 4):
        ts *= 2
    return ts, MOE_SUBTILES_PER_TILE * ts


def _layer(x, c_pad, w_ada, b_ada, w_in, b_forget, q_norm_g, kv_norm_g, kidx_ln_g, kidx_ln_b,
           w_uq, w_uk, w_uv, w_iq, fox_out_g, dsa_out_g, w_o, ln1_g, ln1_b,
           w_router, b_router, w1, b1, w2, b2, ln2_g, ln2_b, rel_bias, alpha_res):
    b, s, d = x.shape
    n = b * s
    fh = b_forget.shape[0]
    ql = q_norm_g.shape[0]
    kvl = kv_norm_g.shape[0]
    dh = w_uk.shape[1]
    ih = w_iq.shape[1] // IDX_DIM
    ne = w_router.shape[1]
    fw = fh * HEAD_DIM
    dw = dh * HEAD_DIM
    assert fw == dw, "head groups share the attention kernel's column-block width"
    assert fh + ih <= LANES and ne <= LANES

    mod = _matmul(c_pad, w_ada, 6 * d, F32, bias=b_ada[None], a_silu=True,
                  tm=16, tn=512, tk=d, name="adaln")
    mod3 = mod[:b].reshape(b, 1, 6 * d)
    u = _modulate(x, mod3, 0, 1).reshape(n, d)

    w_in_t = jnp.swapaxes(w_in, 0, 1)
    q_scale = HEAD_DIM ** -0.5 * LOG2E
    qk = _matmul(u, w_in_t, 2 * fw, BF16, w_t=True, scale_cols=(fw, q_scale), tm=2048,
                 name="proj_qk")
    tail_w = -(-(ql + kvl + IDX_DIM + fh + ih) // LANES) * LANES
    tail = _matmul(u, w_in_t, tail_w, F32, w_t=True, col0=3 * fw, tm=2048, name="proj_tail")
    cqn, ckvn, kin, small, cum = _tail_prep(tail, b, s, ql, kvl, fh, ih, q_norm_g, kv_norm_g,
                                            kidx_ln_g, kidx_ln_b, b_forget)
    v_t = _matmul(w_in_t, u, n, BF16, w_t=True, a_rows=(2 * fw, fw), tm=2048, name="proj_vt")

    t = 256 if s % 256 == 0 else 128
    fox = _attention("fox", b, s, fh, t, (qk, 0), (qk, 1), v_t, fox_out_g, cum=cum)

    q_d = _matmul(cqn, w_uq, dw, BF16, scale_cols=(dw, q_scale), tm=2048, name="proj_qd")
    q_i = _matmul(cqn, w_iq, ih * IDX_DIM, BF16, tm=2048, name="proj_qi")
    k_d = _matmul(ckvn, w_uk.reshape(kvl, dw), dw, BF16, tm=2048, name="expand_k")
    vd_t = _matmul(jnp.swapaxes(w_uv.reshape(kvl, dw), 0, 1), ckvn, n, BF16, w_t=True, tm=2048,
                   name="expand_vt")
    k_sel = min(TOPK_MAX, s // 4)
    mask = _indexer(q_i, kin, small, b, s, ih, fh, t, k_sel)
    bias_t = _bias_tiles(rel_bias, dh, t)
    dsa = _attention("dsa", b, s, dh, t, (q_d, 0), (k_d, 0), vd_t, dsa_out_g,
                     mask=mask, bias_tiles=bias_t)

    mix = _matmul(fox, w_o, d, BF16, a2=dsa, tm=2048, name="proj_out")

    wr_pad = jnp.zeros((d, LANES), F32).at[:, :ne].set(w_router)
    br_pad = jnp.zeros((1, LANES), F32).at[0, :ne].set(b_router)
    x1, u_packed, eidx, rank, gate, cnt = _router(x.reshape(n, d), mix, mod3, ln1_g, ln1_b,
                                                  wr_pad, br_pad, s, ne, alpha_res)
    ts, tm = _moe_tiles(n, ne)
    n_tiles = -(-(n * TOP_K) // tm) + ne
    counts = cnt[0, :ne]
    ntile = (counts + tm - 1) // tm
    tend = jnp.cumsum(ntile)
    tstart = tend - ntile
    n_used = tend[-1]
    dest = (tstart * tm)[eidx[:, :TOP_K]] + rank[:, :TOP_K]
    dest_flat = dest.reshape(n * TOP_K).astype(I32)
    tid = jnp.minimum(jnp.arange(n_tiles, dtype=I32), n_used - 1)
    tile_e = jnp.minimum(jnp.searchsorted(tend, tid, side="right"), ne - 1).astype(I32)
    valid = jnp.clip(counts[tile_e] - (tid - tstart[tile_e]) * tm, 0, tm)
    valid = jnp.where(jnp.arange(n_tiles) < n_used, valid, 0)
    tile_nsub = ((valid + ts - 1) // ts).astype(I32)
    xs = _dispatch(dest_flat, u_packed, n_tiles * tm)
    y_sorted = _expert_ffn(xs, w1, b1, w2, b2, tile_e, tile_nsub,
                           n_used.reshape(1).astype(I32), tm=tm, ts=ts, n_tiles=n_tiles)
    out = _combine(dest_flat, y_sorted, gate, x1, mod3, ln2_g, ln2_b, s, alpha_res)
    return out.reshape(b, s, d)


def kernel(x, c, w_ada, b_ada, w_in, b_forget, q_norm_g, kv_norm_g, kidx_ln_g, kidx_ln_b, w_uq, w_uk, w_uv, w_iq, fox_out_g, dsa_out_g, w_o, ln1_g, ln1_b, w_router, b_router, w1, b1, w2, b2, ln2_g, ln2_b, rel_bias):
    depth = w_ada.shape[0]
    alpha_res = (2 * depth) ** 0.25
    b, d = c.shape
    c_pad = jnp.zeros((16, d), F32).at[:b].set(c)
    for l in range(depth):
        x = _layer(x, c_pad, w_ada[l], b_ada[l], w_in[l], b_forget[l], q_norm_g[l], kv_norm_g[l],
                   kidx_ln_g[l], kidx_ln_b[l], w_uq[l], w_uk[l], w_uv[l], w_iq[l], fox_out_g[l],
                   dsa_out_g[l], w_o[l], ln1_g[l], ln1_b[l], w_router[l], b_router[l],
                   w1[l], b1[l], w2[l], b2[l], ln2_g[l], ln2_b[l], rel_bias, alpha_res)
    return x
```

```python
import functools
import math

import numpy as np
import jax
import jax.numpy as jnp
from jax import lax
from jax.experimental import pallas as pl
from jax.experimental.pallas import tpu as pltpu

F32 = jnp.float32
BF16 = jnp.bfloat16
I32 = jnp.int32

LANES = 128
HEAD_DIM = 128
IDX_DIM = 128
TOPK_MAX = 256
N_BUCKETS = 32
MAX_DISTANCE = 128
TOP_K = 4
SWIGLU_LIMIT = 7.0
SWIGLU_ALPHA = 1.702
NEG_BIG = -1e30
LOG2E = math.log2(math.e)
VMEM_LIMIT = 56 * 1024 * 1024
INT_MIN = -2 ** 31
DMA_ISSUE_UNROLL = 8


def _tile(n, pref, unit=LANES):
    if n <= pref:
        return n
    t = (pref // unit) * unit
    while t > unit and n % t:
        t -= unit
    assert n % t == 0, (n, pref)
    return t


def _pack_bf16_pair(lo, hi):
    lo_bits = lax.bitcast_convert_type(lo.astype(BF16).astype(F32), I32)
    hi_bits = lax.bitcast_convert_type(hi.astype(BF16).astype(F32), I32)
    return lax.shift_right_logical(lo_bits, 16) | (hi_bits & jnp.int32(-65536))


def _unpack_bf16_pair(words):
    lo = lax.bitcast_convert_type(lax.shift_left(words, 16), F32)
    hi = lax.bitcast_convert_type(words & jnp.int32(-65536), F32)
    return lo, hi


def _params(sem, vmem=VMEM_LIMIT):
    return pltpu.CompilerParams(dimension_semantics=sem, vmem_limit_bytes=vmem)


def _mm_body(*refs, nk, nk1, a_silu, has_bias, col_block0, w_cols_if_overhang, w_t, out_scale):
    refs = list(refs)
    a_ref = refs.pop(0)
    a2_ref = refs.pop(0) if nk1 < nk else None
    w_ref = refs.pop(0)
    b_ref = refs.pop(0) if has_bias else None
    o_ref, acc_ref = refs
    k = pl.program_id(2)

    @pl.when(k == 0)
    def _():
        acc_ref[...] = jnp.zeros_like(acc_ref)

    def accumulate(src_ref):
        a = src_ref[...]
        if a_silu:
            a = a.astype(F32)
            a = a * jax.nn.sigmoid(a)
        w = w_ref[...]
        n_axis = 0 if w_t else 1
        if w_cols_if_overhang is not None:
            tn = w.shape[n_axis]
            col = lax.broadcasted_iota(I32, w.shape, n_axis) + (pl.program_id(1) + col_block0) * tn
            w = jnp.where(col < w_cols_if_overhang, w, 0.0)
        acc_ref[...] += lax.dot_general(a.astype(BF16), w.astype(BF16),
                                        (((1,), (1 if w_t else 0,)), ((), ())),
                                        preferred_element_type=F32)

    if a2_ref is None:
        accumulate(a_ref)
    else:
        pl.when(k < nk1)(lambda: accumulate(a_ref))
        pl.when(k >= nk1)(lambda: accumulate(a2_ref))

    @pl.when(k == nk - 1)
    def _():
        r = acc_ref[...]
        if has_bias:
            r = r + b_ref[...]
        if out_scale is not None:
            n_blocks, factor = out_scale
            r = r * jnp.where(pl.program_id(1) < n_blocks, factor, 1.0)
        o_ref[...] = r.astype(o_ref.dtype)


def _matmul(a, w, n_out, out_dtype, *, a2=None, bias=None, a_silu=False, col0=0, w_t=False,
            a_rows=None, scale_cols=None, tm=1024, tn=1024, tk=1024, name="matmul"):
    m, kdim = a.shape
    row0 = 0
    if a_rows is not None:
        row0, m = a_rows
    w_cols = w.shape[0] if w_t else w.shape[1]
    tm = _tile(math.gcd(m, row0), tm, 16)
    row_block0 = row0 // tm
    tn = _tile(math.gcd(n_out, col0, scale_cols[0] if scale_cols else 0), tn)
    tk = _tile(kdim, tk)
    nk1 = kdim // tk
    nk = nk1 if a2 is None else nk1 + a2.shape[1] // tk
    col_block0 = col0 // tn
    overhang = col0 + n_out > w_cols
    out_scale = None
    if scale_cols is not None:
        assert scale_cols[0] % tn == 0
        out_scale = (scale_cols[0] // tn, scale_cols[1])
    in_specs = [pl.BlockSpec((tm, tk), lambda i, j, k: (i + row_block0, jnp.minimum(k, nk1 - 1)))]
    args = [a]
    if a2 is not None:
        assert a2.shape[1] % tk == 0
        in_specs.append(pl.BlockSpec((tm, tk), lambda i, j, k: (i, jnp.maximum(k - nk1, 0))))
        args.append(a2)
    if w_t:
        in_specs.append(pl.BlockSpec((tn, tk), lambda i, j, k: (j + col_block0, k)))
    else:
        in_specs.append(pl.BlockSpec((tk, tn), lambda i, j, k: (k, j + col_block0)))
    args.append(w)
    if bias is not None:
        in_specs.append(pl.BlockSpec((1, tn), lambda i, j, k: (0, j)))
        args.append(bias)
    return pl.pallas_call(
        functools.partial(_mm_body, nk=nk, nk1=nk1, a_silu=a_silu, has_bias=bias is not None,
                          col_block0=col_block0, w_t=w_t, out_scale=out_scale,
                          w_cols_if_overhang=w_cols if overhang else None),
        grid=(m // tm, n_out // tn, nk),
        in_specs=in_specs,
        out_specs=pl.BlockSpec((tm, tn), lambda i, j, k: (i, j)),
        out_shape=jax.ShapeDtypeStruct((m, n_out), out_dtype),
        scratch_shapes=[pltpu.VMEM((tm, tn), F32)],
        compiler_params=_params(("parallel", "parallel", "arbitrary")),
        name=name,
    )(*args)


def _modulate_body(x_ref, sh_ref, sc_ref, o_ref):
    o_ref[0] = (x_ref[0] * (1.0 + sc_ref[0]) + sh_ref[0]).astype(o_ref.dtype)


def _modulate(x, mod3, shift_chunk, scale_chunk):
    b, s, d = x.shape
    ts = _tile(s, 512, 16)
    return pl.pallas_call(
        _modulate_body,
        grid=(b, s // ts),
        in_specs=[pl.BlockSpec((1, ts, d), lambda i, j: (i, j, 0)),
                  pl.BlockSpec((1, 1, d), lambda i, j: (i, 0, shift_chunk)),
                  pl.BlockSpec((1, 1, d), lambda i, j: (i, 0, scale_chunk))],
        out_specs=pl.BlockSpec((1, ts, d), lambda i, j: (i, j, 0)),
        out_shape=jax.ShapeDtypeStruct((b, s, d), BF16),
        compiler_params=_params(("parallel", "parallel")),
        name="modulate",
    )(x, mod3, mod3)


def _tail_body(t_ref, qg_ref, kvg_ref, lng_ref, lnb_ref, bf_ref,
               cq_ref, ckv_ref, kin_ref, small_ref, cum_ref, carry_ref, *, fh, ql, kvl, w_scale):
    j = pl.program_id(1)
    ts = t_ref.shape[0]

    @pl.when(j == 0)
    def _():
        carry_ref[...] = jnp.zeros_like(carry_ref)

    cq = t_ref[:, fh:fh + ql]
    cq_ref[...] = (cq * lax.rsqrt(jnp.mean(cq * cq, axis=-1, keepdims=True) + 1e-6)
                   * qg_ref[...]).astype(cq_ref.dtype)
    ckv = t_ref[:, fh + ql:fh + ql + kvl]
    ckv_ref[...] = (ckv * lax.rsqrt(jnp.mean(ckv * ckv, axis=-1, keepdims=True) + 1e-6)
                    * kvg_ref[...]).astype(ckv_ref.dtype)
    ki = t_ref[:, fh + ql + kvl:fh + ql + kvl + IDX_DIM]
    mu = jnp.mean(ki, axis=-1, keepdims=True)
    kc = ki - mu
    var = jnp.mean(kc * kc, axis=-1, keepdims=True)
    kin_ref[...] = (kc * lax.rsqrt(var + 1e-5) * lng_ref[...] + lnb_ref[...]).astype(kin_ref.dtype)

    small_ref[...] = t_ref[:, ql + kvl + IDX_DIM:ql + kvl + IDX_DIM + LANES] * w_scale
    z = t_ref[:, 0:LANES] + bf_ref[...]
    log_f = jnp.minimum(z, 0.0) - jnp.log(1.0 + jnp.exp(-jnp.abs(z)))
    row = lax.broadcasted_iota(I32, (ts, ts), 0)
    col = lax.broadcasted_iota(I32, (ts, ts), 1)
    tri = jnp.where(col <= row, 1.0, 0.0).astype(F32)
    incl = jnp.dot(tri, log_f, preferred_element_type=F32,
                   precision=lax.Precision.HIGHEST) + carry_ref[0:1, :]
    carry_ref[0:1, :] = incl[ts - 1:ts, :]
    cum_ref[...] = incl * (-LOG2E)


def _tail_prep(tail, b, s, ql, kvl, fh, ih, q_norm_g, kv_norm_g, ln_g, ln_b, b_forget):
    n = tail.shape[0]
    assert ql % LANES == 0 and kvl % LANES == 0
    ts = _tile(s, 256)
    nj = s // ts
    bf_pad = jnp.zeros((1, LANES), F32).at[0, :fh].set(b_forget)
    w_scale = (ih ** -0.5) * (IDX_DIM ** -0.5)
    row = lambda i, j: (i * nj + j, 0)
    const = lambda i, j: (0, 0)
    return pl.pallas_call(
        functools.partial(_tail_body, fh=fh, ql=ql, kvl=kvl, w_scale=w_scale),
        grid=(b, nj),
        in_specs=[pl.BlockSpec((ts, tail.shape[1]), row),
                  pl.BlockSpec((1, ql), const), pl.BlockSpec((1, kvl), const),
                  pl.BlockSpec((1, IDX_DIM), const), pl.BlockSpec((1, IDX_DIM), const),
                  pl.BlockSpec((1, LANES), const)],
        out_specs=[pl.BlockSpec((ts, ql), row), pl.BlockSpec((ts, kvl), row),
                   pl.BlockSpec((ts, IDX_DIM), row), pl.BlockSpec((ts, LANES), row),
                   pl.BlockSpec((ts, LANES), row)],
        out_shape=[jax.ShapeDtypeStruct((n, ql), BF16), jax.ShapeDtypeStruct((n, kvl), BF16),
                   jax.ShapeDtypeStruct((n, IDX_DIM), BF16), jax.ShapeDtypeStruct((n, LANES), F32),
                   jax.ShapeDtypeStruct((n, LANES), F32)],
        scratch_shapes=[pltpu.VMEM((8, LANES), F32)],
        compiler_params=_params(("parallel", "arbitrary")),
        name="tail_prep",
    )(tail, q_norm_g[None], kv_norm_g[None], ln_g[None], ln_b[None], bf_pad)


_MAX_EXACT = N_BUCKETS // 2
_BUCKET_THRESHOLDS = tuple(
    int(math.ceil(_MAX_EXACT * (MAX_DISTANCE / _MAX_EXACT) ** (k / (N_BUCKETS - _MAX_EXACT)) - 1e-9))
    for k in range(1, N_BUCKETS - _MAX_EXACT))


def _bias_body(rb_ref, o_ref, *, t):
    which = pl.program_id(0)
    row = lax.broadcasted_iota(I32, (t, t), 0)
    col = lax.broadcasted_iota(I32, (t, t), 1)
    d = jnp.maximum(which * t + col - row, 0)
    large = jnp.full((t, t), _MAX_EXACT, I32)
    for thr in _BUCKET_THRESHOLDS:
        large = large + jnp.where(d >= thr, 1, 0)
    bucket = jnp.where(d < _MAX_EXACT, d, large)
    for h in range(o_ref.shape[1]):
        bias = jnp.zeros((t, t), F32)
        for bkt in range(N_BUCKETS):
            bias = jnp.where(bucket == bkt, rb_ref[bkt, h], bias)
        o_ref[0, h] = (bias - rb_ref[N_BUCKETS - 1, h]) * LOG2E


def _bias_tiles(rel_bias, nh, t):
    assert 2 * t - (t - 1) >= _BUCKET_THRESHOLDS[-1], "key tiles two or more away are all last-bucket"
    return pl.pallas_call(
        functools.partial(_bias_body, t=t),
        grid=(2,),
        in_specs=[pl.BlockSpec(memory_space=pltpu.SMEM)],
        out_specs=pl.BlockSpec((1, nh, t, t), lambda w: (w, 0, 0, 0)),
        out_shape=jax.ShapeDtypeStruct((2, nh, t, t), F32),
        compiler_params=_params(("parallel",)),
        name="bias_tiles",
    )(rel_bias)


ATTN_QCOLS = 128


def _attn_body(qi_ref, kj_ref, *refs, nh, t, mode):
    if mode == "fox":
        q_ref, k_ref, vt_ref, g_ref, cum_ref, o_ref, m_ref, l_ref, acc_ref, neg_ref = refs
    else:
        q_ref, k_ref, vt_ref, g_ref, mask_ref, bt_ref, o_ref, m_ref, l_ref, acc_ref, neg_ref = refs
    p = pl.program_id(1)
    qi = qi_ref[p]
    kj = kj_ref[p]
    nblk = t // ATTN_QCOLS

    @pl.when(kj == 0)
    def _():
        m_ref[...] = jnp.full_like(m_ref, NEG_BIG)
        l_ref[...] = jnp.zeros_like(l_ref)
        acc_ref[...] = jnp.zeros_like(acc_ref)

    def heads(addend):
        for h in range(nh):
            sl = slice(h * HEAD_DIM, (h + 1) * HEAD_DIM)
            for r in range(nblk):
                qs = slice(r * ATTN_QCOLS, (r + 1) * ATTN_QCOLS)
                idx = h * nblk + r
                s = lax.dot_general(k_ref[:, sl], q_ref[qs, sl], (((1,), (1,)), ((), ())),
                                    preferred_element_type=F32)
                s = s + addend(h, qs)
                m_prev = m_ref[idx]
                m_next = jnp.maximum(m_prev, jnp.max(s, axis=0, keepdims=True))
                alpha = jnp.exp2(m_prev - m_next)
                pr = jnp.exp2(s - m_next)
                l_ref[idx] = alpha * l_ref[idx] + jnp.sum(pr, axis=0, keepdims=True)
                acc_ref[idx] = acc_ref[idx] * alpha + jnp.dot(
                    vt_ref[sl, :], pr.astype(BF16), preferred_element_type=F32)
                m_ref[idx] = m_next

    key_i = lax.broadcasted_iota(I32, (t, t), 0)
    qry_i = lax.broadcasted_iota(I32, (t, t), 1)
    if mode == "fox":
        @pl.when(kj == qi)
        def _():
            neg_ref[...] = jnp.where(key_i <= qry_i, 0.0, NEG_BIG).astype(F32)
            heads(lambda h, qs: neg_ref[:, qs] + cum_ref[:, h:h + 1])

        @pl.when(kj != qi)
        def _():
            heads(lambda h, qs: cum_ref[:, h:h + 1])
    else:
        neg_ref[...] = (mask_ref[0, 0].astype(F32) - 1.0) * (-NEG_BIG)

        @pl.when(qi - kj >= 2)
        def _():
            heads(lambda h, qs: neg_ref[:, qs])

        @pl.when(qi - kj < 2)
        def _():
            near = qi - kj
            heads(lambda h, qs: neg_ref[:, qs] + bt_ref[near, h, :, qs])

    @pl.when(kj == qi)
    def _():
        for r in range(nblk):
            qs = slice(r * ATTN_QCOLS, (r + 1) * ATTN_QCOLS)
            ss = jnp.zeros((1, ATTN_QCOLS), F32)
            for h in range(nh):
                o = acc_ref[h * nblk + r] / l_ref[h * nblk + r]
                acc_ref[h * nblk + r] = o
                ss = ss + jnp.sum(o * o, axis=0, keepdims=True)
            rinv = lax.rsqrt(ss / (nh * HEAD_DIM) + 1e-6)
            for h in range(nh):
                sl = slice(h * HEAD_DIM, (h + 1) * HEAD_DIM)
                o_ref[qs, sl] = ((acc_ref[h * nblk + r] * rinv).T * g_ref[:, sl]).astype(o_ref.dtype)


def _attention(mode, b, s, nh, t, q_src, k_src, vt, gain, *, cum=None, mask=None, bias_tiles=None):
    nq = s // t
    pairs = [(i, j) for i in range(nq) for j in range(i + 1)]
    qi_arr = jnp.asarray(np.array([pq for pq, _ in pairs], np.int32))
    kj_arr = jnp.asarray(np.array([pk for _, pk in pairs], np.int32))
    hw = nh * HEAD_DIM
    n = b * s
    nblk = t // ATTN_QCOLS

    def qmap(cb):
        return lambda bi, p, qi, kj: (bi * nq + qi[p], cb)

    def kmap(cb):
        return lambda bi, p, qi, kj: (bi * nq + kj[p], cb)

    in_specs = [pl.BlockSpec((t, hw), qmap(q_src[1])),
                pl.BlockSpec((t, hw), kmap(k_src[1])),
                pl.BlockSpec((hw, t), lambda bi, p, qi, kj: (0, bi * nq + kj[p])),
                pl.BlockSpec((1, hw), lambda bi, p, qi, kj: (0, 0))]
    args = [q_src[0], k_src[0], vt, gain[None]]
    if mode == "fox":
        in_specs.append(pl.BlockSpec((t, LANES), kmap(0)))
        args.append(cum)
    else:
        in_specs += [pl.BlockSpec((1, 1, t, t), lambda bi, p, qi, kj: (bi, kj[p], 0, qi[p])),
                     pl.BlockSpec(memory_space=pltpu.VMEM)]
        args += [mask, bias_tiles]
    grid_spec = pltpu.PrefetchScalarGridSpec(
        num_scalar_prefetch=2,
        grid=(b, len(pairs)),
        in_specs=in_specs,
        out_specs=pl.BlockSpec((t, hw), lambda bi, p, qi, kj: (bi * nq + qi[p], 0)),
        scratch_shapes=[pltpu.VMEM((nh * nblk, 1, ATTN_QCOLS), F32),
                        pltpu.VMEM((nh * nblk, 1, ATTN_QCOLS), F32),
                        pltpu.VMEM((nh * nblk, HEAD_DIM, ATTN_QCOLS), F32),
                        pltpu.VMEM((t, t), F32)],
    )
    return pl.pallas_call(
        functools.partial(_attn_body, nh=nh, t=t, mode=mode),
        grid_spec=grid_spec,
        out_shape=jax.ShapeDtypeStruct((n, hw), BF16),
        compiler_params=_params(("parallel", "arbitrary")),
        name=mode + "_attention",
    )(qi_arr, kj_arr, *args)


IDX_QCOLS = 128
SUBLANES = 8


def _float_key(x):
    bits = lax.bitcast_convert_type(x, I32)
    return bits ^ (lax.shift_right_arithmetic(bits, 31) & 0x7FFFFFFF)


def _indexer_body(q_ref, k_ref, w_ref, o_ref, key_ref, *, ih, w_lane0, tq, ck, nchunk, k_sel):
    i = pl.program_id(1)
    n_valid = ((i + 1) * tq + ck - 1) // ck
    key_i = lax.broadcasted_iota(I32, (ck, tq), 0)
    qry_i = lax.broadcasted_iota(I32, (ck, tq), 1) + i * tq
    w_t = w_ref[...].T

    def score_chunk(c, carry):
        kc = k_ref[0, pl.ds(pl.multiple_of(c * ck, ck), ck), :]
        for r in range(tq // IDX_QCOLS):
            qs = slice(r * IDX_QCOLS, (r + 1) * IDX_QCOLS)
            acc = jnp.zeros((ck, IDX_QCOLS), F32)
            for h in range(ih):
                sh = lax.dot_general(kc, q_ref[qs, h * IDX_DIM:(h + 1) * IDX_DIM],
                                     (((1,), (1,)), ((), ())), preferred_element_type=F32)
                acc = acc + w_t[w_lane0 + h:w_lane0 + h + 1, qs] * jnp.maximum(sh, 0.0)
            causal = (lax.broadcasted_iota(I32, (ck, IDX_QCOLS), 0) + c * ck
                      <= lax.broadcasted_iota(I32, (ck, IDX_QCOLS), 1) + (i * tq + r * IDX_QCOLS))
            key_ref[c, :, qs] = _float_key(jnp.where(causal, acc, -jnp.inf))
        return carry

    lax.fori_loop(0, n_valid, score_chunk, 0)

    def count(pred_fn, level):
        def body(c, acc):
            ind = jnp.where(pred_fn(key_ref[c], level), 1, 0)
            return acc + jnp.sum(ind.reshape(ck // SUBLANES, SUBLANES, tq), axis=0)
        acc = lax.fori_loop(0, n_valid, body, jnp.zeros((SUBLANES, tq), I32))
        return jnp.sum(acc, axis=0, keepdims=True)

    ge = lambda kv, lv: kv >= lv
    t0 = jnp.full((1, tq), INT_MIN, I32)
    zero = jnp.zeros((1, tq), I32)
    thr = jnp.where(count(ge, zero) >= k_sel, zero, t0)

    def bit_step(bi, thr):
        cand = thr | lax.shift_left(jnp.int32(1), 30 - bi)
        return jnp.where(count(ge, cand) >= k_sel, cand, thr)

    thr = lax.fori_loop(0, 31, bit_step, thr)
    need = (k_sel - count(lambda kv, lv: kv > lv, thr)).astype(F32)

    lrow = lax.broadcasted_iota(I32, (ck, ck), 0)
    lcol = lax.broadcasted_iota(I32, (ck, ck), 1)
    lower = jnp.where(lcol <= lrow, 1.0, 0.0).astype(BF16)
    ones = jnp.ones((SUBLANES, ck), BF16)

    def select_chunk(c, offset):
        kv = key_ref[c]
        eq = kv == thr
        eqb = jnp.where(eq, 1.0, 0.0).astype(BF16)
        rank = jnp.dot(lower, eqb, preferred_element_type=F32) + offset
        take = (kv > thr) | (eq & (rank <= need))
        take = take & (key_i + c * ck <= qry_i)
        o_ref[0, c] = jnp.where(take, 1, 0).astype(jnp.int8)
        return offset + jnp.dot(ones, eqb, preferred_element_type=F32)[0:1, :]

    lax.fori_loop(0, n_valid, select_chunk, jnp.zeros((1, tq), F32))

    def zero_chunk(c, carry):
        o_ref[0, c] = jnp.zeros((ck, tq), jnp.int8)
        return carry

    lax.fori_loop(n_valid, nchunk, zero_chunk, 0)


def _indexer(q_idx, k_in, small, b, s, ih, w_lane0, ck, k_sel):
    tq = _tile(s, 256, IDX_QCOLS)
    nq = s // tq
    nchunk = s // ck
    k3 = k_in.reshape(b, s, IDX_DIM)
    return pl.pallas_call(
        functools.partial(_indexer_body, ih=ih, w_lane0=w_lane0, tq=tq, ck=ck, nchunk=nchunk,
                          k_sel=k_sel),
        grid=(b, nq),
        in_specs=[pl.BlockSpec((tq, ih * IDX_DIM), lambda bi, i: (bi * nq + i, 0)),
                  pl.BlockSpec((1, s, IDX_DIM), lambda bi, i: (bi, 0, 0)),
                  pl.BlockSpec((tq, LANES), lambda bi, i: (bi * nq + i, 0))],
        out_specs=pl.BlockSpec((1, nchunk, ck, tq), lambda bi, i: (bi, 0, 0, i)),
        out_shape=jax.ShapeDtypeStruct((b, nchunk, ck, s), jnp.int8),
        scratch_shapes=[pltpu.VMEM((nchunk, ck, tq), I32)],
        compiler_params=_params(("parallel", "parallel")),
        name="indexer_topk",
    )(q_idx, k3, small)


def _layer_norm(h, g, bvec):
    mu = jnp.mean(h, axis=-1, keepdims=True)
    hc = h - mu
    var = jnp.mean(hc * hc, axis=-1, keepdims=True)
    return hc * lax.rsqrt(var + 1e-5) * g + bvec


def _router_body(x_ref, mix_ref, ga_ref, shm_ref, scm_ref, lg_ref, lb_ref, wr_ref, br_ref,
                 x1_ref, up_ref, eidx_ref, rank_ref, gate_ref, cnt_ref, carry_ref,
                 *, alpha_res, ne, nsteps):
    i = pl.program_id(0)
    tm, d = x_ref.shape

    @pl.when(i == 0)
    def _():
        carry_ref[...] = jnp.zeros_like(carry_ref)

    x1 = _layer_norm(alpha_res * x_ref[...] + ga_ref[0] * mix_ref[...].astype(F32),
                     lg_ref[...], lb_ref[...])
    x1_ref[...] = x1
    u = x1 * (1.0 + scm_ref[0]) + shm_ref[0]
    up_ref[...] = _pack_bf16_pair(u[:, :d // 2], u[:, d // 2:])

    def split(a):
        hi = a.astype(BF16)
        return hi, (a - hi.astype(F32)).astype(BF16)
    u_hi, u_lo = split(u)
    w_hi, w_lo = split(wr_ref[...])
    logits = (jnp.dot(u_hi, w_hi, preferred_element_type=F32)
              + jnp.dot(u_hi, w_lo, preferred_element_type=F32)
              + jnp.dot(u_lo, w_hi, preferred_element_type=F32)) + br_ref[...]
    lane = lax.broadcasted_iota(I32, (tm, LANES), 1)
    lane_f = lane.astype(F32)
    work = jnp.where(lane < ne, logits, -jnp.inf)
    sel = jnp.zeros((tm, LANES), F32)
    idxs, vals, hits = [], [], []
    for _ in range(TOP_K):
        mx = jnp.max(work, axis=1, keepdims=True)
        ik = jnp.min(jnp.where(work == mx, lane_f, float(LANES)), axis=1, keepdims=True)
        hit = lane_f == ik
        sel = jnp.where(hit, 1.0, sel)
        work = jnp.where(hit, -jnp.inf, work)
        idxs.append(ik.astype(I32))
        vals.append(mx)
        hits.append(hit)
    exps = [jnp.exp(v - vals[0]) for v in vals]
    denom = exps[0] + exps[1] + exps[2] + exps[3]

    row = lax.broadcasted_iota(I32, (tm, tm), 0)
    col = lax.broadcasted_iota(I32, (tm, tm), 1)
    tri = jnp.where(col < row, 1.0, 0.0).astype(BF16)
    before = jnp.dot(tri, sel.astype(BF16), preferred_element_type=F32) + carry_ref[0:1, :]
    total = carry_ref[0:1, :] + jnp.sum(sel, axis=0, keepdims=True)
    carry_ref[0:1, :] = total

    eidx = jnp.zeros((tm, LANES), I32)
    rank = jnp.zeros((tm, LANES), I32)
    gate = jnp.zeros((tm, LANES), F32)
    for kk in range(TOP_K):
        rk = jnp.sum(jnp.where(hits[kk], before, 0.0), axis=1, keepdims=True)
        eidx = jnp.where(lane == kk, idxs[kk], eidx)
        rank = jnp.where(lane == kk, rk.astype(I32), rank)
        gate = jnp.where(lane == kk, exps[kk] / denom, gate)
    eidx_ref[...] = eidx
    rank_ref[...] = rank
    gate_ref[...] = gate
    cnt_ref[...] = jnp.broadcast_to(total, cnt_ref.shape).astype(I32)


def _router(x2d, mix, mod3, ln_g, ln_b, w_router_pad, b_router_pad, s, ne, alpha_res):
    n, d = x2d.shape
    tm = _tile(s, 256, 16)
    nsteps = n // tm
    per_b = s // tm
    row = lambda i: (i, 0)
    const = lambda i: (0, 0)
    modspec = lambda chunk: pl.BlockSpec((1, 1, d), lambda i: (i // per_b, 0, chunk))
    return pl.pallas_call(
        functools.partial(_router_body, alpha_res=alpha_res, ne=ne, nsteps=nsteps),
        grid=(nsteps,),
        in_specs=[pl.BlockSpec((tm, d), row), pl.BlockSpec((tm, d), row),
                  modspec(2), modspec(3), modspec(4),
                  pl.BlockSpec((1, d), const), pl.BlockSpec((1, d), const),
                  pl.BlockSpec((d, LANES), const), pl.BlockSpec((1, LANES), const)],
        out_specs=[pl.BlockSpec((tm, d), row), pl.BlockSpec((tm, d // 2), row),
                   pl.BlockSpec((tm, LANES), row), pl.BlockSpec((tm, LANES), row),
                   pl.BlockSpec((tm, LANES), row), pl.BlockSpec((8, LANES), const)],
        out_shape=[jax.ShapeDtypeStruct((n, d), F32), jax.ShapeDtypeStruct((n, d // 2), I32),
                   jax.ShapeDtypeStruct((n, LANES), I32), jax.ShapeDtypeStruct((n, LANES), I32),
                   jax.ShapeDtypeStruct((n, LANES), F32), jax.ShapeDtypeStruct((8, LANES), I32)],
        scratch_shapes=[pltpu.VMEM((8, LANES), F32)],
        compiler_params=_params(("arbitrary",)),
        name="ln1_router",
    )(x2d, mix, mod3, mod3, mod3, ln_g[None], ln_b[None], w_router_pad, b_router_pad)


def _dispatch_body(dest_ref, u_ref, xs_ref, sem):
    i = pl.program_id(0)
    tc = u_ref.shape[0]

    def start(j, cc):
        for kk in range(TOP_K):
            d = dest_ref[(i * tc + j) * TOP_K + kk]
            pltpu.make_async_copy(u_ref.at[pl.ds(j, 1)], xs_ref.at[pl.ds(d, 1)], sem.at[0]).start()
        return cc

    lax.fori_loop(0, tc, start, 0, unroll=DMA_ISSUE_UNROLL)
    for _ in range(TOP_K):
        pltpu.make_async_copy(u_ref, xs_ref.at[pl.ds(0, tc)], sem.at[0]).wait()


def _dispatch(dest_flat, u_packed, n_slots):
    n, dw = u_packed.shape
    tc = _tile(n, 256, 8)
    grid_spec = pltpu.PrefetchScalarGridSpec(
        num_scalar_prefetch=1, grid=(n // tc,),
        in_specs=[pl.BlockSpec((tc, dw), lambda i, dr: (i, 0))],
        out_specs=pl.BlockSpec(memory_space=pl.ANY),
        scratch_shapes=[pltpu.SemaphoreType.DMA((1,))])
    return pl.pallas_call(
        _dispatch_body,
        grid_spec=grid_spec,
        out_shape=jax.ShapeDtypeStruct((n_slots, dw), I32),
        compiler_params=pltpu.CompilerParams(dimension_semantics=("arbitrary",),
                                             has_side_effects=True),
        name="moe_dispatch",
    )(dest_flat, u_packed)


def _ffn_body(te_ref, nsub_ref, nused_ref, x_ref, w1g_ref, w1u_ref, b1g_ref, b1u_ref,
              w2lo_ref, w2hi_ref, b2lo_ref, b2hi_ref, o_ref, h_ref, xbuf, xsem, *, nf, tf, ts, nr):
    i = pl.program_id(0)
    st = pl.program_id(1)
    _, tm, dh = xbuf.shape
    nsub = nsub_ref[i]
    slot = i % 2

    def x_copy(tile, to_slot):
        return pltpu.make_async_copy(x_ref.at[pl.ds(pl.multiple_of(tile * tm, tm), tm)],
                                     xbuf.at[to_slot], xsem.at[to_slot])

    @pl.when(jnp.logical_and(i == 0, st == 0))
    def _():
        x_copy(0, 0).start()

    @pl.when(st == 0)
    def _():
        x_copy(i, slot).wait()

    @pl.when(jnp.logical_and(st == nf, i + 1 < nused_ref[0]))
    def _():
        x_copy(i + 1, 1 - slot).start()

    def up_step(rows):
        lo, hi = _unpack_bf16_pair(xbuf[slot, 0:rows, :])
        lo, hi = lo.astype(BF16), hi.astype(BF16)

        def proj(w_ref, b_ref):
            return (jnp.dot(lo, w_ref[0, 0:dh, :].astype(BF16), preferred_element_type=F32)
                    + jnp.dot(hi, w_ref[0, dh:2 * dh, :].astype(BF16), preferred_element_type=F32)
                    + b_ref[0])
        gate = jnp.minimum(proj(w1g_ref, b1g_ref), SWIGLU_LIMIT)
        up = jnp.clip(proj(w1u_ref, b1u_ref), -SWIGLU_LIMIT, SWIGLU_LIMIT)
        act = (up + 1.0) * gate * jax.nn.sigmoid(SWIGLU_ALPHA * gate)
        h_ref[st, 0:rows, :] = act.astype(BF16)

    def down_step(rows):
        def proj(w_ref, b_ref):
            acc = jnp.broadcast_to(b_ref[0], (rows, o_ref.shape[1])).astype(F32)
            for f in range(nf):
                acc = acc + jnp.dot(h_ref[f, 0:rows, :],
                                    w_ref[0, f * tf:(f + 1) * tf, :].astype(BF16),
                                    preferred_element_type=F32)
            return acc
        o_ref[0:rows, :] = _pack_bf16_pair(proj(w2lo_ref, b2lo_ref), proj(w2hi_ref, b2hi_ref))
        if rows < tm:
            o_ref[rows:tm, :] = jnp.zeros((tm - rows, o_ref.shape[1]), I32)

    for r in range(1, nr + 1):
        @pl.when(jnp.logical_and(nsub == r, st < nf))
        def _(r=r):
            up_step(r * ts)

        @pl.when(jnp.logical_and(nsub == r, st >= nf))
        def _(r=r):
            down_step(r * ts)


def _expert_ffn(xs, w1, b1, w2, b2, tile_e, tile_nsub, n_used, *, tm, ts, n_tiles):
    ne, d, de2 = w1.shape
    de = de2 // 2
    dh = d // 2
    tf = _tile(de, 256)
    td = _tile(dh, 256)
    nf, nd = de // tf, dh // td
    nstep = nf + nd
    b1r = b1.reshape(ne, 1, de2)
    b2r = b2.reshape(ne, 1, d)

    def tile_idx(i, nu):
        return jnp.maximum(jnp.minimum(i, nu[0] - 1), 0)

    def step_idx(i, s, nu):
        return jnp.where(i < nu[0], s, nstep - 1)

    def up_map(off):
        return lambda i, s, te, ns, nu: (te[tile_idx(i, nu)], 0,
                                         off + jnp.minimum(step_idx(i, s, nu), nf - 1))

    def down_map(off):
        return lambda i, s, te, ns, nu: (te[tile_idx(i, nu)], 0,
                                         off + jnp.maximum(step_idx(i, s, nu) - nf, 0))

    grid_spec = pltpu.PrefetchScalarGridSpec(
        num_scalar_prefetch=3,
        grid=(n_used[0], nstep),
        in_specs=[pl.BlockSpec(memory_space=pl.ANY),
                  pl.BlockSpec((1, d, tf), up_map(0)), pl.BlockSpec((1, d, tf), up_map(nf)),
                  pl.BlockSpec((1, 1, tf), up_map(0)), pl.BlockSpec((1, 1, tf), up_map(nf)),
                  pl.BlockSpec((1, de, td), down_map(0)), pl.BlockSpec((1, de, td), down_map(nd)),
                  pl.BlockSpec((1, 1, td), down_map(0)), pl.BlockSpec((1, 1, td), down_map(nd))],
        out_specs=pl.BlockSpec((tm, td), lambda i, s, te, ns, nu: (
            tile_idx(i, nu), jnp.maximum(step_idx(i, s, nu) - nf, 0))),
        scratch_shapes=[pltpu.VMEM((nf, tm, tf), BF16), pltpu.VMEM((2, tm, dh), I32),
                        pltpu.SemaphoreType.DMA((2,))],
    )
    return pl.pallas_call(
        functools.partial(_ffn_body, nf=nf, tf=tf, ts=ts, nr=tm // ts),
        grid_spec=grid_spec,
        out_shape=jax.ShapeDtypeStruct((n_tiles * tm, dh), I32),
        compiler_params=_params(("arbitrary", "arbitrary")),
        name="expert_ffn",
    )(tile_e, tile_nsub, n_used, xs, w1, w1, b1r, b1r, w2, w2, b2r, b2r)


def _combine_body(dest_ref, y_ref, gate_ref, x1_ref, gm_ref, lg_ref, lb_ref, o_ref, ybuf, sem,
                  *, alpha_res):
    i = pl.program_id(0)
    nsteps = pl.num_programs(0)
    tc = x1_ref.shape[0]

    def gather_tile(tile, slot):
        def start(j, cc):
            for kk in range(TOP_K):
                d = dest_ref[(tile * tc + j) * TOP_K + kk]
                pltpu.make_async_copy(y_ref.at[pl.ds(d, 1)], ybuf.at[slot, kk, pl.ds(j, 1)],
                                      sem.at[slot]).start()
            return cc
        lax.fori_loop(0, tc, start, 0, unroll=DMA_ISSUE_UNROLL)

    slot = i % 2

    @pl.when(i == 0)
    def _():
        gather_tile(0, 0)

    @pl.when(i + 1 < nsteps)
    def _():
        gather_tile(i + 1, 1 - slot)

    for kk in range(TOP_K):
        pltpu.make_async_copy(y_ref.at[pl.ds(0, tc)], ybuf.at[slot, kk], sem.at[slot]).wait()
    g = gate_ref[...]
    y_lo = jnp.zeros((tc, ybuf.shape[3]), F32)
    y_hi = jnp.zeros((tc, ybuf.shape[3]), F32)
    for kk in range(TOP_K):
        lo, hi = _unpack_bf16_pair(ybuf[slot, kk])
        y_lo = y_lo + g[:, kk:kk + 1] * lo
        y_hi = y_hi + g[:, kk:kk + 1] * hi
    y = jnp.concatenate([y_lo, y_hi], axis=1)
    o_ref[...] = _layer_norm(alpha_res * x1_ref[...] + gm_ref[0] * y, lg_ref[...], lb_ref[...])


def _combine(dest_flat, y_sorted, gate, x1, mod3, ln_g, ln_b, s, alpha_res):
    n, d = x1.shape
    tc = _tile(s, 128, 8)
    per_b = s // tc
    row = lambda i, dr: (i, 0)
    const = lambda i, dr: (0, 0)
    grid_spec = pltpu.PrefetchScalarGridSpec(
        num_scalar_prefetch=1, grid=(n // tc,),
        in_specs=[pl.BlockSpec(memory_space=pl.ANY),
                  pl.BlockSpec((tc, LANES), row), pl.BlockSpec((tc, d), row),
                  pl.BlockSpec((1, 1, d), lambda i, dr: (i // per_b, 0, 5)),
                  pl.BlockSpec((1, d), const), pl.BlockSpec((1, d), const)],
        out_specs=pl.BlockSpec((tc, d), row),
        scratch_shapes=[pltpu.VMEM((2, TOP_K, tc, d // 2), I32), pltpu.SemaphoreType.DMA((2,))])
    return pl.pallas_call(
        functools.partial(_combine_body, alpha_res=alpha_res),
        grid_spec=grid_spec,
        out_shape=jax.ShapeDtypeStruct((n, d), F32),
        compiler_params=_params(("arbitrary",)),
        name="moe_combine",
    )(dest_flat, y_sorted, gate, x1, mod3, ln_g[None], ln_b[None])


MOE_SUBTILES_PER_TILE = 5


def _moe_tiles(n_tok, n_exp):
    mean_load = n_tok * TOP_K // n_exp
    ts = 16
    while ts * 2 <= min(256, mean_load // 4):
        ts *= 2
    return ts, MOE_SUBTILES_PER_TILE * ts


def _layer(x, c_pad, w_ada, b_ada, w_in, b_forget, q_norm_g, kv_norm_g, kidx_ln_g, kidx_ln_b,
           w_uq, w_uk, w_uv, w_iq, fox_out_g, dsa_out_g, w_o, ln1_g, ln1_b,
           w_router, b_router, w1, b1, w2, b2, ln2_g, ln2_b, rel_bias, alpha_res):
    b, s, d = x.shape
    n = b * s
    fh = b_forget.shape[0]
    ql = q_norm_g.shape[0]
    kvl = kv_norm_g.shape[0]
    dh = w_uk.shape[1]
    ih = w_iq.shape[1] // IDX_DIM
    ne = w_router.shape[1]
    fw = fh * HEAD_DIM
    dw = dh * HEAD_DIM
    assert fw == dw, "head groups share the attention kernel's column-block width"
    assert fh + ih <= LANES and ne <= LANES

    mod = _matmul(c_pad, w_ada, 6 * d, F32, bias=b_ada[None], a_silu=True,
                  tm=16, tn=512, tk=d, name="adaln")
    mod3 = mod[:b].reshape(b, 1, 6 * d)
    u = _modulate(x, mod3, 0, 1).reshape(n, d)

    w_in_t = jnp.swapaxes(w_in, 0, 1)
    q_scale = HEAD_DIM ** -0.5 * LOG2E
    qk = _matmul(u, w_in_t, 2 * fw, BF16, w_t=True, scale_cols=(fw, q_scale), tm=2048,
                 name="proj_qk")
    tail_w = -(-(ql + kvl + IDX_DIM + fh + ih) // LANES) * LANES
    tail = _matmul(u, w_in_t, tail_w, F32, w_t=True, col0=3 * fw, tm=2048, name="proj_tail")
    cqn, ckvn, kin, small, cum = _tail_prep(tail, b, s, ql, kvl, fh, ih, q_norm_g, kv_norm_g,
                                            kidx_ln_g, kidx_ln_b, b_forget)
    v_t = _matmul(w_in_t, u, n, BF16, w_t=True, a_rows=(2 * fw, fw), tm=2048, name="proj_vt")

    t = 256 if s % 256 == 0 else 128
    fox = _attention("fox", b, s, fh, t, (qk, 0), (qk, 1), v_t, fox_out_g, cum=cum)

    q_d = _matmul(cqn, w_uq, dw, BF16, scale_cols=(dw, q_scale), tm=2048, name="proj_qd")
    q_i = _matmul(cqn, w_iq, ih * IDX_DIM, BF16, tm=2048, name="proj_qi")
    k_d = _matmul(ckvn, w_uk.reshape(kvl, dw), dw, BF16, tm=2048, name="expand_k")
    vd_t = _matmul(jnp.swapaxes(w_uv.reshape(kvl, dw), 0, 1), ckvn, n, BF16, w_t=True, tm=2048,
                   name="expand_vt")
    k_sel = min(TOPK_MAX, s // 4)
    mask = _indexer(q_i, kin, small, b, s, ih, fh, t, k_sel)
    bias_t = _bias_tiles(rel_bias, dh, t)
    dsa = _attention("dsa", b, s, dh, t, (q_d, 0), (k_d, 0), vd_t, dsa_out_g,
                     mask=mask, bias_tiles=bias_t)

    mix = _matmul(fox, w_o, d, BF16, a2=dsa, tm=2048, name="proj_out")

    wr_pad = jnp.zeros((d, LANES), F32).at[:, :ne].set(w_router)
    br_pad = jnp.zeros((1, LANES), F32).at[0, :ne].set(b_router)
    x1, u_packed, eidx, rank, gate, cnt = _router(x.reshape(n, d), mix, mod3, ln1_g, ln1_b,
                                                  wr_pad, br_pad, s, ne, alpha_res)
    ts, tm = _moe_tiles(n, ne)
    n_tiles = -(-(n * TOP_K) // tm) + ne
    counts = cnt[0, :ne]
    ntile = (counts + tm - 1) // tm
    tend = jnp.cumsum(ntile)
    tstart = tend - ntile
    n_used = tend[-1]
    dest = (tstart * tm)[eidx[:, :TOP_K]] + rank[:, :TOP_K]
    dest_flat = dest.reshape(n * TOP_K).astype(I32)
    tid = jnp.minimum(jnp.arange(n_tiles, dtype=I32), n_used - 1)
    tile_e = jnp.minimum(jnp.searchsorted(tend, tid, side="right"), ne - 1).astype(I32)
    valid = jnp.clip(counts[tile_e] - (tid - tstart[tile_e]) * tm, 0, tm)
    valid = jnp.where(jnp.arange(n_tiles) < n_used, valid, 0)
    tile_nsub = ((valid + ts - 1) // ts).astype(I32)
    xs = _dispatch(dest_flat, u_packed, n_tiles * tm)
    y_sorted = _expert_ffn(xs, w1, b1, w2, b2, tile_e, tile_nsub,
                           n_used.reshape(1).astype(I32), tm=tm, ts=ts, n_tiles=n_tiles)
    out = _combine(dest_flat, y_sorted, gate, x1, mod3, ln2_g, ln2_b, s, alpha_res)
    return out.reshape(b, s, d)


def kernel(x, c, w_ada, b_ada, w_in, b_forget, q_norm_g, kv_norm_g, kidx_ln_g, kidx_ln_b, w_uq, w_uk, w_uv, w_iq, fox_out_g, dsa_out_g, w_o, ln1_g, ln1_b, w_router, b_router, w1, b1, w2, b2, ln2_g, ln2_b, rel_bias):
    depth = w_ada.shape[0]
    alpha_res = (2 * depth) ** 0.25
    b, d = c.shape
    c_pad = jnp.zeros((16, d), F32).at[:b].set(c)
    for l in range(depth):
        x = _layer(x, c_pad, w_ada[l], b_ada[l], w_in[l], b_forget[l], q_norm_g[l], kv_norm_g[l],
                   kidx_ln_g[l], kidx_ln_b[l], w_uq[l], w_uk[l], w_uv[l], w_iq[l], fox_out_g[l],
                   dsa_out_g[l], w_o[l], ln1_g[l], ln1_b[l], w_router[l], b_router[l],
                   w1[l], b1[l], w2[l], b2[l], ln2_g[l], ln2_b[l], rel_bias, alpha_res)
    return x
```

```python
import functools
import math

import numpy as np
import jax
import jax.numpy as jnp
from jax import lax
from jax.experimental import pallas as pl
from jax.experimental.pallas import tpu as pltpu

F32 = jnp.float32
BF16 = jnp.bfloat16
I32 = jnp.int32

LANES = 128
HEAD_DIM = 128
IDX_DIM = 128
TOPK_MAX = 256
N_BUCKETS = 32
MAX_DISTANCE = 128
TOP_K = 4
SWIGLU_LIMIT = 7.0
SWIGLU_ALPHA = 1.702
NEG_BIG = -1e30
LOG2E = math.log2(math.e)
VMEM_LIMIT = 56 * 1024 * 1024
INT_MIN = -2 ** 31
DMA_ISSUE_UNROLL = 8


def _tile(n, pref, unit=LANES):
    if n <= pref:
        return n
    t = (pref // unit) * unit
    while t > unit and n % t:
        t -= unit
    assert n % t == 0, (n, pref)
    return t


def _pack_bf16_pair(lo, hi):
    lo_bits = lax.bitcast_convert_type(lo.astype(BF16).astype(F32), I32)
    hi_bits = lax.bitcast_convert_type(hi.astype(BF16).astype(F32), I32)
    return lax.shift_right_logical(lo_bits, 16) | (hi_bits & jnp.int32(-65536))


def _unpack_bf16_pair(words):
    lo = lax.bitcast_convert_type(lax.shift_left(words, 16), F32)
    hi = lax.bitcast_convert_type(words & jnp.int32(-65536), F32)
    return lo, hi


def _params(sem, vmem=VMEM_LIMIT):
    return pltpu.CompilerParams(dimension_semantics=sem, vmem_limit_bytes=vmem)


def _mm_body(*refs, nk, nk1, a_silu, has_bias, col_block0, w_cols_if_overhang, w_t, out_scale):
    refs = list(refs)
    a_ref = refs.pop(0)
    a2_ref = refs.pop(0) if nk1 < nk else None
    w_ref = refs.pop(0)
    b_ref = refs.pop(0) if has_bias else None
    o_ref, acc_ref = refs
    k = pl.program_id(2)

    @pl.when(k == 0)
    def _():
        acc_ref[...] = jnp.zeros_like(acc_ref)

    def accumulate(src_ref):
        a = src_ref[...]
        if a_silu:
            a = a.astype(F32)
            a = a * jax.nn.sigmoid(a)
        w = w_ref[...]
        n_axis = 0 if w_t else 1
        if w_cols_if_overhang is not None:
            tn = w.shape[n_axis]
            col = lax.broadcasted_iota(I32, w.shape, n_axis) + (pl.program_id(1) + col_block0) * tn
            w = jnp.where(col < w_cols_if_overhang, w, 0.0)
        acc_ref[...] += lax.dot_general(a.astype(BF16), w.astype(BF16),
                                        (((1,), (1 if w_t else 0,)), ((), ())),
                                        preferred_element_type=F32)

    if a2_ref is None:
        accumulate(a_ref)
    else:
        pl.when(k < nk1)(lambda: accumulate(a_ref))
        pl.when(k >= nk1)(lambda: accumulate(a2_ref))

    @pl.when(k == nk - 1)
    def _():
        r = acc_ref[...]
        if has_bias:
            r = r + b_ref[...]
        if out_scale is not None:
            n_blocks, factor = out_scale
            r = r * jnp.where(pl.program_id(1) < n_blocks, factor, 1.0)
        o_ref[...] = r.astype(o_ref.dtype)


def _matmul(a, w, n_out, out_dtype, *, a2=None, bias=None, a_silu=False, col0=0, w_t=False,
            a_rows=None, scale_cols=None, tm=1024, tn=1024, tk=1024, name="matmul"):
    m, kdim = a.shape
    row0 = 0
    if a_rows is not None:
        row0, m = a_rows
    w_cols = w.shape[0] if w_t else w.shape[1]
    tm = _tile(math.gcd(m, row0), tm, 16)
    row_block0 = row0 // tm
    tn = _tile(math.gcd(n_out, col0, scale_cols[0] if scale_cols else 0), tn)
    tk = _tile(kdim, tk)
    nk1 = kdim // tk
    nk = nk1 if a2 is None else nk1 + a2.shape[1] // tk
    col_block0 = col0 // tn
    overhang = col0 + n_out > w_cols
    out_scale = None
    if scale_cols is not None:
        assert scale_cols[0] % tn == 0
        out_scale = (scale_cols[0] // tn, scale_cols[1])
    in_specs = [pl.BlockSpec((tm, tk), lambda i, j, k: (i + row_block0, jnp.minimum(k, nk1 - 1)))]
    args = [a]
    if a2 is not None:
        assert a2.shape[1] % tk == 0
        in_specs.append(pl.BlockSpec((tm, tk), lambda i, j, k: (i, jnp.maximum(k - nk1, 0))))
        args.append(a2)
    if w_t:
        in_specs.append(pl.BlockSpec((tn, tk), lambda i, j, k: (j + col_block0, k)))
    else:
        in_specs.append(pl.BlockSpec((tk, tn), lambda i, j, k: (k, j + col_block0)))
    args.append(w)
    if bias is not None:
        in_specs.append(pl.BlockSpec((1, tn), lambda i, j, k: (0, j)))
        args.append(bias)
    return pl.pallas_call(
        functools.partial(_mm_body, nk=nk, nk1=nk1, a_silu=a_silu, has_bias=bias is not None,
                          col_block0=col_block0, w_t=w_t, out_scale=out_scale,
                          w_cols_if_overhang=w_cols if overhang else None),
        grid=(m // tm, n_out // tn, nk),
        in_specs=in_specs,
        out_specs=pl.BlockSpec((tm, tn), lambda i, j, k: (i, j)),
        out_shape=jax.ShapeDtypeStruct((m, n_out), out_dtype),
        scratch_shapes=[pltpu.VMEM((tm, tn), F32)],
        compiler_params=_params(("parallel", "parallel", "arbitrary")),
        name=name,
    )(*args)


def _modulate_body(x_ref, sh_ref, sc_ref, o_ref):
    o_ref[0] = (x_ref[0] * (1.0 + sc_ref[0]) + sh_ref[0]).astype(o_ref.dtype)


def _modulate(x, mod3, shift_chunk, scale_chunk):
    b, s, d = x.shape
    ts = _tile(s, 512, 16)
    return pl.pallas_call(
        _modulate_body,
        grid=(b, s // ts),
        in_specs=[pl.BlockSpec((1, ts, d), lambda i, j: (i, j, 0)),
                  pl.BlockSpec((1, 1, d), lambda i, j: (i, 0, shift_chunk)),
                  pl.BlockSpec((1, 1, d), lambda i, j: (i, 0, scale_chunk))],
        out_specs=pl.BlockSpec((1, ts, d), lambda i, j: (i, j, 0)),
        out_shape=jax.ShapeDtypeStruct((b, s, d), BF16),
        compiler_params=_params(("parallel", "parallel")),
        name="modulate",
    )(x, mod3, mod3)


def _tail_body(t_ref, qg_ref, kvg_ref, lng_ref, lnb_ref, bf_ref,
               cq_ref, ckv_ref, kin_ref, small_ref, cum_ref, carry_ref, *, fh, ql, kvl, w_scale):
    j = pl.program_id(1)
    ts = t_ref.shape[0]

    @pl.when(j == 0)
    def _():
        carry_ref[...] = jnp.zeros_like(carry_ref)

    cq = t_ref[:, fh:fh + ql]
    cq_ref[...] = (cq * lax.rsqrt(jnp.mean(cq * cq, axis=-1, keepdims=True) + 1e-6)
                   * qg_ref[...]).astype(cq_ref.dtype)
    ckv = t_ref[:, fh + ql:fh + ql + kvl]
    ckv_ref[...] = (ckv * lax.rsqrt(jnp.mean(ckv * ckv, axis=-1, keepdims=True) + 1e-6)
                    * kvg_ref[...]).astype(ckv_ref.dtype)
    ki = t_ref[:, fh + ql + kvl:fh + ql + kvl + IDX_DIM]
    mu = jnp.mean(ki, axis=-1, keepdims=True)
    kc = ki - mu
    var = jnp.mean(kc * kc, axis=-1, keepdims=True)
    kin_ref[...] = (kc * lax.rsqrt(var + 1e-5) * lng_ref[...] + lnb_ref[...]).astype(kin_ref.dtype)

    small_ref[...] = t_ref[:, ql + kvl + IDX_DIM:ql + kvl + IDX_DIM + LANES] * w_scale
    z = t_ref[:, 0:LANES] + bf_ref[...]
    log_f = jnp.minimum(z, 0.0) - jnp.log(1.0 + jnp.exp(-jnp.abs(z)))
    row = lax.broadcasted_iota(I32, (ts, ts), 0)
    col = lax.broadcasted_iota(I32, (ts, ts), 1)
    tri = jnp.where(col <= row, 1.0, 0.0).astype(F32)
    incl = jnp.dot(tri, log_f, preferred_element_type=F32,
                   precision=lax.Precision.HIGHEST) + carry_ref[0:1, :]
    carry_ref[0:1, :] = incl[ts - 1:ts, :]
    cum_ref[...] = incl * (-LOG2E)


def _tail_prep(tail, b, s, ql, kvl, fh, ih, q_norm_g, kv_norm_g, ln_g, ln_b, b_forget):
    n = tail.shape[0]
    assert ql % LANES == 0 and kvl % LANES == 0
    ts = _tile(s, 256)
    nj = s // ts
    bf_pad = jnp.zeros((1, LANES), F32).at[0, :fh].set(b_forget)
    w_scale = (ih ** -0.5) * (IDX_DIM ** -0.5)
    row = lambda i, j: (i * nj + j, 0)
    const = lambda i, j: (0, 0)
    return pl.pallas_call(
        functools.partial(_tail_body, fh=fh, ql=ql, kvl=kvl, w_scale=w_scale),
        grid=(b, nj),
        in_specs=[pl.BlockSpec((ts, tail.shape[1]), row),
                  pl.BlockSpec((1, ql), const), pl.BlockSpec((1, kvl), const),
                  pl.BlockSpec((1, IDX_DIM), const), pl.BlockSpec((1, IDX_DIM), const),
                  pl.BlockSpec((1, LANES), const)],
        out_specs=[pl.BlockSpec((ts, ql), row), pl.BlockSpec((ts, kvl), row),
                   pl.BlockSpec((ts, IDX_DIM), row), pl.BlockSpec((ts, LANES), row),
                   pl.BlockSpec((ts, LANES), row)],
        out_shape=[jax.ShapeDtypeStruct((n, ql), BF16), jax.ShapeDtypeStruct((n, kvl), BF16),
                   jax.ShapeDtypeStruct((n, IDX_DIM), BF16), jax.ShapeDtypeStruct((n, LANES), F32),
                   jax.ShapeDtypeStruct((n, LANES), F32)],
        scratch_shapes=[pltpu.VMEM((8, LANES), F32)],
        compiler_params=_params(("parallel", "arbitrary")),
        name="tail_prep",
    )(tail, q_norm_g[None], kv_norm_g[None], ln_g[None], ln_b[None], bf_pad)


_MAX_EXACT = N_BUCKETS // 2
_BUCKET_THRESHOLDS = tuple(
    int(math.ceil(_MAX_EXACT * (MAX_DISTANCE / _MAX_EXACT) ** (k / (N_BUCKETS - _MAX_EXACT)) - 1e-9))
    for k in range(1, N_BUCKETS - _MAX_EXACT))


def _bias_body(rb_ref, o_ref, *, t):
    which = pl.program_id(0)
    row = lax.broadcasted_iota(I32, (t, t), 0)
    col = lax.broadcasted_iota(I32, (t, t), 1)
    d = jnp.maximum(which * t + col - row, 0)
    large = jnp.full((t, t), _MAX_EXACT, I32)
    for thr in _BUCKET_THRESHOLDS:
        large = large + jnp.where(d >= thr, 1, 0)
    bucket = jnp.where(d < _MAX_EXACT, d, large)
    for h in range(o_ref.shape[1]):
        bias = jnp.zeros((t, t), F32)
        for bkt in range(N_BUCKETS):
            bias = jnp.where(bucket == bkt, rb_ref[bkt, h], bias)
        o_ref[0, h] = (bias - rb_ref[N_BUCKETS - 1, h]) * LOG2E


def _bias_tiles(rel_bias, nh, t):
    assert 2 * t - (t - 1) >= _BUCKET_THRESHOLDS[-1], "key tiles two or more away are all last-bucket"
    return pl.pallas_call(
        functools.partial(_bias_body, t=t),
        grid=(2,),
        in_specs=[pl.BlockSpec(memory_space=pltpu.SMEM)],
        out_specs=pl.BlockSpec((1, nh, t, t), lambda w: (w, 0, 0, 0)),
        out_shape=jax.ShapeDtypeStruct((2, nh, t, t), F32),
        compiler_params=_params(("parallel",)),
        name="bias_tiles",
    )(rel_bias)


ATTN_QCOLS = 128


def _attn_body(qi_ref, kj_ref, *refs, nh, t, mode):
    if mode == "fox":
        q_ref, k_ref, vt_ref, g_ref, cum_ref, o_ref, m_ref, l_ref, acc_ref, neg_ref = refs
    else:
        q_ref, k_ref, vt_ref, g_ref, mask_ref, bt_ref, o_ref, m_ref, l_ref, acc_ref, neg_ref = refs
    p = pl.program_id(1)
    qi = qi_ref[p]
    kj = kj_ref[p]
    nblk = t // ATTN_QCOLS

    @pl.when(kj == 0)
    def _():
        m_ref[...] = jnp.full_like(m_ref, NEG_BIG)
        l_ref[...] = jnp.zeros_like(l_ref)
        acc_ref[...] = jnp.zeros_like(acc_ref)

    def heads(addend):
        for h in range(nh):
            sl = slice(h * HEAD_DIM, (h + 1) * HEAD_DIM)
            for r in range(nblk):
                qs = slice(r * ATTN_QCOLS, (r + 1) * ATTN_QCOLS)
                idx = h * nblk + r
                s = lax.dot_general(k_ref[:, sl], q_ref[qs, sl], (((1,), (1,)), ((), ())),
                                    preferred_element_type=F32)
                s = s + addend(h, qs)
                m_prev = m_ref[idx]
                m_next = jnp.maximum(m_prev, jnp.max(s, axis=0, keepdims=True))
                alpha = jnp.exp2(m_prev - m_next)
                pr = jnp.exp2(s - m_next)
                l_ref[idx] = alpha * l_ref[idx] + jnp.sum(pr, axis=0, keepdims=True)
                acc_ref[idx] = acc_ref[idx] * alpha + jnp.dot(
                    vt_ref[sl, :], pr.astype(BF16), preferred_element_type=F32)
                m_ref[idx] = m_next

    key_i = lax.broadcasted_iota(I32, (t, t), 0)
    qry_i = lax.broadcasted_iota(I32, (t, t), 1)
    if mode == "fox":
        @pl.when(kj == qi)
        def _():
            neg_ref[...] = jnp.where(key_i <= qry_i, 0.0, NEG_BIG).astype(F32)
            heads(lambda h, qs: neg_ref[:, qs] + cum_ref[:, h:h + 1])

        @pl.when(kj != qi)
        def _():
            heads(lambda h, qs: cum_ref[:, h:h + 1])
    else:
        neg_ref[...] = (mask_ref[0, 0].astype(F32) - 1.0) * (-NEG_BIG)

        @pl.when(qi - kj >= 2)
        def _():
            heads(lambda h, qs: neg_ref[:, qs])

        @pl.when(qi - kj < 2)
        def _():
            near = qi - kj
            heads(lambda h, qs: neg_ref[:, qs] + bt_ref[near, h, :, qs])

    @pl.when(kj == qi)
    def _():
        for r in range(nblk):
            qs = slice(r * ATTN_QCOLS, (r + 1) * ATTN_QCOLS)
            ss = jnp.zeros((1, ATTN_QCOLS), F32)
            for h in range(nh):
                o = acc_ref[h * nblk + r] / l_ref[h * nblk + r]
                acc_ref[h * nblk + r] = o
                ss = ss + jnp.sum(o * o, axis=0, keepdims=True)
            rinv = lax.rsqrt(ss / (nh * HEAD_DIM) + 1e-6)
            for h in range(nh):
                sl = slice(h * HEAD_DIM, (h + 1) * HEAD_DIM)
                o_ref[qs, sl] = ((acc_ref[h * nblk + r] * rinv).T * g_ref[:, sl]).astype(o_ref.dtype)


def _attention(mode, b, s, nh, t, q_src, k_src, vt, gain, *, cum=None, mask=None, bias_tiles=None):
    nq = s // t
    pairs = [(i, j) for i in range(nq) for j in range(i + 1)]
    qi_arr = jnp.asarray(np.array([pq for pq, _ in pairs], np.int32))
    kj_arr = jnp.asarray(np.array([pk for _, pk in pairs], np.int32))
    hw = nh * HEAD_DIM
    n = b * s
    nblk = t // ATTN_QCOLS

    def qmap(cb):
        return lambda bi, p, qi, kj: (bi * nq + qi[p], cb)

    def kmap(cb):
        return lambda bi, p, qi, kj: (bi * nq + kj[p], cb)

    in_specs = [pl.BlockSpec((t, hw), qmap(q_src[1])),
                pl.BlockSpec((t, hw), kmap(k_src[1])),
                pl.BlockSpec((hw, t), lambda bi, p, qi, kj: (0, bi * nq + kj[p])),
                pl.BlockSpec((1, hw), lambda bi, p, qi, kj: (0, 0))]
    args = [q_src[0], k_src[0], vt, gain[None]]
    if mode == "fox":
        in_specs.append(pl.BlockSpec((t, LANES), kmap(0)))
        args.append(cum)
    else:
        in_specs += [pl.BlockSpec((1, 1, t, t), lambda bi, p, qi, kj: (bi, kj[p], 0, qi[p])),
                     pl.BlockSpec(memory_space=pltpu.VMEM)]
        args += [mask, bias_tiles]
    grid_spec = pltpu.PrefetchScalarGridSpec(
        num_scalar_prefetch=2,
        grid=(b, len(pairs)),
        in_specs=in_specs,
        out_specs=pl.BlockSpec((t, hw), lambda bi, p, qi, kj: (bi * nq + qi[p], 0)),
        scratch_shapes=[pltpu.VMEM((nh * nblk, 1, ATTN_QCOLS), F32),
                        pltpu.VMEM((nh * nblk, 1, ATTN_QCOLS), F32),
                        pltpu.VMEM((nh * nblk, HEAD_DIM, ATTN_QCOLS), F32),
                        pltpu.VMEM((t, t), F32)],
    )
    return pl.pallas_call(
        functools.partial(_attn_body, nh=nh, t=t, mode=mode),
        grid_spec=grid_spec,
        out_shape=jax.ShapeDtypeStruct((n, hw), BF16),
        compiler_params=_params(("parallel", "arbitrary")),
        name=mode + "_attention",
    )(qi_arr, kj_arr, *args)


IDX_QCOLS = 128
SUBLANES = 8


def _float_key(x):
    bits = lax.bitcast_convert_type(x, I32)
    return bits ^ (lax.shift_right_arithmetic(bits, 31) & 0x7FFFFFFF)


def _indexer_body(q_ref, k_ref, w_ref, o_ref, key_ref, *, ih, w_lane0, tq, ck, nchunk, k_sel):
    i = pl.program_id(1)
    n_valid = ((i + 1) * tq + ck - 1) // ck
    key_i = lax.broadcasted_iota(I32, (ck, tq), 0)
    qry_i = lax.broadcasted_iota(I32, (ck, tq), 1) + i * tq
    w_t = w_ref[...].T

    def score_chunk(c, carry):
        kc = k_ref[0, pl.ds(pl.multiple_of(c * ck, ck), ck), :]
        for r in range(tq // IDX_QCOLS):
            qs = slice(r * IDX_QCOLS, (r + 1) * IDX_QCOLS)
            acc = jnp.zeros((ck, IDX_QCOLS), F32)
            for h in range(ih):
                sh = lax.dot_general(kc, q_ref[qs, h * IDX_DIM:(h + 1) * IDX_DIM],
                                     (((1,), (1,)), ((), ())), preferred_element_type=F32)
                acc = acc + w_t[w_lane0 + h:w_lane0 + h + 1, qs] * jnp.maximum(sh, 0.0)
            causal = (lax.broadcasted_iota(I32, (ck, IDX_QCOLS), 0) + c * ck
                      <= lax.broadcasted_iota(I32, (ck, IDX_QCOLS), 1) + (i * tq + r * IDX_QCOLS))
            key_ref[c, :, qs] = _float_key(jnp.where(causal, acc, -jnp.inf))
        return carry

    lax.fori_loop(0, n_valid, score_chunk, 0)

    def count(pred_fn, level):
        def body(c, acc):
            ind = jnp.where(pred_fn(key_ref[c], level), 1, 0)
            return acc + jnp.sum(ind.reshape(ck // SUBLANES, SUBLANES, tq), axis=0)
        acc = lax.fori_loop(0, n_valid, body, jnp.zeros((SUBLANES, tq), I32))
        return jnp.sum(acc, axis=0, keepdims=True)

    ge = lambda kv, lv: kv >= lv
    t0 = jnp.full((1, tq), INT_MIN, I32)
    zero = jnp.zeros((1, tq), I32)
    thr = jnp.where(count(ge, zero) >= k_sel, zero, t0)

    def bit_step(bi, thr):
        cand = thr | lax.shift_left(jnp.int32(1), 30 - bi)
        return jnp.where(count(ge, cand) >= k_sel, cand, thr)

    thr = lax.fori_loop(0, 31, bit_step, thr)
    need = (k_sel - count(lambda kv, lv: kv > lv, thr)).astype(F32)

    lrow = lax.broadcasted_iota(I32, (ck, ck), 0)
    lcol = lax.broadcasted_iota(I32, (ck, ck), 1)
    lower = jnp.where(lcol <= lrow, 1.0, 0.0).astype(BF16)
    ones = jnp.ones((SUBLANES, ck), BF16)

    def select_chunk(c, offset):
        kv = key_ref[c]
        eq = kv == thr
        eqb = jnp.where(eq, 1.0, 0.0).astype(BF16)
        rank = jnp.dot(lower, eqb, preferred_element_type=F32) + offset
        take = (kv > thr) | (eq & (rank <= need))
        take = take & (key_i + c * ck <= qry_i)
        o_ref[0, c] = jnp.where(take, 1, 0).astype(jnp.int8)
        return offset + jnp.dot(ones, eqb, preferred_element_type=F32)[0:1, :]

    lax.fori_loop(0, n_valid, select_chunk, jnp.zeros((1, tq), F32))

    def zero_chunk(c, carry):
        o_ref[0, c] = jnp.zeros((ck, tq), jnp.int8)
        return carry

    lax.fori_loop(n_valid, nchunk, zero_chunk, 0)


def _indexer(q_idx, k_in, small, b, s, ih, w_lane0, ck, k_sel):
    tq = _tile(s, 256, IDX_QCOLS)
    nq = s // tq
    nchunk = s // ck
    k3 = k_in.reshape(b, s, IDX_DIM)
    return pl.pallas_call(
        functools.partial(_indexer_body, ih=ih, w_lane0=w_lane0, tq=tq, ck=ck, nchunk=nchunk,
                          k_sel=k_sel),
        grid=(b, nq),
        in_specs=[pl.BlockSpec((tq, ih * IDX_DIM), lambda bi, i: (bi * nq + i, 0)),
                  pl.BlockSpec((1, s, IDX_DIM), lambda bi, i: (bi, 0, 0)),
                  pl.BlockSpec((tq, LANES), lambda bi, i: (bi * nq + i, 0))],
        out_specs=pl.BlockSpec((1, nchunk, ck, tq), lambda bi, i: (bi, 0, 0, i)),
        out_shape=jax.ShapeDtypeStruct((b, nchunk, ck, s), jnp.int8),
        scratch_shapes=[pltpu.VMEM((nchunk, ck, tq), I32)],
        compiler_params=_params(("parallel", "parallel")),
        name="indexer_topk",
    )(q_idx, k3, small)


def _layer_norm(h, g, bvec):
    mu = jnp.mean(h, axis=-1, keepdims=True)
    hc = h - mu
    var = jnp.mean(hc * hc, axis=-1, keepdims=True)
    return hc * lax.rsqrt(var + 1e-5) * g + bvec


def _router_body(x_ref, mix_ref, ga_ref, shm_ref, scm_ref, lg_ref, lb_ref, wr_ref, br_ref,
                 x1_ref, up_ref, eidx_ref, rank_ref, gate_ref, cnt_ref, carry_ref,
                 *, alpha_res, ne, nsteps):
    i = pl.program_id(0)
    tm, d = x_ref.shape

    @pl.when(i == 0)
    def _():
        carry_ref[...] = jnp.zeros_like(carry_ref)

    x1 = _layer_norm(alpha_res * x_ref[...] + ga_ref[0] * mix_ref[...].astype(F32),
                     lg_ref[...], lb_ref[...])
    x1_ref[...] = x1
    u = x1 * (1.0 + scm_ref[0]) + shm_ref[0]
    up_ref[...] = _pack_bf16_pair(u[:, :d // 2], u[:, d // 2:])

    def split(a):
        hi = a.astype(BF16)
        return hi, (a - hi.astype(F32)).astype(BF16)
    u_hi, u_lo = split(u)
    w_hi, w_lo = split(wr_ref[...])
    logits = (jnp.dot(u_hi, w_hi, preferred_element_type=F32)
              + jnp.dot(u_hi, w_lo, preferred_element_type=F32)
              + jnp.dot(u_lo, w_hi, preferred_element_type=F32)) + br_ref[...]
    lane = lax.broadcasted_iota(I32, (tm, LANES), 1)
    lane_f = lane.astype(F32)
    work = jnp.where(lane < ne, logits, -jnp.inf)
    sel = jnp.zeros((tm, LANES), F32)
    idxs, vals, hits = [], [], []
    for _ in range(TOP_K):
        mx = jnp.max(work, axis=1, keepdims=True)
        ik = jnp.min(jnp.where(work == mx, lane_f, float(LANES)), axis=1, keepdims=True)
        hit = lane_f == ik
        sel = jnp.where(hit, 1.0, sel)
        work = jnp.where(hit, -jnp.inf, work)
        idxs.append(ik.astype(I32))
        vals.append(mx)
        hits.append(hit)
    exps = [jnp.exp(v - vals[0]) for v in vals]
    denom = exps[0] + exps[1] + exps[2] + exps[3]

    row = lax.broadcasted_iota(I32, (tm, tm), 0)
    col = lax.broadcasted_iota(I32, (tm, tm), 1)
    tri = jnp.where(col < row, 1.0, 0.0).astype(BF16)
    before = jnp.dot(tri, sel.astype(BF16), preferred_element_type=F32) + carry_ref[0:1, :]
    total = carry_ref[0:1, :] + jnp.sum(sel, axis=0, keepdims=True)
    carry_ref[0:1, :] = total

    eidx = jnp.zeros((tm, LANES), I32)
    rank = jnp.zeros((tm, LANES), I32)
    gate = jnp.zeros((tm, LANES), F32)
    for kk in range(TOP_K):
        rk = jnp.sum(jnp.where(hits[kk], before, 0.0), axis=1, keepdims=True)
        eidx = jnp.where(lane == kk, idxs[kk], eidx)
        rank = jnp.where(lane == kk, rk.astype(I32), rank)
        gate = jnp.where(lane == kk, exps[kk] / denom, gate)
    eidx_ref[...] = eidx
    rank_ref[...] = rank
    gate_ref[...] = gate
    cnt_ref[...] = jnp.broadcast_to(total, cnt_ref.shape).astype(I32)


def _router(x2d, mix, mod3, ln_g, ln_b, w_router_pad, b_router_pad, s, ne, alpha_res):
    n, d = x2d.shape
    tm = _tile(s, 256, 16)
    nsteps = n // tm
    per_b = s // tm
    row = lambda i: (i, 0)
    const = lambda i: (0, 0)
    modspec = lambda chunk: pl.BlockSpec((1, 1, d), lambda i: (i // per_b, 0, chunk))
    return pl.pallas_call(
        functools.partial(_router_body, alpha_res=alpha_res, ne=ne, nsteps=nsteps),
        grid=(nsteps,),
        in_specs=[pl.BlockSpec((tm, d), row), pl.BlockSpec((tm, d), row),
                  modspec(2), modspec(3), modspec(4),
                  pl.BlockSpec((1, d), const), pl.BlockSpec((1, d), const),
                  pl.BlockSpec((d, LANES), const), pl.BlockSpec((1, LANES), const)],
        out_specs=[pl.BlockSpec((tm, d), row), pl.BlockSpec((tm, d // 2), row),
                   pl.BlockSpec((tm, LANES), row), pl.BlockSpec((tm, LANES), row),
                   pl.BlockSpec((tm, LANES), row), pl.BlockSpec((8, LANES), const)],
        out_shape=[jax.ShapeDtypeStruct((n, d), F32), jax.ShapeDtypeStruct((n, d // 2), I32),
                   jax.ShapeDtypeStruct((n, LANES), I32), jax.ShapeDtypeStruct((n, LANES), I32),
                   jax.ShapeDtypeStruct((n, LANES), F32), jax.ShapeDtypeStruct((8, LANES), I32)],
        scratch_shapes=[pltpu.VMEM((8, LANES), F32)],
        compiler_params=_params(("arbitrary",)),
        name="ln1_router",
    )(x2d, mix, mod3, mod3, mod3, ln_g[None], ln_b[None], w_router_pad, b_router_pad)


def _dispatch_body(dest_ref, u_ref, xs_ref, sem):
    i = pl.program_id(0)
    tc = u_ref.shape[0]

    def start(j, cc):
        for kk in range(TOP_K):
            d = dest_ref[(i * tc + j) * TOP_K + kk]
            pltpu.make_async_copy(u_ref.at[pl.ds(j, 1)], xs_ref.at[pl.ds(d, 1)], sem.at[0]).start()
        return cc

    lax.fori_loop(0, tc, start, 0, unroll=DMA_ISSUE_UNROLL)
    for _ in range(TOP_K):
        pltpu.make_async_copy(u_ref, xs_ref.at[pl.ds(0, tc)], sem.at[0]).wait()


def _dispatch(dest_flat, u_packed, n_slots):
    n, dw = u_packed.shape
    tc = _tile(n, 256, 8)
    grid_spec = pltpu.PrefetchScalarGridSpec(
        num_scalar_prefetch=1, grid=(n // tc,),
        in_specs=[pl.BlockSpec((tc, dw), lambda i, dr: (i, 0))],
        out_specs=pl.BlockSpec(memory_space=pl.ANY),
        scratch_shapes=[pltpu.SemaphoreType.DMA((1,))])
    return pl.pallas_call(
        _dispatch_body,
        grid_spec=grid_spec,
        out_shape=jax.ShapeDtypeStruct((n_slots, dw), I32),
        compiler_params=pltpu.CompilerParams(dimension_semantics=("arbitrary",),
                                             has_side_effects=True),
        name="moe_dispatch",
    )(dest_flat, u_packed)


def _ffn_body(te_ref, nsub_ref, nused_ref, x_ref, w1g_ref, w1u_ref, b1g_ref, b1u_ref,
              w2lo_ref, w2hi_ref, b2lo_ref, b2hi_ref, o_ref, h_ref, xbuf, xsem, *, nf, tf, ts, nr):
    i = pl.program_id(0)
    st = pl.program_id(1)
    _, tm, dh = xbuf.shape
    nsub = nsub_ref[i]
    slot = i % 2

    def x_copy(tile, to_slot):
        return pltpu.make_async_copy(x_ref.at[pl.ds(pl.multiple_of(tile * tm, tm), tm)],
                                     xbuf.at[to_slot], xsem.at[to_slot])

    @pl.when(jnp.logical_and(i == 0, st == 0))
    def _():
        x_copy(0, 0).start()

    @pl.when(st == 0)
    def _():
        x_copy(i, slot).wait()

    @pl.when(jnp.logical_and(st == nf, i + 1 < nused_ref[0]))
    def _():
        x_copy(i + 1, 1 - slot).start()

    def up_step(r0, rows):
        lo, hi = _unpack_bf16_pair(xbuf[slot, r0:r0 + rows, :])
        lo, hi = lo.astype(BF16), hi.astype(BF16)

        def proj(w_ref, b_ref):
            return (jnp.dot(lo, w_ref[0, 0:dh, :].astype(BF16), preferred_element_type=F32)
                    + jnp.dot(hi, w_ref[0, dh:2 * dh, :].astype(BF16), preferred_element_type=F32)
                    + b_ref[0])
        gate = jnp.minimum(proj(w1g_ref, b1g_ref), SWIGLU_LIMIT)
        up = jnp.clip(proj(w1u_ref, b1u_ref), -SWIGLU_LIMIT, SWIGLU_LIMIT)
        act = (up + 1.0) * gate * jax.nn.sigmoid(SWIGLU_ALPHA * gate)
        h_ref[st, r0:r0 + rows, :] = act.astype(BF16)

    def down_step(r0, rows, block_rows):
        def proj(w_ref, b_ref):
            acc = jnp.broadcast_to(b_ref[0], (rows, o_ref.shape[1])).astype(F32)
            for f in range(nf):
                acc = acc + jnp.dot(h_ref[f, r0:r0 + rows, :],
                                    w_ref[0, f * tf:(f + 1) * tf, :].astype(BF16),
                                    preferred_element_type=F32)
            return acc
        if rows:
            o_ref[r0:r0 + rows, :] = _pack_bf16_pair(proj(w2lo_ref, b2lo_ref),
                                                     proj(w2hi_ref, b2hi_ref))
        if rows < block_rows:
            o_ref[r0 + rows:r0 + block_rows, :] = jnp.zeros((block_rows - rows, o_ref.shape[1]),
                                                            I32)

    for blk in range(tm // (nr * ts)):
        r0 = blk * nr * ts
        nsub_b = jnp.clip(nsub - blk * nr, 0, nr)
        for r in range(nr + 1):
            if r:
                @pl.when(jnp.logical_and(nsub_b == r, st < nf))
                def _(r=r, r0=r0):
                    up_step(r0, r * ts)

            @pl.when(jnp.logical_and(nsub_b == r, st >= nf))
            def _(r=r, r0=r0):
                down_step(r0, r * ts, nr * ts)


def _expert_ffn(xs, w1, b1, w2, b2, tile_e, tile_nsub, n_used, *, tm, ts, n_tiles):
    ne, d, de2 = w1.shape
    de = de2 // 2
    dh = d // 2
    tf = _tile(de, 256)
    td = _tile(dh, 256)
    nf, nd = de // tf, dh // td
    nstep = nf + nd
    b1r = b1.reshape(ne, 1, de2)
    b2r = b2.reshape(ne, 1, d)

    def tile_idx(i, nu):
        return jnp.maximum(jnp.minimum(i, nu[0] - 1), 0)

    def step_idx(i, s, nu):
        return jnp.where(i < nu[0], s, nstep - 1)

    def up_map(off):
        return lambda i, s, te, ns, nu: (te[tile_idx(i, nu)], 0,
                                         off + jnp.minimum(step_idx(i, s, nu), nf - 1))

    def down_map(off):
        return lambda i, s, te, ns, nu: (te[tile_idx(i, nu)], 0,
                                         off + jnp.maximum(step_idx(i, s, nu) - nf, 0))

    grid_spec = pltpu.PrefetchScalarGridSpec(
        num_scalar_prefetch=3,
        grid=(n_used[0], nstep),
        in_specs=[pl.BlockSpec(memory_space=pl.ANY),
                  pl.BlockSpec((1, d, tf), up_map(0)), pl.BlockSpec((1, d, tf), up_map(nf)),
                  pl.BlockSpec((1, 1, tf), up_map(0)), pl.BlockSpec((1, 1, tf), up_map(nf)),
                  pl.BlockSpec((1, de, td), down_map(0)), pl.BlockSpec((1, de, td), down_map(nd)),
                  pl.BlockSpec((1, 1, td), down_map(0)), pl.BlockSpec((1, 1, td), down_map(nd))],
        out_specs=pl.BlockSpec((tm, td), lambda i, s, te, ns, nu: (
            tile_idx(i, nu), jnp.maximum(step_idx(i, s, nu) - nf, 0))),
        scratch_shapes=[pltpu.VMEM((nf, tm, tf), BF16), pltpu.VMEM((2, tm, dh), I32),
                        pltpu.SemaphoreType.DMA((2,))],
    )
    return pl.pallas_call(
        functools.partial(_ffn_body, nf=nf, tf=tf, ts=ts, nr=MOE_SUBTILES_PER_BLOCK),
        grid_spec=grid_spec,
        out_shape=jax.ShapeDtypeStruct((n_tiles * tm, dh), I32),
        compiler_params=_params(("arbitrary", "arbitrary")),
        name="expert_ffn",
    )(tile_e, tile_nsub, n_used, xs, w1, w1, b1r, b1r, w2, w2, b2r, b2r)


def _combine_body(dest_ref, y_ref, gate_ref, x1_ref, gm_ref, lg_ref, lb_ref, o_ref, ybuf, sem,
                  *, alpha_res):
    i = pl.program_id(0)
    nsteps = pl.num_programs(0)
    tc = x1_ref.shape[0]

    def gather_tile(tile, slot):
        def start(j, cc):
            for kk in range(TOP_K):
                d = dest_ref[(tile * tc + j) * TOP_K + kk]
                pltpu.make_async_copy(y_ref.at[pl.ds(d, 1)], ybuf.at[slot, kk, pl.ds(j, 1)],
                                      sem.at[slot]).start()
            return cc
        lax.fori_loop(0, tc, start, 0, unroll=DMA_ISSUE_UNROLL)

    slot = i % 2

    @pl.when(i == 0)
    def _():
        gather_tile(0, 0)

    @pl.when(i + 1 < nsteps)
    def _():
        gather_tile(i + 1, 1 - slot)

    for kk in range(TOP_K):
        pltpu.make_async_copy(y_ref.at[pl.ds(0, tc)], ybuf.at[slot, kk], sem.at[slot]).wait()
    g = gate_ref[...]
    y_lo = jnp.zeros((tc, ybuf.shape[3]), F32)
    y_hi = jnp.zeros((tc, ybuf.shape[3]), F32)
    for kk in range(TOP_K):
        lo, hi = _unpack_bf16_pair(ybuf[slot, kk])
        y_lo = y_lo + g[:, kk:kk + 1] * lo
        y_hi = y_hi + g[:, kk:kk + 1] * hi
    y = jnp.concatenate([y_lo, y_hi], axis=1)
    o_ref[...] = _layer_norm(alpha_res * x1_ref[...] + gm_ref[0] * y, lg_ref[...], lb_ref[...])


def _combine(dest_flat, y_sorted, gate, x1, mod3, ln_g, ln_b, s, alpha_res):
    n, d = x1.shape
    tc = _tile(s, 128, 8)
    per_b = s // tc
    row = lambda i, dr: (i, 0)
    const = lambda i, dr: (0, 0)
    grid_spec = pltpu.PrefetchScalarGridSpec(
        num_scalar_prefetch=1, grid=(n // tc,),
        in_specs=[pl.BlockSpec(memory_space=pl.ANY),
                  pl.BlockSpec((tc, LANES), row), pl.BlockSpec((tc, d), row),
                  pl.BlockSpec((1, 1, d), lambda i, dr: (i // per_b, 0, 5)),
                  pl.BlockSpec((1, d), const), pl.BlockSpec((1, d), const)],
        out_specs=pl.BlockSpec((tc, d), row),
        scratch_shapes=[pltpu.VMEM((2, TOP_K, tc, d // 2), I32), pltpu.SemaphoreType.DMA((2,))])
    return pl.pallas_call(
        functools.partial(_combine_body, alpha_res=alpha_res),
        grid_spec=grid_spec,
        out_shape=jax.ShapeDtypeStruct((n, d), F32),
        compiler_params=_params(("arbitrary",)),
        name="moe_combine",
    )(dest_flat, y_sorted, gate, x1, mod3, ln_g[None], ln_b[None])


MOE_SUBTILES_PER_BLOCK = 5
MOE_BLOCKS_PER_TILE = 2


def _moe_tiles(n_tok, n_exp):
    mean_load = n_tok * TOP_K // n_exp
    ts = 16
    while ts * 2 <= min(128, mean_load // 8):
        ts *= 2
    return ts, MOE_BLOCKS_PER_TILE * MOE_SUBTILES_PER_BLOCK * ts


def _layer(x, c_pad, w_ada, b_ada, w_in, b_forget, q_norm_g, kv_norm_g, kidx_ln_g, kidx_ln_b,
           w_uq, w_uk, w_uv, w_iq, fox_out_g, dsa_out_g, w_o, ln1_g, ln1_b,
           w_router, b_router, w1, b1, w2, b2, ln2_g, ln2_b, rel_bias, alpha_res):
    b, s, d = x.shape
    n = b * s
    fh = b_forget.shape[0]
    ql = q_norm_g.shape[0]
    kvl = kv_norm_g.shape[0]
    dh = w_uk.shape[1]
    ih = w_iq.shape[1] // IDX_DIM
    ne = w_router.shape[1]
    fw = fh * HEAD_DIM
    dw = dh * HEAD_DIM
    assert fw == dw, "head groups share the attention kernel's column-block width"
    assert fh + ih <= LANES and ne <= LANES

    mod = _matmul(c_pad, w_ada, 6 * d, F32, bias=b_ada[None], a_silu=True,
                  tm=16, tn=512, tk=d, name="adaln")
    mod3 = mod[:b].reshape(b, 1, 6 * d)
    u = _modulate(x, mod3, 0, 1).reshape(n, d)

    w_in_t = jnp.swapaxes(w_in, 0, 1)
    q_scale = HEAD_DIM ** -0.5 * LOG2E
    qk = _matmul(u, w_in_t, 2 * fw, BF16, w_t=True, scale_cols=(fw, q_scale), tm=2048,
                 name="proj_qk")
    tail_w = -(-(ql + kvl + IDX_DIM + fh + ih) // LANES) * LANES
    tail = _matmul(u, w_in_t, tail_w, F32, w_t=True, col0=3 * fw, tm=2048, name="proj_tail")
    cqn, ckvn, kin, small, cum = _tail_prep(tail, b, s, ql, kvl, fh, ih, q_norm_g, kv_norm_g,
                                            kidx_ln_g, kidx_ln_b, b_forget)
    v_t = _matmul(w_in_t, u, n, BF16, w_t=True, a_rows=(2 * fw, fw), tm=2048, name="proj_vt")

    t = 256 if s % 256 == 0 else 128
    fox = _attention("fox", b, s, fh, t, (qk, 0), (qk, 1), v_t, fox_out_g, cum=cum)

    q_d = _matmul(cqn, w_uq, dw, BF16, scale_cols=(dw, q_scale), tm=2048, name="proj_qd")
    q_i = _matmul(cqn, w_iq, ih * IDX_DIM, BF16, tm=2048, name="proj_qi")
    k_d = _matmul(ckvn, w_uk.reshape(kvl, dw), dw, BF16, tm=2048, name="expand_k")
    vd_t = _matmul(jnp.swapaxes(w_uv.reshape(kvl, dw), 0, 1), ckvn, n, BF16, w_t=True, tm=2048,
                   name="expand_vt")
    k_sel = min(TOPK_MAX, s // 4)
    mask = _indexer(q_i, kin, small, b, s, ih, fh, t, k_sel)
    bias_t = _bias_tiles(rel_bias, dh, t)
    dsa = _attention("dsa", b, s, dh, t, (q_d, 0), (k_d, 0), vd_t, dsa_out_g,
                     mask=mask, bias_tiles=bias_t)

    mix = _matmul(fox, w_o, d, BF16, a2=dsa, tm=2048, name="proj_out")

    wr_pad = jnp.zeros((d, LANES), F32).at[:, :ne].set(w_router)
    br_pad = jnp.zeros((1, LANES), F32).at[0, :ne].set(b_router)
    x1, u_packed, eidx, rank, gate, cnt = _router(x.reshape(n, d), mix, mod3, ln1_g, ln1_b,
                                                  wr_pad, br_pad, s, ne, alpha_res)
    ts, tm = _moe_tiles(n, ne)
    n_tiles = -(-(n * TOP_K) // tm) + ne
    counts = cnt[0, :ne]
    ntile = (counts + tm - 1) // tm
    tend = jnp.cumsum(ntile)
    tstart = tend - ntile
    n_used = tend[-1]
    dest = (tstart * tm)[eidx[:, :TOP_K]] + rank[:, :TOP_K]
    dest_flat = dest.reshape(n * TOP_K).astype(I32)
    tid = jnp.minimum(jnp.arange(n_tiles, dtype=I32), n_used - 1)
    tile_e = jnp.minimum(jnp.searchsorted(tend, tid, side="right"), ne - 1).astype(I32)
    valid = jnp.clip(counts[tile_e] - (tid - tstart[tile_e]) * tm, 0, tm)
    valid = jnp.where(jnp.arange(n_tiles) < n_used, valid, 0)
    tile_nsub = ((valid + ts - 1) // ts).astype(I32)
    xs = _dispatch(dest_flat, u_packed, n_tiles * tm)
    y_sorted = _expert_ffn(xs, w1, b1, w2, b2, tile_e, tile_nsub,
                           n_used.reshape(1).astype(I32), tm=tm, ts=ts, n_tiles=n_tiles)
    out = _combine(dest_flat, y_sorted, gate, x1, mod3, ln2_g, ln2_b, s, alpha_res)
    return out.reshape(b, s, d)


def kernel(x, c, w_ada, b_ada, w_in, b_forget, q_norm_g, kv_norm_g, kidx_ln_g, kidx_ln_b, w_uq, w_uk, w_uv, w_iq, fox_out_g, dsa_out_g, w_o, ln1_g, ln1_b, w_router, b_router, w1, b1, w2, b2, ln2_g, ln2_b, rel_bias):
    depth = w_ada.shape[0]
    alpha_res = (2 * depth) ** 0.25
    b, d = c.shape
    c_pad = jnp.zeros((16, d), F32).at[:b].set(c)
    for l in range(depth):
        x = _layer(x, c_pad, w_ada[l], b_ada[l], w_in[l], b_forget[l], q_norm_g[l], kv_norm_g[l],
                   kidx_ln_g[l], kidx_ln_b[l], w_uq[l], w_uk[l], w_uv[l], w_iq[l], fox_out_g[l],
                   dsa_out_g[l], w_o[l], ln1_g[l], ln1_b[l], w_router[l], b_router[l],
                   w1[l], b1[l], w2[l], b2[l], ln2_g[l], ln2_b[l], rel_bias, alpha_res)
    return x
```

```python
import functools
import math

import numpy as np
import jax
import jax.numpy as jnp
from jax import lax
from jax.experimental import pallas as pl
from jax.experimental.pallas import tpu as pltpu

F32 = jnp.float32
BF16 = jnp.bfloat16
I32 = jnp.int32

LANES = 128
HEAD_DIM = 128
IDX_DIM = 128
TOPK_MAX = 256
N_BUCKETS = 32
MAX_DISTANCE = 128
TOP_K = 4
SWIGLU_LIMIT = 7.0
SWIGLU_ALPHA = 1.702
NEG_BIG = -1e30
LOG2E = math.log2(math.e)
VMEM_LIMIT = 56 * 1024 * 1024
INT_MIN = -2 ** 31
DMA_ISSUE_UNROLL = 8
WIDE_TILES = dict(tm=2048, tn=2048, tk=512)


def _tile(n, pref, unit=LANES):
    if n <= pref:
        return n
    t = (pref // unit) * unit
    while t > unit and n % t:
        t -= unit
    assert n % t == 0, (n, pref)
    return t


def _pack_bf16_pair(lo, hi):
    lo_bits = lax.bitcast_convert_type(lo.astype(BF16).astype(F32), I32)
    hi_bits = lax.bitcast_convert_type(hi.astype(BF16).astype(F32), I32)
    return lax.shift_right_logical(lo_bits, 16) | (hi_bits & jnp.int32(-65536))


def _unpack_bf16_pair(words):
    lo = lax.bitcast_convert_type(lax.shift_left(words, 16), F32)
    hi = lax.bitcast_convert_type(words & jnp.int32(-65536), F32)
    return lo, hi


def _params(sem, vmem=VMEM_LIMIT):
    return pltpu.CompilerParams(dimension_semantics=sem, vmem_limit_bytes=vmem)


def _mm_body(*refs, nk, nk1, a_silu, has_bias, col_block0, w_cols_if_overhang, w_t, out_scale):
    refs = list(refs)
    a_ref = refs.pop(0)
    a2_ref = refs.pop(0) if nk1 < nk else None
    w_ref = refs.pop(0)
    b_ref = refs.pop(0) if has_bias else None
    o_ref, acc_ref = refs
    k = pl.program_id(2)

    @pl.when(k == 0)
    def _():
        acc_ref[...] = jnp.zeros_like(acc_ref)

    def accumulate(src_ref):
        a = src_ref[...]
        if a_silu:
            a = a.astype(F32)
            a = a * jax.nn.sigmoid(a)
        w = w_ref[...]
        n_axis = 0 if w_t else 1
        if w_cols_if_overhang is not None:
            tn = w.shape[n_axis]
            col = lax.broadcasted_iota(I32, w.shape, n_axis) + (pl.program_id(1) + col_block0) * tn
            w = jnp.where(col < w_cols_if_overhang, w, 0.0)
        acc_ref[...] += lax.dot_general(a.astype(BF16), w.astype(BF16),
                                        (((1,), (1 if w_t else 0,)), ((), ())),
                                        preferred_element_type=F32)

    if a2_ref is None:
        accumulate(a_ref)
    else:
        pl.when(k < nk1)(lambda: accumulate(a_ref))
        pl.when(k >= nk1)(lambda: accumulate(a2_ref))

    @pl.when(k == nk - 1)
    def _():
        r = acc_ref[...]
        if has_bias:
            r = r + b_ref[...]
        if out_scale is not None:
            n_blocks, factor = out_scale
            r = r * jnp.where(pl.program_id(1) < n_blocks, factor, 1.0)
        o_ref[...] = r.astype(o_ref.dtype)


def _matmul(a, w, n_out, out_dtype, *, a2=None, bias=None, a_silu=False, col0=0, w_t=False,
            a_rows=None, scale_cols=None, tm=1024, tn=1024, tk=1024, name="matmul"):
    m, kdim = a.shape
    row0 = 0
    if a_rows is not None:
        row0, m = a_rows
    w_cols = w.shape[0] if w_t else w.shape[1]
    tm = _tile(math.gcd(m, row0), tm, 16)
    row_block0 = row0 // tm
    tn = _tile(math.gcd(n_out, col0, scale_cols[0] if scale_cols else 0), tn)
    tk = _tile(kdim, tk)
    nk1 = kdim // tk
    nk = nk1 if a2 is None else nk1 + a2.shape[1] // tk
    col_block0 = col0 // tn
    overhang = col0 + n_out > w_cols
    out_scale = None
    if scale_cols is not None:
        assert scale_cols[0] % tn == 0
        out_scale = (scale_cols[0] // tn, scale_cols[1])
    in_specs = [pl.BlockSpec((tm, tk), lambda i, j, k: (i + row_block0, jnp.minimum(k, nk1 - 1)))]
    args = [a]
    if a2 is not None:
        assert a2.shape[1] % tk == 0
        in_specs.append(pl.BlockSpec((tm, tk), lambda i, j, k: (i, jnp.maximum(k - nk1, 0))))
        args.append(a2)
    if w_t:
        in_specs.append(pl.BlockSpec((tn, tk), lambda i, j, k: (j + col_block0, k)))
    else:
        in_specs.append(pl.BlockSpec((tk, tn), lambda i, j, k: (k, j + col_block0)))
    args.append(w)
    if bias is not None:
        in_specs.append(pl.BlockSpec((1, tn), lambda i, j, k: (0, j)))
        args.append(bias)
    return pl.pallas_call(
        functools.partial(_mm_body, nk=nk, nk1=nk1, a_silu=a_silu, has_bias=bias is not None,
                          col_block0=col_block0, w_t=w_t, out_scale=out_scale,
                          w_cols_if_overhang=w_cols if overhang else None),
        grid=(m // tm, n_out // tn, nk),
        in_specs=in_specs,
        out_specs=pl.BlockSpec((tm, tn), lambda i, j, k: (i, j)),
        out_shape=jax.ShapeDtypeStruct((m, n_out), out_dtype),
        scratch_shapes=[pltpu.VMEM((tm, tn), F32)],
        compiler_params=_params(("parallel", "parallel", "arbitrary")),
        name=name,
    )(*args)


def _modulate_body(x_ref, sh_ref, sc_ref, o_ref):
    o_ref[0] = (x_ref[0] * (1.0 + sc_ref[0]) + sh_ref[0]).astype(o_ref.dtype)


def _modulate(x, mod3, shift_chunk, scale_chunk):
    b, s, d = x.shape
    ts = _tile(s, 512, 16)
    return pl.pallas_call(
        _modulate_body,
        grid=(b, s // ts),
        in_specs=[pl.BlockSpec((1, ts, d), lambda i, j: (i, j, 0)),
                  pl.BlockSpec((1, 1, d), lambda i, j: (i, 0, shift_chunk)),
                  pl.BlockSpec((1, 1, d), lambda i, j: (i, 0, scale_chunk))],
        out_specs=pl.BlockSpec((1, ts, d), lambda i, j: (i, j, 0)),
        out_shape=jax.ShapeDtypeStruct((b, s, d), BF16),
        compiler_params=_params(("parallel", "parallel")),
        name="modulate",
    )(x, mod3, mod3)


def _tail_body(t_ref, qg_ref, kvg_ref, lng_ref, lnb_ref, bf_ref,
               cq_ref, ckv_ref, kin_ref, small_ref, cum_ref, carry_ref, *, fh, ql, kvl, w_scale):
    j = pl.program_id(1)
    ts = t_ref.shape[0]

    @pl.when(j == 0)
    def _():
        carry_ref[...] = jnp.zeros_like(carry_ref)

    cq = t_ref[:, fh:fh + ql]
    cq_ref[...] = (cq * lax.rsqrt(jnp.mean(cq * cq, axis=-1, keepdims=True) + 1e-6)
                   * qg_ref[...]).astype(cq_ref.dtype)
    ckv = t_ref[:, fh + ql:fh + ql + kvl]
    ckv_ref[...] = (ckv * lax.rsqrt(jnp.mean(ckv * ckv, axis=-1, keepdims=True) + 1e-6)
                    * kvg_ref[...]).astype(ckv_ref.dtype)
    ki = t_ref[:, fh + ql + kvl:fh + ql + kvl + IDX_DIM]
    mu = jnp.mean(ki, axis=-1, keepdims=True)
    kc = ki - mu
    var = jnp.mean(kc * kc, axis=-1, keepdims=True)
    kin_ref[...] = (kc * lax.rsqrt(var + 1e-5) * lng_ref[...] + lnb_ref[...]).astype(kin_ref.dtype)

    small_ref[...] = t_ref[:, ql + kvl + IDX_DIM:ql + kvl + IDX_DIM + LANES] * w_scale
    z = t_ref[:, 0:LANES] + bf_ref[...]
    log_f = jnp.minimum(z, 0.0) - jnp.log(1.0 + jnp.exp(-jnp.abs(z)))
    row = lax.broadcasted_iota(I32, (ts, ts), 0)
    col = lax.broadcasted_iota(I32, (ts, ts), 1)
    tri = jnp.where(col <= row, 1.0, 0.0).astype(F32)
    incl = jnp.dot(tri, log_f, preferred_element_type=F32,
                   precision=lax.Precision.HIGHEST) + carry_ref[0:1, :]
    carry_ref[0:1, :] = incl[ts - 1:ts, :]
    cum_ref[...] = incl * (-LOG2E)


def _tail_prep(tail, b, s, ql, kvl, fh, ih, q_norm_g, kv_norm_g, ln_g, ln_b, b_forget):
    n = tail.shape[0]
    assert ql % LANES == 0 and kvl % LANES == 0
    ts = _tile(s, 256)
    nj = s // ts
    bf_pad = jnp.zeros((1, LANES), F32).at[0, :fh].set(b_forget)
    w_scale = (ih ** -0.5) * (IDX_DIM ** -0.5)
    row = lambda i, j: (i * nj + j, 0)
    const = lambda i, j: (0, 0)
    return pl.pallas_call(
        functools.partial(_tail_body, fh=fh, ql=ql, kvl=kvl, w_scale=w_scale),
        grid=(b, nj),
        in_specs=[pl.BlockSpec((ts, tail.shape[1]), row),
                  pl.BlockSpec((1, ql), const), pl.BlockSpec((1, kvl), const),
                  pl.BlockSpec((1, IDX_DIM), const), pl.BlockSpec((1, IDX_DIM), const),
                  pl.BlockSpec((1, LANES), const)],
        out_specs=[pl.BlockSpec((ts, ql), row), pl.BlockSpec((ts, kvl), row),
                   pl.BlockSpec((ts, IDX_DIM), row), pl.BlockSpec((ts, LANES), row),
                   pl.BlockSpec((ts, LANES), row)],
        out_shape=[jax.ShapeDtypeStruct((n, ql), BF16), jax.ShapeDtypeStruct((n, kvl), BF16),
                   jax.ShapeDtypeStruct((n, IDX_DIM), BF16), jax.ShapeDtypeStruct((n, LANES), F32),
                   jax.ShapeDtypeStruct((n, LANES), F32)],
        scratch_shapes=[pltpu.VMEM((8, LANES), F32)],
        compiler_params=_params(("parallel", "arbitrary")),
        name="tail_prep",
    )(tail, q_norm_g[None], kv_norm_g[None], ln_g[None], ln_b[None], bf_pad)


_MAX_EXACT = N_BUCKETS // 2
_BUCKET_THRESHOLDS = tuple(
    int(math.ceil(_MAX_EXACT * (MAX_DISTANCE / _MAX_EXACT) ** (k / (N_BUCKETS - _MAX_EXACT)) - 1e-9))
    for k in range(1, N_BUCKETS - _MAX_EXACT))


def _bias_body(rb_ref, o_ref, *, t):
    which = pl.program_id(0)
    row = lax.broadcasted_iota(I32, (t, t), 0)
    col = lax.broadcasted_iota(I32, (t, t), 1)
    d = jnp.maximum(which * t + col - row, 0)
    large = jnp.full((t, t), _MAX_EXACT, I32)
    for thr in _BUCKET_THRESHOLDS:
        large = large + jnp.where(d >= thr, 1, 0)
    bucket = jnp.where(d < _MAX_EXACT, d, large)
    for h in range(o_ref.shape[1]):
        bias = jnp.zeros((t, t), F32)
        for bkt in range(N_BUCKETS):
            bias = jnp.where(bucket == bkt, rb_ref[bkt, h], bias)
        o_ref[0, h] = (bias - rb_ref[N_BUCKETS - 1, h]) * LOG2E


def _bias_tiles(rel_bias, nh, t):
    assert 2 * t - (t - 1) >= _BUCKET_THRESHOLDS[-1], "key tiles two or more away are all last-bucket"
    return pl.pallas_call(
        functools.partial(_bias_body, t=t),
        grid=(2,),
        in_specs=[pl.BlockSpec(memory_space=pltpu.SMEM)],
        out_specs=pl.BlockSpec((1, nh, t, t), lambda w: (w, 0, 0, 0)),
        out_shape=jax.ShapeDtypeStruct((2, nh, t, t), F32),
        compiler_params=_params(("parallel",)),
        name="bias_tiles",
    )(rel_bias)


ATTN_QCOLS = 128


def _attn_body(qi_ref, kj_ref, *refs, nh, t, mode):
    if mode == "fox":
        q_ref, k_ref, vt_ref, g_ref, cum_ref, o_ref, m_ref, l_ref, acc_ref, neg_ref = refs
    else:
        q_ref, k_ref, vt_ref, g_ref, mask_ref, bt_ref, o_ref, m_ref, l_ref, acc_ref, neg_ref = refs
    p = pl.program_id(1)
    qi = qi_ref[p]
    kj = kj_ref[p]
    nblk = t // ATTN_QCOLS

    @pl.when(kj == 0)
    def _():
        m_ref[...] = jnp.full_like(m_ref, NEG_BIG)
        l_ref[...] = jnp.zeros_like(l_ref)
        acc_ref[...] = jnp.zeros_like(acc_ref)

    def heads(addend):
        for h in range(nh):
            sl = slice(h * HEAD_DIM, (h + 1) * HEAD_DIM)
            for r in range(nblk):
                qs = slice(r * ATTN_QCOLS, (r + 1) * ATTN_QCOLS)
                idx = h * nblk + r
                s = lax.dot_general(k_ref[:, sl], q_ref[qs, sl], (((1,), (1,)), ((), ())),
                                    preferred_element_type=F32)
                s = s + addend(h, qs)
                m_prev = m_ref[idx]
                m_next = jnp.maximum(m_prev, jnp.max(s, axis=0, keepdims=True))
                alpha = jnp.exp2(m_prev - m_next)
                pr = jnp.exp2(s - m_next)
                l_ref[idx] = alpha * l_ref[idx] + jnp.sum(pr, axis=0, keepdims=True)
                acc_ref[idx] = acc_ref[idx] * alpha + jnp.dot(
                    vt_ref[sl, :], pr.astype(BF16), preferred_element_type=F32)
                m_ref[idx] = m_next

    key_i = lax.broadcasted_iota(I32, (t, t), 0)
    qry_i = lax.broadcasted_iota(I32, (t, t), 1)
    if mode == "fox":
        @pl.when(kj == qi)
        def _():
            neg_ref[...] = jnp.where(key_i <= qry_i, 0.0, NEG_BIG).astype(F32)
            heads(lambda h, qs: neg_ref[:, qs] + cum_ref[:, h:h + 1])

        @pl.when(kj != qi)
        def _():
            heads(lambda h, qs: cum_ref[:, h:h + 1])
    else:
        neg_ref[...] = (mask_ref[0, 0].astype(F32) - 1.0) * (-NEG_BIG)

        @pl.when(qi - kj >= 2)
        def _():
            heads(lambda h, qs: neg_ref[:, qs])

        @pl.when(qi - kj < 2)
        def _():
            near = qi - kj
            heads(lambda h, qs: neg_ref[:, qs] + bt_ref[near, h, :, qs])

    @pl.when(kj == qi)
    def _():
        for r in range(nblk):
            qs = slice(r * ATTN_QCOLS, (r + 1) * ATTN_QCOLS)
            ss = jnp.zeros((1, ATTN_QCOLS), F32)
            for h in range(nh):
                o = acc_ref[h * nblk + r] / l_ref[h * nblk + r]
                acc_ref[h * nblk + r] = o
                ss = ss + jnp.sum(o * o, axis=0, keepdims=True)
            rinv = lax.rsqrt(ss / (nh * HEAD_DIM) + 1e-6)
            for h in range(nh):
                sl = slice(h * HEAD_DIM, (h + 1) * HEAD_DIM)
                o_ref[qs, sl] = ((acc_ref[h * nblk + r] * rinv).T * g_ref[:, sl]).astype(o_ref.dtype)


def _attention(mode, b, s, nh, t, q_src, k_src, vt, gain, *, cum=None, mask=None, bias_tiles=None):
    nq = s // t
    pairs = [(i, j) for i in range(nq) for j in range(i + 1)]
    qi_arr = jnp.asarray(np.array([pq for pq, _ in pairs], np.int32))
    kj_arr = jnp.asarray(np.array([pk for _, pk in pairs], np.int32))
    hw = nh * HEAD_DIM
    n = b * s
    nblk = t // ATTN_QCOLS

    def qmap(cb):
        return lambda bi, p, qi, kj: (bi * nq + qi[p], cb)

    def kmap(cb):
        return lambda bi, p, qi, kj: (bi * nq + kj[p], cb)

    in_specs = [pl.BlockSpec((t, hw), qmap(q_src[1])),
                pl.BlockSpec((t, hw), kmap(k_src[1])),
                pl.BlockSpec((hw, t), lambda bi, p, qi, kj: (0, bi * nq + kj[p])),
                pl.BlockSpec((1, hw), lambda bi, p, qi, kj: (0, 0))]
    args = [q_src[0], k_src[0], vt, gain[None]]
    if mode == "fox":
        in_specs.append(pl.BlockSpec((t, LANES), kmap(0)))
        args.append(cum)
    else:
        in_specs += [pl.BlockSpec((1, 1, t, t), lambda bi, p, qi, kj: (bi, kj[p], 0, qi[p])),
                     pl.BlockSpec(memory_space=pltpu.VMEM)]
        args += [mask, bias_tiles]
    grid_spec = pltpu.PrefetchScalarGridSpec(
        num_scalar_prefetch=2,
        grid=(b, len(pairs)),
        in_specs=in_specs,
        out_specs=pl.BlockSpec((t, hw), lambda bi, p, qi, kj: (bi * nq + qi[p], 0)),
        scratch_shapes=[pltpu.VMEM((nh * nblk, 1, ATTN_QCOLS), F32),
                        pltpu.VMEM((nh * nblk, 1, ATTN_QCOLS), F32),
                        pltpu.VMEM((nh * nblk, HEAD_DIM, ATTN_QCOLS), F32),
                        pltpu.VMEM((t, t), F32)],
    )
    return pl.pallas_call(
        functools.partial(_attn_body, nh=nh, t=t, mode=mode),
        grid_spec=grid_spec,
        out_shape=jax.ShapeDtypeStruct((n, hw), BF16),
        compiler_params=_params(("parallel", "arbitrary")),
        name=mode + "_attention",
    )(qi_arr, kj_arr, *args)


IDX_QCOLS = 128
SUBLANES = 8


def _float_key(x):
    bits = lax.bitcast_convert_type(x, I32)
    return bits ^ (lax.shift_right_arithmetic(bits, 31) & 0x7FFFFFFF)


def _indexer_body(q_ref, k_ref, w_ref, o_ref, key_ref, *, ih, w_lane0, tq, ck, nchunk, k_sel):
    i = pl.program_id(1)
    n_valid = ((i + 1) * tq + ck - 1) // ck
    key_i = lax.broadcasted_iota(I32, (ck, tq), 0)
    qry_i = lax.broadcasted_iota(I32, (ck, tq), 1) + i * tq
    w_t = w_ref[...].T

    def score_chunk(c, carry):
        kc = k_ref[0, pl.ds(pl.multiple_of(c * ck, ck), ck), :]
        for r in range(tq // IDX_QCOLS):
            qs = slice(r * IDX_QCOLS, (r + 1) * IDX_QCOLS)
            acc = jnp.zeros((ck, IDX_QCOLS), F32)
            for h in range(ih):
                sh = lax.dot_general(kc, q_ref[qs, h * IDX_DIM:(h + 1) * IDX_DIM],
                                     (((1,), (1,)), ((), ())), preferred_element_type=F32)
                acc = acc + w_t[w_lane0 + h:w_lane0 + h + 1, qs] * jnp.maximum(sh, 0.0)
            causal = (lax.broadcasted_iota(I32, (ck, IDX_QCOLS), 0) + c * ck
                      <= lax.broadcasted_iota(I32, (ck, IDX_QCOLS), 1) + (i * tq + r * IDX_QCOLS))
            key_ref[c, :, qs] = _float_key(jnp.where(causal, acc, -jnp.inf))
        return carry

    lax.fori_loop(0, n_valid, score_chunk, 0)

    def count(pred_fn, level):
        def body(c, acc):
            ind = jnp.where(pred_fn(key_ref[c], level), 1, 0)
            return acc + jnp.sum(ind.reshape(ck // SUBLANES, SUBLANES, tq), axis=0)
        acc = lax.fori_loop(0, n_valid, body, jnp.zeros((SUBLANES, tq), I32))
        return jnp.sum(acc, axis=0, keepdims=True)

    ge = lambda kv, lv: kv >= lv
    t0 = jnp.full((1, tq), INT_MIN, I32)
    zero = jnp.zeros((1, tq), I32)
    thr = jnp.where(count(ge, zero) >= k_sel, zero, t0)

    def bit_step(bi, thr):
        cand = thr | lax.shift_left(jnp.int32(1), 30 - bi)
        return jnp.where(count(ge, cand) >= k_sel, cand, thr)

    thr = lax.fori_loop(0, 31, bit_step, thr)
    need = (k_sel - count(lambda kv, lv: kv > lv, thr)).astype(F32)

    lrow = lax.broadcasted_iota(I32, (ck, ck), 0)
    lcol = lax.broadcasted_iota(I32, (ck, ck), 1)
    lower = jnp.where(lcol <= lrow, 1.0, 0.0).astype(BF16)
    ones = jnp.ones((SUBLANES, ck), BF16)

    def select_chunk(c, offset):
        kv = key_ref[c]
        eq = kv == thr
        eqb = jnp.where(eq, 1.0, 0.0).astype(BF16)
        rank = jnp.dot(lower, eqb, preferred_element_type=F32) + offset
        take = (kv > thr) | (eq & (rank <= need))
        take = take & (key_i + c * ck <= qry_i)
        o_ref[0, c] = jnp.where(take, 1, 0).astype(jnp.int8)
        return offset + jnp.dot(ones, eqb, preferred_element_type=F32)[0:1, :]

    lax.fori_loop(0, n_valid, select_chunk, jnp.zeros((1, tq), F32))

    def zero_chunk(c, carry):
        o_ref[0, c] = jnp.zeros((ck, tq), jnp.int8)
        return carry

    lax.fori_loop(n_valid, nchunk, zero_chunk, 0)


def _indexer(q_idx, k_in, small, b, s, ih, w_lane0, ck, k_sel):
    tq = _tile(s, 256, IDX_QCOLS)
    nq = s // tq
    nchunk = s // ck
    k3 = k_in.reshape(b, s, IDX_DIM)
    return pl.pallas_call(
        functools.partial(_indexer_body, ih=ih, w_lane0=w_lane0, tq=tq, ck=ck, nchunk=nchunk,
                          k_sel=k_sel),
        grid=(b, nq),
        in_specs=[pl.BlockSpec((tq, ih * IDX_DIM), lambda bi, i: (bi * nq + i, 0)),
                  pl.BlockSpec((1, s, IDX_DIM), lambda bi, i: (bi, 0, 0)),
                  pl.BlockSpec((tq, LANES), lambda bi, i: (bi * nq + i, 0))],
        out_specs=pl.BlockSpec((1, nchunk, ck, tq), lambda bi, i: (bi, 0, 0, i)),
        out_shape=jax.ShapeDtypeStruct((b, nchunk, ck, s), jnp.int8),
        scratch_shapes=[pltpu.VMEM((nchunk, ck, tq), I32)],
        compiler_params=_params(("parallel", "parallel")),
        name="indexer_topk",
    )(q_idx, k3, small)


def _layer_norm(h, g, bvec):
    mu = jnp.mean(h, axis=-1, keepdims=True)
    hc = h - mu
    var = jnp.mean(hc * hc, axis=-1, keepdims=True)
    return hc * lax.rsqrt(var + 1e-5) * g + bvec


def _router_body(x_ref, mix_ref, ga_ref, shm_ref, scm_ref, lg_ref, lb_ref, wr_ref, br_ref,
                 x1_ref, up_ref, eidx_ref, rank_ref, gate_ref, cnt_ref, carry_ref,
                 *, alpha_res, ne, nsteps):
    i = pl.program_id(0)
    tm, d = x_ref.shape

    @pl.when(i == 0)
    def _():
        carry_ref[...] = jnp.zeros_like(carry_ref)

    x1 = _layer_norm(alpha_res * x_ref[...] + ga_ref[0] * mix_ref[...].astype(F32),
                     lg_ref[...], lb_ref[...])
    x1_ref[...] = x1
    u = x1 * (1.0 + scm_ref[0]) + shm_ref[0]
    up_ref[...] = _pack_bf16_pair(u[:, :d // 2], u[:, d // 2:])

    def split(a):
        hi = a.astype(BF16)
        return hi, (a - hi.astype(F32)).astype(BF16)
    u_hi, u_lo = split(u)
    w_hi, w_lo = split(wr_ref[...])
    logits = (jnp.dot(u_hi, w_hi, preferred_element_type=F32)
              + jnp.dot(u_hi, w_lo, preferred_element_type=F32)
              + jnp.dot(u_lo, w_hi, preferred_element_type=F32)) + br_ref[...]
    lane = lax.broadcasted_iota(I32, (tm, LANES), 1)
    lane_f = lane.astype(F32)
    work = jnp.where(lane < ne, logits, -jnp.inf)
    sel = jnp.zeros((tm, LANES), F32)
    idxs, vals, hits = [], [], []
    for _ in range(TOP_K):
        mx = jnp.max(work, axis=1, keepdims=True)
        ik = jnp.min(jnp.where(work == mx, lane_f, float(LANES)), axis=1, keepdims=True)
        hit = lane_f == ik
        sel = jnp.where(hit, 1.0, sel)
        work = jnp.where(hit, -jnp.inf, work)
        idxs.append(ik.astype(I32))
        vals.append(mx)
        hits.append(hit)
    exps = [jnp.exp(v - vals[0]) for v in vals]
    denom = exps[0] + exps[1] + exps[2] + exps[3]

    row = lax.broadcasted_iota(I32, (tm, tm), 0)
    col = lax.broadcasted_iota(I32, (tm, tm), 1)
    tri = jnp.where(col < row, 1.0, 0.0).astype(BF16)
    before = jnp.dot(tri, sel.astype(BF16), preferred_element_type=F32) + carry_ref[0:1, :]
    total = carry_ref[0:1, :] + jnp.sum(sel, axis=0, keepdims=True)
    carry_ref[0:1, :] = total

    eidx = jnp.zeros((tm, LANES), I32)
    rank = jnp.zeros((tm, LANES), I32)
    gate = jnp.zeros((tm, LANES), F32)
    for kk in range(TOP_K):
        rk = jnp.sum(jnp.where(hits[kk], before, 0.0), axis=1, keepdims=True)
        eidx = jnp.where(lane == kk, idxs[kk], eidx)
        rank = jnp.where(lane == kk, rk.astype(I32), rank)
        gate = jnp.where(lane == kk, exps[kk] / denom, gate)
    eidx_ref[...] = eidx
    rank_ref[...] = rank
    gate_ref[...] = gate
    cnt_ref[...] = jnp.broadcast_to(total, cnt_ref.shape).astype(I32)


def _router(x2d, mix, mod3, ln_g, ln_b, w_router_pad, b_router_pad, s, ne, alpha_res):
    n, d = x2d.shape
    tm = _tile(s, 256, 16)
    nsteps = n // tm
    per_b = s // tm
    row = lambda i: (i, 0)
    const = lambda i: (0, 0)
    modspec = lambda chunk: pl.BlockSpec((1, 1, d), lambda i: (i // per_b, 0, chunk))
    return pl.pallas_call(
        functools.partial(_router_body, alpha_res=alpha_res, ne=ne, nsteps=nsteps),
        grid=(nsteps,),
        in_specs=[pl.BlockSpec((tm, d), row), pl.BlockSpec((tm, d), row),
                  modspec(2), modspec(3), modspec(4),
                  pl.BlockSpec((1, d), const), pl.BlockSpec((1, d), const),
                  pl.BlockSpec((d, LANES), const), pl.BlockSpec((1, LANES), const)],
        out_specs=[pl.BlockSpec((tm, d), row), pl.BlockSpec((tm, d // 2), row),
                   pl.BlockSpec((tm, LANES), row), pl.BlockSpec((tm, LANES), row),
                   pl.BlockSpec((tm, LANES), row), pl.BlockSpec((8, LANES), const)],
        out_shape=[jax.ShapeDtypeStruct((n, d), F32), jax.ShapeDtypeStruct((n, d // 2), I32),
                   jax.ShapeDtypeStruct((n, LANES), I32), jax.ShapeDtypeStruct((n, LANES), I32),
                   jax.ShapeDtypeStruct((n, LANES), F32), jax.ShapeDtypeStruct((8, LANES), I32)],
        scratch_shapes=[pltpu.VMEM((8, LANES), F32)],
        compiler_params=_params(("arbitrary",)),
        name="ln1_router",
    )(x2d, mix, mod3, mod3, mod3, ln_g[None], ln_b[None], w_router_pad, b_router_pad)


def _dispatch_body(dest_ref, u_ref, xs_ref, sem):
    i = pl.program_id(0)
    tc = u_ref.shape[0]

    def start(j, cc):
        for kk in range(TOP_K):
            d = dest_ref[(i * tc + j) * TOP_K + kk]
            pltpu.make_async_copy(u_ref.at[pl.ds(j, 1)], xs_ref.at[pl.ds(d, 1)], sem.at[0]).start()
        return cc

    lax.fori_loop(0, tc, start, 0, unroll=DMA_ISSUE_UNROLL)
    for _ in range(TOP_K):
        pltpu.make_async_copy(u_ref, xs_ref.at[pl.ds(0, tc)], sem.at[0]).wait()


def _dispatch(dest_flat, u_packed, n_slots):
    n, dw = u_packed.shape
    tc = _tile(n, 256, 8)
    grid_spec = pltpu.PrefetchScalarGridSpec(
        num_scalar_prefetch=1, grid=(n // tc,),
        in_specs=[pl.BlockSpec((tc, dw), lambda i, dr: (i, 0))],
        out_specs=pl.BlockSpec(memory_space=pl.ANY),
        scratch_shapes=[pltpu.SemaphoreType.DMA((1,))])
    return pl.pallas_call(
        _dispatch_body,
        grid_spec=grid_spec,
        out_shape=jax.ShapeDtypeStruct((n_slots, dw), I32),
        compiler_params=pltpu.CompilerParams(dimension_semantics=("arbitrary",),
                                             has_side_effects=True),
        name="moe_dispatch",
    )(dest_flat, u_packed)


def _ffn_body(te_ref, nsub_ref, nused_ref, x_ref, w1g_ref, w1u_ref, b1g_ref, b1u_ref,
              w2lo_ref, w2hi_ref, b2lo_ref, b2hi_ref, o_ref, h_ref, xbuf, xsem, *, nf, tf, ts, nr):
    i = pl.program_id(0)
    st = pl.program_id(1)
    _, tm, dh = xbuf.shape
    nsub = nsub_ref[i]
    slot = i % 2

    def x_copy(tile, to_slot):
        return pltpu.make_async_copy(x_ref.at[pl.ds(pl.multiple_of(tile * tm, tm), tm)],
                                     xbuf.at[to_slot], xsem.at[to_slot])

    @pl.when(jnp.logical_and(i == 0, st == 0))
    def _():
        x_copy(0, 0).start()

    @pl.when(st == 0)
    def _():
        x_copy(i, slot).wait()

    @pl.when(jnp.logical_and(st == nf, i + 1 < nused_ref[0]))
    def _():
        x_copy(i + 1, 1 - slot).start()

    def up_step(r0, rows):
        lo, hi = _unpack_bf16_pair(xbuf[slot, r0:r0 + rows, :])
        lo, hi = lo.astype(BF16), hi.astype(BF16)

        def proj(w_ref, b_ref):
            return (jnp.dot(lo, w_ref[0, 0:dh, :].astype(BF16), preferred_element_type=F32)
                    + jnp.dot(hi, w_ref[0, dh:2 * dh, :].astype(BF16), preferred_element_type=F32)
                    + b_ref[0])
        gate = jnp.minimum(proj(w1g_ref, b1g_ref), SWIGLU_LIMIT)
        up = jnp.clip(proj(w1u_ref, b1u_ref), -SWIGLU_LIMIT, SWIGLU_LIMIT)
        act = (up + 1.0) * gate * jax.nn.sigmoid(SWIGLU_ALPHA * gate)
        h_ref[st, r0:r0 + rows, :] = act.astype(BF16)

    def down_step(r0, rows, block_rows):
        def proj(w_ref, b_ref):
            acc = jnp.broadcast_to(b_ref[0], (rows, o_ref.shape[1])).astype(F32)
            for f in range(nf):
                acc = acc + jnp.dot(h_ref[f, r0:r0 + rows, :],
                                    w_ref[0, f * tf:(f + 1) * tf, :].astype(BF16),
                                    preferred_element_type=F32)
            return acc
        if rows:
            o_ref[r0:r0 + rows, :] = _pack_bf16_pair(proj(w2lo_ref, b2lo_ref),
                                                     proj(w2hi_ref, b2hi_ref))
        if rows < block_rows:
            o_ref[r0 + rows:r0 + block_rows, :] = jnp.zeros((block_rows - rows, o_ref.shape[1]),
                                                            I32)

    for blk in range(tm // (nr * ts)):
        r0 = blk * nr * ts
        nsub_b = jnp.clip(nsub - blk * nr, 0, nr)
        for r in range(nr + 1):
            if r:
                @pl.when(jnp.logical_and(nsub_b == r, st < nf))
                def _(r=r, r0=r0):
                    up_step(r0, r * ts)

            @pl.when(jnp.logical_and(nsub_b == r, st >= nf))
            def _(r=r, r0=r0):
                down_step(r0, r * ts, nr * ts)


def _expert_ffn(xs, w1, b1, w2, b2, tile_e, tile_nsub, n_used, *, tm, ts, n_tiles):
    ne, d, de2 = w1.shape
    de = de2 // 2
    dh = d // 2
    tf = _tile(de, 256)
    td = _tile(dh, 256)
    nf, nd = de // tf, dh // td
    nstep = nf + nd
    b1r = b1.reshape(ne, 1, de2)
    b2r = b2.reshape(ne, 1, d)

    def tile_idx(i, nu):
        return jnp.maximum(jnp.minimum(i, nu[0] - 1), 0)

    def step_idx(i, s, nu):
        return jnp.where(i < nu[0], s, nstep - 1)

    def up_map(off):
        return lambda i, s, te, ns, nu: (te[tile_idx(i, nu)], 0,
                                         off + jnp.minimum(step_idx(i, s, nu), nf - 1))

    def down_map(off):
        return lambda i, s, te, ns, nu: (te[tile_idx(i, nu)], 0,
                                         off + jnp.maximum(step_idx(i, s, nu) - nf, 0))

    grid_spec = pltpu.PrefetchScalarGridSpec(
        num_scalar_prefetch=3,
        grid=(n_used[0], nstep),
        in_specs=[pl.BlockSpec(memory_space=pl.ANY),
                  pl.BlockSpec((1, d, tf), up_map(0)), pl.BlockSpec((1, d, tf), up_map(nf)),
                  pl.BlockSpec((1, 1, tf), up_map(0)), pl.BlockSpec((1, 1, tf), up_map(nf)),
                  pl.BlockSpec((1, de, td), down_map(0)), pl.BlockSpec((1, de, td), down_map(nd)),
                  pl.BlockSpec((1, 1, td), down_map(0)), pl.BlockSpec((1, 1, td), down_map(nd))],
        out_specs=pl.BlockSpec((tm, td), lambda i, s, te, ns, nu: (
            tile_idx(i, nu), jnp.maximum(step_idx(i, s, nu) - nf, 0))),
        scratch_shapes=[pltpu.VMEM((nf, tm, tf), BF16), pltpu.VMEM((2, tm, dh), I32),
                        pltpu.SemaphoreType.DMA((2,))],
    )
    return pl.pallas_call(
        functools.partial(_ffn_body, nf=nf, tf=tf, ts=ts, nr=MOE_SUBTILES_PER_BLOCK),
        grid_spec=grid_spec,
        out_shape=jax.ShapeDtypeStruct((n_tiles * tm, dh), I32),
        compiler_params=_params(("arbitrary", "arbitrary")),
        name="expert_ffn",
    )(tile_e, tile_nsub, n_used, xs, w1, w1, b1r, b1r, w2, w2, b2r, b2r)


def _combine_body(dest_ref, y_ref, gate_ref, x1_ref, gm_ref, lg_ref, lb_ref, o_ref, ybuf, sem,
                  *, alpha_res):
    i = pl.program_id(0)
    nsteps = pl.num_programs(0)
    tc = x1_ref.shape[0]

    def gather_tile(tile, slot):
        def start(j, cc):
            for kk in range(TOP_K):
                d = dest_ref[(tile * tc + j) * TOP_K + kk]
                pltpu.make_async_copy(y_ref.at[pl.ds(d, 1)], ybuf.at[slot, kk, pl.ds(j, 1)],
                                      sem.at[slot]).start()
            return cc
        lax.fori_loop(0, tc, start, 0, unroll=DMA_ISSUE_UNROLL)

    slot = i % 2

    @pl.when(i == 0)
    def _():
        gather_tile(0, 0)

    @pl.when(i + 1 < nsteps)
    def _():
        gather_tile(i + 1, 1 - slot)

    for kk in range(TOP_K):
        pltpu.make_async_copy(y_ref.at[pl.ds(0, tc)], ybuf.at[slot, kk], sem.at[slot]).wait()
    g = gate_ref[...]
    y_lo = jnp.zeros((tc, ybuf.shape[3]), F32)
    y_hi = jnp.zeros((tc, ybuf.shape[3]), F32)
    for kk in range(TOP_K):
        lo, hi = _unpack_bf16_pair(ybuf[slot, kk])
        y_lo = y_lo + g[:, kk:kk + 1] * lo
        y_hi = y_hi + g[:, kk:kk + 1] * hi
    y = jnp.concatenate([y_lo, y_hi], axis=1)
    o_ref[...] = _layer_norm(alpha_res * x1_ref[...] + gm_ref[0] * y, lg_ref[...], lb_ref[...])


def _combine(dest_flat, y_sorted, gate, x1, mod3, ln_g, ln_b, s, alpha_res):
    n, d = x1.shape
    tc = _tile(s, 128, 8)
    per_b = s // tc
    row = lambda i, dr: (i, 0)
    const = lambda i, dr: (0, 0)
    grid_spec = pltpu.PrefetchScalarGridSpec(
        num_scalar_prefetch=1, grid=(n // tc,),
        in_specs=[pl.BlockSpec(memory_space=pl.ANY),
                  pl.BlockSpec((tc, LANES), row), pl.BlockSpec((tc, d), row),
                  pl.BlockSpec((1, 1, d), lambda i, dr: (i // per_b, 0, 5)),
                  pl.BlockSpec((1, d), const), pl.BlockSpec((1, d), const)],
        out_specs=pl.BlockSpec((tc, d), row),
        scratch_shapes=[pltpu.VMEM((2, TOP_K, tc, d // 2), I32), pltpu.SemaphoreType.DMA((2,))])
    return pl.pallas_call(
        functools.partial(_combine_body, alpha_res=alpha_res),
        grid_spec=grid_spec,
        out_shape=jax.ShapeDtypeStruct((n, d), F32),
        compiler_params=_params(("arbitrary",)),
        name="moe_combine",
    )(dest_flat, y_sorted, gate, x1, mod3, ln_g[None], ln_b[None])


MOE_SUBTILES_PER_BLOCK = 5
MOE_BLOCKS_PER_TILE = 1


def _moe_tiles(n_tok, n_exp):
    mean_load = n_tok * TOP_K // n_exp
    ts = 16
    while ts * 2 <= min(256, mean_load // (4 * MOE_BLOCKS_PER_TILE)):
        ts *= 2
    return ts, MOE_BLOCKS_PER_TILE * MOE_SUBTILES_PER_BLOCK * ts


def _layer(x, c_pad, w_ada, b_ada, w_in, b_forget, q_norm_g, kv_norm_g, kidx_ln_g, kidx_ln_b,
           w_uq, w_uk, w_uv, w_iq, fox_out_g, dsa_out_g, w_o, ln1_g, ln1_b,
           w_router, b_router, w1, b1, w2, b2, ln2_g, ln2_b, rel_bias, alpha_res):
    b, s, d = x.shape
    n = b * s
    fh = b_forget.shape[0]
    ql = q_norm_g.shape[0]
    kvl = kv_norm_g.shape[0]
    dh = w_uk.shape[1]
    ih = w_iq.shape[1] // IDX_DIM
    ne = w_router.shape[1]
    fw = fh * HEAD_DIM
    dw = dh * HEAD_DIM
    assert fw == dw, "head groups share the attention kernel's column-block width"
    assert fh + ih <= LANES and ne <= LANES

    mod = _matmul(c_pad, w_ada, 6 * d, F32, bias=b_ada[None], a_silu=True,
                  tm=16, tn=512, tk=d, name="adaln")
    mod3 = mod[:b].reshape(b, 1, 6 * d)
    u = _modulate(x, mod3, 0, 1).reshape(n, d)

    w_in_t = jnp.swapaxes(w_in, 0, 1)
    q_scale = HEAD_DIM ** -0.5 * LOG2E
    qk = _matmul(u, w_in_t, 2 * fw, BF16, w_t=True, scale_cols=(fw, q_scale), **WIDE_TILES,
                 name="proj_qk")
    tail_w = -(-(ql + kvl + IDX_DIM + fh + ih) // LANES) * LANES
    tail = _matmul(u, w_in_t, tail_w, F32, w_t=True, col0=3 * fw, tm=2048, name="proj_tail")
    cqn, ckvn, kin, small, cum = _tail_prep(tail, b, s, ql, kvl, fh, ih, q_norm_g, kv_norm_g,
                                            kidx_ln_g, kidx_ln_b, b_forget)
    v_t = _matmul(w_in_t, u, n, BF16, w_t=True, a_rows=(2 * fw, fw), **WIDE_TILES,
                  name="proj_vt")

    t = 256 if s % 256 == 0 else 128
    fox = _attention("fox", b, s, fh, t, (qk, 0), (qk, 1), v_t, fox_out_g, cum=cum)

    q_d = _matmul(cqn, w_uq, dw, BF16, scale_cols=(dw, q_scale), **WIDE_TILES, name="proj_qd")
    q_i = _matmul(cqn, w_iq, ih * IDX_DIM, BF16, **WIDE_TILES, name="proj_qi")
    k_d = _matmul(ckvn, w_uk.reshape(kvl, dw), dw, BF16, **WIDE_TILES, name="expand_k")
    vd_t = _matmul(jnp.swapaxes(w_uv.reshape(kvl, dw), 0, 1), ckvn, n, BF16, w_t=True,
                   **WIDE_TILES, name="expand_vt")
    k_sel = min(TOPK_MAX, s // 4)
    mask = _indexer(q_i, kin, small, b, s, ih, fh, t, k_sel)
    bias_t = _bias_tiles(rel_bias, dh, t)
    dsa = _attention("dsa", b, s, dh, t, (q_d, 0), (k_d, 0), vd_t, dsa_out_g,
                     mask=mask, bias_tiles=bias_t)

    mix = _matmul(fox, w_o, d, BF16, a2=dsa, **WIDE_TILES, name="proj_out")

    wr_pad = jnp.zeros((d, LANES), F32).at[:, :ne].set(w_router)
    br_pad = jnp.zeros((1, LANES), F32).at[0, :ne].set(b_router)
    x1, u_packed, eidx, rank, gate, cnt = _router(x.reshape(n, d), mix, mod3, ln1_g, ln1_b,
                                                  wr_pad, br_pad, s, ne, alpha_res)
    ts, tm = _moe_tiles(n, ne)
    n_tiles = -(-(n * TOP_K) // tm) + ne
    counts = cnt[0, :ne]
    ntile = (counts + tm - 1) // tm
    tend = jnp.cumsum(ntile)
    tstart = tend - ntile
    n_used = tend[-1]
    dest = (tstart * tm)[eidx[:, :TOP_K]] + rank[:, :TOP_K]
    dest_flat = dest.reshape(n * TOP_K).astype(I32)
    tid = jnp.minimum(jnp.arange(n_tiles, dtype=I32), n_used - 1)
    tile_e = jnp.minimum(jnp.searchsorted(tend, tid, side="right"), ne - 1).astype(I32)
    valid = jnp.clip(counts[tile_e] - (tid - tstart[tile_e]) * tm, 0, tm)
    valid = jnp.where(jnp.arange(n_tiles) < n_used, valid, 0)
    tile_nsub = ((valid + ts - 1) // ts).astype(I32)
    xs = _dispatch(dest_flat, u_packed, n_tiles * tm)
    y_sorted = _expert_ffn(xs, w1, b1, w2, b2, tile_e, tile_nsub,
                           n_used.reshape(1).astype(I32), tm=tm, ts=ts, n_tiles=n_tiles)
    out = _combine(dest_flat, y_sorted, gate, x1, mod3, ln2_g, ln2_b, s, alpha_res)
    return out.reshape(b, s, d)


def kernel(x, c, w_ada, b_ada, w_in, b_forget, q_norm_g, kv_norm_g, kidx_ln_g, kidx_ln_b, w_uq, w_uk, w_uv, w_iq, fox_out_g, dsa_out_g, w_o, ln1_g, ln1_b, w_router, b_router, w1, b1, w2, b2, ln2_g, ln2_b, rel_bias):
    depth = w_ada.shape[0]
    alpha_res = (2 * depth) ** 0.25
    b, d = c.shape
    c_pad = jnp.zeros((16, d), F32).at[:b].set(c)
    for l in range(depth):
        x = _layer(x, c_pad, w_ada[l], b_ada[l], w_in[l], b_forget[l], q_norm_g[l], kv_norm_g[l],
                   kidx_ln_g[l], kidx_ln_b[l], w_uq[l], w_uk[l], w_uv[l], w_iq[l], fox_out_g[l],
                   dsa_out_g[l], w_o[l], ln1_g[l], ln1_b[l], w_router[l], b_router[l],
                   w1[l], b1[l], w2[l], b2[l], ln2_g[l], ln2_b[l], rel_bias, alpha_res)
    return x
```

```python
import functools
import math

import numpy as np
import jax
import jax.numpy as jnp
from jax import lax
from jax.experimental import pallas as pl
from jax.experimental.pallas import tpu as pltpu

F32 = jnp.float32
BF16 = jnp.bfloat16
I32 = jnp.int32

LANES = 128
HEAD_DIM = 128
IDX_DIM = 128
TOPK_MAX = 256
N_BUCKETS = 32
MAX_DISTANCE = 128
TOP_K = 4
SWIGLU_LIMIT = 7.0
SWIGLU_ALPHA = 1.702
NEG_BIG = -1e30
LOG2E = math.log2(math.e)
VMEM_LIMIT = 56 * 1024 * 1024
INT_MIN = -2 ** 31
DMA_ISSUE_UNROLL = 8
WIDE_TILES = dict(tm=2048, tn=2048, tk=512)


def _tile(n, pref, unit=LANES):
    if n <= pref:
        return n
    t = (pref // unit) * unit
    while t > unit and n % t:
        t -= unit
    assert n % t == 0, (n, pref)
    return t


def _pack_bf16_pair(lo, hi):
    lo_bits = lax.bitcast_convert_type(lo.astype(BF16).astype(F32), I32)
    hi_bits = lax.bitcast_convert_type(hi.astype(BF16).astype(F32), I32)
    return lax.shift_right_logical(lo_bits, 16) | (hi_bits & jnp.int32(-65536))


def _unpack_bf16_pair(words):
    lo = lax.bitcast_convert_type(lax.shift_left(words, 16), F32)
    hi = lax.bitcast_convert_type(words & jnp.int32(-65536), F32)
    return lo, hi


def _params(sem, vmem=VMEM_LIMIT):
    return pltpu.CompilerParams(dimension_semantics=sem, vmem_limit_bytes=vmem)


def _mm_body(*refs, nk, nk1, a_silu, has_bias, col_block0, w_cols_if_overhang, w_t, out_scale):
    refs = list(refs)
    a_ref = refs.pop(0)
    a2_ref = refs.pop(0) if nk1 < nk else None
    w_ref = refs.pop(0)
    b_ref = refs.pop(0) if has_bias else None
    o_ref, acc_ref = refs
    k = pl.program_id(2)

    @pl.when(k == 0)
    def _():
        acc_ref[...] = jnp.zeros_like(acc_ref)

    def accumulate(src_ref):
        a = src_ref[...]
        if a_silu:
            a = a.astype(F32)
            a = a * jax.nn.sigmoid(a)
        w = w_ref[...]
        n_axis = 0 if w_t else 1
        if w_cols_if_overhang is not None:
            tn = w.shape[n_axis]
            col = lax.broadcasted_iota(I32, w.shape, n_axis) + (pl.program_id(1) + col_block0) * tn
            w = jnp.where(col < w_cols_if_overhang, w, 0.0)
        acc_ref[...] += lax.dot_general(a.astype(BF16), w.astype(BF16),
                                        (((1,), (1 if w_t else 0,)), ((), ())),
                                        preferred_element_type=F32)

    if a2_ref is None:
        accumulate(a_ref)
    else:
        pl.when(k < nk1)(lambda: accumulate(a_ref))
        pl.when(k >= nk1)(lambda: accumulate(a2_ref))

    @pl.when(k == nk - 1)
    def _():
        r = acc_ref[...]
        if has_bias:
            r = r + b_ref[...]
        if out_scale is not None:
            n_blocks, factor = out_scale
            r = r * jnp.where(pl.program_id(1) < n_blocks, factor, 1.0)
        o_ref[...] = r.astype(o_ref.dtype)


def _matmul(a, w, n_out, out_dtype, *, a2=None, bias=None, a_silu=False, col0=0, w_t=False,
            a_rows=None, scale_cols=None, tm=1024, tn=1024, tk=1024, name="matmul"):
    m, kdim = a.shape
    row0 = 0
    if a_rows is not None:
        row0, m = a_rows
    w_cols = w.shape[0] if w_t else w.shape[1]
    tm = _tile(math.gcd(m, row0), tm, 16)
    row_block0 = row0 // tm
    tn = _tile(math.gcd(n_out, col0, scale_cols[0] if scale_cols else 0), tn)
    tk = _tile(kdim, tk)
    nk1 = kdim // tk
    nk = nk1 if a2 is None else nk1 + a2.shape[1] // tk
    col_block0 = col0 // tn
    overhang = col0 + n_out > w_cols
    out_scale = None
    if scale_cols is not None:
        assert scale_cols[0] % tn == 0
        out_scale = (scale_cols[0] // tn, scale_cols[1])
    in_specs = [pl.BlockSpec((tm, tk), lambda i, j, k: (i + row_block0, jnp.minimum(k, nk1 - 1)))]
    args = [a]
    if a2 is not None:
        assert a2.shape[1] % tk == 0
        in_specs.append(pl.BlockSpec((tm, tk), lambda i, j, k: (i, jnp.maximum(k - nk1, 0))))
        args.append(a2)
    if w_t:
        in_specs.append(pl.BlockSpec((tn, tk), lambda i, j, k: (j + col_block0, k)))
    else:
        in_specs.append(pl.BlockSpec((tk, tn), lambda i, j, k: (k, j + col_block0)))
    args.append(w)
    if bias is not None:
        in_specs.append(pl.BlockSpec((1, tn), lambda i, j, k: (0, j)))
        args.append(bias)
    return pl.pallas_call(
        functools.partial(_mm_body, nk=nk, nk1=nk1, a_silu=a_silu, has_bias=bias is not None,
                          col_block0=col_block0, w_t=w_t, out_scale=out_scale,
                          w_cols_if_overhang=w_cols if overhang else None),
        grid=(m // tm, n_out // tn, nk),
        in_specs=in_specs,
        out_specs=pl.BlockSpec((tm, tn), lambda i, j, k: (i, j)),
        out_shape=jax.ShapeDtypeStruct((m, n_out), out_dtype),
        scratch_shapes=[pltpu.VMEM((tm, tn), F32)],
        compiler_params=_params(("parallel", "parallel", "arbitrary")),
        name=name,
    )(*args)


def _modulate_body(x_ref, sh_ref, sc_ref, o_ref):
    o_ref[0] = (x_ref[0] * (1.0 + sc_ref[0]) + sh_ref[0]).astype(o_ref.dtype)


def _modulate(x, mod3, shift_chunk, scale_chunk):
    b, s, d = x.shape
    ts = _tile(s, 512, 16)
    return pl.pallas_call(
        _modulate_body,
        grid=(b, s // ts),
        in_specs=[pl.BlockSpec((1, ts, d), lambda i, j: (i, j, 0)),
                  pl.BlockSpec((1, 1, d), lambda i, j: (i, 0, shift_chunk)),
                  pl.BlockSpec((1, 1, d), lambda i, j: (i, 0, scale_chunk))],
        out_specs=pl.BlockSpec((1, ts, d), lambda i, j: (i, j, 0)),
        out_shape=jax.ShapeDtypeStruct((b, s, d), BF16),
        compiler_params=_params(("parallel", "parallel")),
        name="modulate",
    )(x, mod3, mod3)


def _tail_body(t_ref, qg_ref, kvg_ref, lng_ref, lnb_ref, bf_ref,
               cq_ref, ckv_ref, kin_ref, small_ref, cum_ref, carry_ref, *, fh, ql, kvl, w_scale):
    j = pl.program_id(1)
    ts = t_ref.shape[0]

    @pl.when(j == 0)
    def _():
        carry_ref[...] = jnp.zeros_like(carry_ref)

    cq = t_ref[:, fh:fh + ql]
    cq_ref[...] = (cq * lax.rsqrt(jnp.mean(cq * cq, axis=-1, keepdims=True) + 1e-6)
                   * qg_ref[...]).astype(cq_ref.dtype)
    ckv = t_ref[:, fh + ql:fh + ql + kvl]
    ckv_ref[...] = (ckv * lax.rsqrt(jnp.mean(ckv * ckv, axis=-1, keepdims=True) + 1e-6)
                    * kvg_ref[...]).astype(ckv_ref.dtype)
    ki = t_ref[:, fh + ql + kvl:fh + ql + kvl + IDX_DIM]
    mu = jnp.mean(ki, axis=-1, keepdims=True)
    kc = ki - mu
    var = jnp.mean(kc * kc, axis=-1, keepdims=True)
    kin_ref[...] = (kc * lax.rsqrt(var + 1e-5) * lng_ref[...] + lnb_ref[...]).astype(kin_ref.dtype)

    small_ref[...] = t_ref[:, ql + kvl + IDX_DIM:ql + kvl + IDX_DIM + LANES] * w_scale
    z = t_ref[:, 0:LANES] + bf_ref[...]
    log_f = jnp.minimum(z, 0.0) - jnp.log(1.0 + jnp.exp(-jnp.abs(z)))
    row = lax.broadcasted_iota(I32, (ts, ts), 0)
    col = lax.broadcasted_iota(I32, (ts, ts), 1)
    tri = jnp.where(col <= row, 1.0, 0.0).astype(F32)
    incl = jnp.dot(tri, log_f, preferred_element_type=F32,
                   precision=lax.Precision.HIGHEST) + carry_ref[0:1, :]
    carry_ref[0:1, :] = incl[ts - 1:ts, :]
    cum_ref[...] = incl * (-LOG2E)


def _tail_prep(tail, b, s, ql, kvl, fh, ih, q_norm_g, kv_norm_g, ln_g, ln_b, b_forget):
    n = tail.shape[0]
    assert ql % LANES == 0 and kvl % LANES == 0
    ts = _tile(s, 256)
    nj = s // ts
    bf_pad = jnp.zeros((1, LANES), F32).at[0, :fh].set(b_forget)
    w_scale = (ih ** -0.5) * (IDX_DIM ** -0.5)
    row = lambda i, j: (i * nj + j, 0)
    const = lambda i, j: (0, 0)
    return pl.pallas_call(
        functools.partial(_tail_body, fh=fh, ql=ql, kvl=kvl, w_scale=w_scale),
        grid=(b, nj),
        in_specs=[pl.BlockSpec((ts, tail.shape[1]), row),
                  pl.BlockSpec((1, ql), const), pl.BlockSpec((1, kvl), const),
                  pl.BlockSpec((1, IDX_DIM), const), pl.BlockSpec((1, IDX_DIM), const),
                  pl.BlockSpec((1, LANES), const)],
        out_specs=[pl.BlockSpec((ts, ql), row), pl.BlockSpec((ts, kvl), row),
                   pl.BlockSpec((ts, IDX_DIM), row), pl.BlockSpec((ts, LANES), row),
                   pl.BlockSpec((ts, LANES), row)],
        out_shape=[jax.ShapeDtypeStruct((n, ql), BF16), jax.ShapeDtypeStruct((n, kvl), BF16),
                   jax.ShapeDtypeStruct((n, IDX_DIM), BF16), jax.ShapeDtypeStruct((n, LANES), F32),
                   jax.ShapeDtypeStruct((n, LANES), F32)],
        scratch_shapes=[pltpu.VMEM((8, LANES), F32)],
        compiler_params=_params(("parallel", "arbitrary")),
        name="tail_prep",
    )(tail, q_norm_g[None], kv_norm_g[None], ln_g[None], ln_b[None], bf_pad)


_MAX_EXACT = N_BUCKETS // 2
_BUCKET_THRESHOLDS = tuple(
    int(math.ceil(_MAX_EXACT * (MAX_DISTANCE / _MAX_EXACT) ** (k / (N_BUCKETS - _MAX_EXACT)) - 1e-9))
    for k in range(1, N_BUCKETS - _MAX_EXACT))


def _bias_body(rb_ref, o_ref, *, t):
    which = pl.program_id(0)
    row = lax.broadcasted_iota(I32, (t, t), 0)
    col = lax.broadcasted_iota(I32, (t, t), 1)
    d = jnp.maximum(which * t + col - row, 0)
    large = jnp.full((t, t), _MAX_EXACT, I32)
    for thr in _BUCKET_THRESHOLDS:
        large = large + jnp.where(d >= thr, 1, 0)
    bucket = jnp.where(d < _MAX_EXACT, d, large)
    for h in range(o_ref.shape[1]):
        bias = jnp.zeros((t, t), F32)
        for bkt in range(N_BUCKETS):
            bias = jnp.where(bucket == bkt, rb_ref[bkt, h], bias)
        o_ref[0, h] = (bias - rb_ref[N_BUCKETS - 1, h]) * LOG2E


def _bias_tiles(rel_bias, nh, t):
    assert 2 * t - (t - 1) >= _BUCKET_THRESHOLDS[-1], "key tiles two or more away are all last-bucket"
    return pl.pallas_call(
        functools.partial(_bias_body, t=t),
        grid=(2,),
        in_specs=[pl.BlockSpec(memory_space=pltpu.SMEM)],
        out_specs=pl.BlockSpec((1, nh, t, t), lambda w: (w, 0, 0, 0)),
        out_shape=jax.ShapeDtypeStruct((2, nh, t, t), F32),
        compiler_params=_params(("parallel",)),
        name="bias_tiles",
    )(rel_bias)


ATTN_QCOLS = 128


def _attn_body(qi_ref, kj_ref, *refs, nh, t, mode):
    if mode == "fox":
        q_ref, k_ref, vt_ref, g_ref, cum_ref, o_ref, m_ref, l_ref, acc_ref, neg_ref = refs
    else:
        q_ref, k_ref, vt_ref, g_ref, mask_ref, bt_ref, o_ref, m_ref, l_ref, acc_ref, neg_ref = refs
    p = pl.program_id(1)
    qi = qi_ref[p]
    kj = kj_ref[p]
    nblk = t // ATTN_QCOLS

    @pl.when(kj == 0)
    def _():
        m_ref[...] = jnp.full_like(m_ref, NEG_BIG)
        l_ref[...] = jnp.zeros_like(l_ref)
        acc_ref[...] = jnp.zeros_like(acc_ref)

    def heads(addend):
        for h in range(nh):
            sl = slice(h * HEAD_DIM, (h + 1) * HEAD_DIM)
            for r in range(nblk):
                qs = slice(r * ATTN_QCOLS, (r + 1) * ATTN_QCOLS)
                idx = h * nblk + r
                s = lax.dot_general(k_ref[:, sl], q_ref[qs, sl], (((1,), (1,)), ((), ())),
                                    preferred_element_type=F32)
                s = s + addend(h, qs)
                m_prev = m_ref[idx]
                m_next = jnp.maximum(m_prev, jnp.max(s, axis=0, keepdims=True))
                alpha = jnp.exp2(m_prev - m_next)
                pr = jnp.exp2(s - m_next)
                l_ref[idx] = alpha * l_ref[idx] + jnp.sum(pr, axis=0, keepdims=True)
                acc_ref[idx] = acc_ref[idx] * alpha + jnp.dot(
                    vt_ref[sl, :], pr.astype(BF16), preferred_element_type=F32)
                m_ref[idx] = m_next

    key_i = lax.broadcasted_iota(I32, (t, t), 0)
    qry_i = lax.broadcasted_iota(I32, (t, t), 1)
    if mode == "fox":
        @pl.when(kj == qi)
        def _():
            neg_ref[...] = jnp.where(key_i <= qry_i, 0.0, NEG_BIG).astype(F32)
            heads(lambda h, qs: neg_ref[:, qs] + cum_ref[:, h:h + 1])

        @pl.when(kj != qi)
        def _():
            heads(lambda h, qs: cum_ref[:, h:h + 1])
    else:
        neg_ref[...] = (mask_ref[0, 0].astype(F32) - 1.0) * (-NEG_BIG)

        @pl.when(qi - kj >= 2)
        def _():
            heads(lambda h, qs: neg_ref[:, qs])

        @pl.when(qi - kj < 2)
        def _():
            near = qi - kj
            heads(lambda h, qs: neg_ref[:, qs] + bt_ref[near, h, :, qs])

    @pl.when(kj == qi)
    def _():
        for r in range(nblk):
            qs = slice(r * ATTN_QCOLS, (r + 1) * ATTN_QCOLS)
            ss = jnp.zeros((1, ATTN_QCOLS), F32)
            for h in range(nh):
                o = acc_ref[h * nblk + r] / l_ref[h * nblk + r]
                acc_ref[h * nblk + r] = o
                ss = ss + jnp.sum(o * o, axis=0, keepdims=True)
            rinv = lax.rsqrt(ss / (nh * HEAD_DIM) + 1e-6)
            for h in range(nh):
                sl = slice(h * HEAD_DIM, (h + 1) * HEAD_DIM)
                o_ref[qs, sl] = ((acc_ref[h * nblk + r] * rinv).T * g_ref[:, sl]).astype(o_ref.dtype)


def _attention(mode, b, s, nh, t, q_src, k_src, vt, gain, *, cum=None, mask=None, bias_tiles=None):
    nq = s // t
    pairs = [(i, j) for i in range(nq) for j in range(i + 1)]
    qi_arr = jnp.asarray(np.array([pq for pq, _ in pairs], np.int32))
    kj_arr = jnp.asarray(np.array([pk for _, pk in pairs], np.int32))
    hw = nh * HEAD_DIM
    n = b * s
    nblk = t // ATTN_QCOLS

    def qmap(cb):
        return lambda bi, p, qi, kj: (bi * nq + qi[p], cb)

    def kmap(cb):
        return lambda bi, p, qi, kj: (bi * nq + kj[p], cb)

    in_specs = [pl.BlockSpec((t, hw), qmap(q_src[1])),
                pl.BlockSpec((t, hw), kmap(k_src[1])),
                pl.BlockSpec((hw, t), lambda bi, p, qi, kj: (0, bi * nq + kj[p])),
                pl.BlockSpec((1, hw), lambda bi, p, qi, kj: (0, 0))]
    args = [q_src[0], k_src[0], vt, gain[None]]
    if mode == "fox":
        in_specs.append(pl.BlockSpec((t, LANES), kmap(0)))
        args.append(cum)
    else:
        in_specs += [pl.BlockSpec((1, 1, t, t), lambda bi, p, qi, kj: (bi, kj[p], 0, qi[p])),
                     pl.BlockSpec(memory_space=pltpu.VMEM)]
        args += [mask, bias_tiles]
    grid_spec = pltpu.PrefetchScalarGridSpec(
        num_scalar_prefetch=2,
        grid=(b, len(pairs)),
        in_specs=in_specs,
        out_specs=pl.BlockSpec((t, hw), lambda bi, p, qi, kj: (bi * nq + qi[p], 0)),
        scratch_shapes=[pltpu.VMEM((nh * nblk, 1, ATTN_QCOLS), F32),
                        pltpu.VMEM((nh * nblk, 1, ATTN_QCOLS), F32),
                        pltpu.VMEM((nh * nblk, HEAD_DIM, ATTN_QCOLS), F32),
                        pltpu.VMEM((t, t), F32)],
    )
    return pl.pallas_call(
        functools.partial(_attn_body, nh=nh, t=t, mode=mode),
        grid_spec=grid_spec,
        out_shape=jax.ShapeDtypeStruct((n, hw), BF16),
        compiler_params=_params(("parallel", "arbitrary")),
        name=mode + "_attention",
    )(qi_arr, kj_arr, *args)


IDX_QCOLS = 128
SUBLANES = 8


def _float_key(x):
    bits = lax.bitcast_convert_type(x, I32)
    return bits ^ (lax.shift_right_arithmetic(bits, 31) & 0x7FFFFFFF)


def _indexer_body(q_ref, k_ref, w_ref, o_ref, key_ref, *, ih, w_lane0, tq, ck, nchunk, k_sel):
    i = pl.program_id(1)
    n_valid = ((i + 1) * tq + ck - 1) // ck
    key_i = lax.broadcasted_iota(I32, (ck, tq), 0)
    qry_i = lax.broadcasted_iota(I32, (ck, tq), 1) + i * tq
    w_t = w_ref[...].T

    def score_chunk(c, carry):
        kc = k_ref[0, pl.ds(pl.multiple_of(c * ck, ck), ck), :]
        for r in range(tq // IDX_QCOLS):
            qs = slice(r * IDX_QCOLS, (r + 1) * IDX_QCOLS)
            acc = jnp.zeros((ck, IDX_QCOLS), F32)
            for h in range(ih):
                sh = lax.dot_general(kc, q_ref[qs, h * IDX_DIM:(h + 1) * IDX_DIM],
                                     (((1,), (1,)), ((), ())), preferred_element_type=F32)
                acc = acc + w_t[w_lane0 + h:w_lane0 + h + 1, qs] * jnp.maximum(sh, 0.0)
            causal = (lax.broadcasted_iota(I32, (ck, IDX_QCOLS), 0) + c * ck
                      <= lax.broadcasted_iota(I32, (ck, IDX_QCOLS), 1) + (i * tq + r * IDX_QCOLS))
            key_ref[c, :, qs] = _float_key(jnp.where(causal, acc, -jnp.inf))
        return carry

    lax.fori_loop(0, n_valid, score_chunk, 0)

    def count(pred_fn, level):
        def body(c, acc):
            ind = jnp.where(pred_fn(key_ref[c], level), 1, 0)
            return acc + jnp.sum(ind.reshape(ck // SUBLANES, SUBLANES, tq), axis=0)
        acc = lax.fori_loop(0, n_valid, body, jnp.zeros((SUBLANES, tq), I32))
        return jnp.sum(acc, axis=0, keepdims=True)

    ge = lambda kv, lv: kv >= lv
    t0 = jnp.full((1, tq), INT_MIN, I32)
    zero = jnp.zeros((1, tq), I32)
    thr = jnp.where(count(ge, zero) >= k_sel, zero, t0)

    def bit_step(bi, thr):
        cand = thr | lax.shift_left(jnp.int32(1), 30 - bi)
        return jnp.where(count(ge, cand) >= k_sel, cand, thr)

    thr = lax.fori_loop(0, 31, bit_step, thr)
    need = (k_sel - count(lambda kv, lv: kv > lv, thr)).astype(F32)

    lrow = lax.broadcasted_iota(I32, (ck, ck), 0)
    lcol = lax.broadcasted_iota(I32, (ck, ck), 1)
    lower = jnp.where(lcol <= lrow, 1.0, 0.0).astype(BF16)
    ones = jnp.ones((SUBLANES, ck), BF16)

    def select_chunk(c, offset):
        kv = key_ref[c]
        eq = kv == thr
        eqb = jnp.where(eq, 1.0, 0.0).astype(BF16)
        rank = jnp.dot(lower, eqb, preferred_element_type=F32) + offset
        take = (kv > thr) | (eq & (rank <= need))
        take = take & (key_i + c * ck <= qry_i)
        o_ref[0, c] = jnp.where(take, 1, 0).astype(jnp.int8)
        return offset + jnp.dot(ones, eqb, preferred_element_type=F32)[0:1, :]

    lax.fori_loop(0, n_valid, select_chunk, jnp.zeros((1, tq), F32))

    def zero_chunk(c, carry):
        o_ref[0, c] = jnp.zeros((ck, tq), jnp.int8)
        return carry

    lax.fori_loop(n_valid, nchunk, zero_chunk, 0)


def _indexer(q_idx, k_in, small, b, s, ih, w_lane0, ck, k_sel):
    tq = _tile(s, 256, IDX_QCOLS)
    nq = s // tq
    nchunk = s // ck
    k3 = k_in.reshape(b, s, IDX_DIM)
    return pl.pallas_call(
        functools.partial(_indexer_body, ih=ih, w_lane0=w_lane0, tq=tq, ck=ck, nchunk=nchunk,
                          k_sel=k_sel),
        grid=(b, nq),
        in_specs=[pl.BlockSpec((tq, ih * IDX_DIM), lambda bi, i: (bi * nq + i, 0)),
                  pl.BlockSpec((1, s, IDX_DIM), lambda bi, i: (bi, 0, 0)),
                  pl.BlockSpec((tq, LANES), lambda bi, i: (bi * nq + i, 0))],
        out_specs=pl.BlockSpec((1, nchunk, ck, tq), lambda bi, i: (bi, 0, 0, i)),
        out_shape=jax.ShapeDtypeStruct((b, nchunk, ck, s), jnp.int8),
        scratch_shapes=[pltpu.VMEM((nchunk, ck, tq), I32)],
        compiler_params=_params(("parallel", "parallel")),
        name="indexer_topk",
    )(q_idx, k3, small)


def _layer_norm(h, g, bvec):
    mu = jnp.mean(h, axis=-1, keepdims=True)
    hc = h - mu
    var = jnp.mean(hc * hc, axis=-1, keepdims=True)
    return hc * lax.rsqrt(var + 1e-5) * g + bvec


def _router_body(x_ref, mix_ref, ga_ref, shm_ref, scm_ref, lg_ref, lb_ref, wr_ref, br_ref,
                 x1_ref, up_ref, eidx_ref, rank_ref, gate_ref, cnt_ref, carry_ref,
                 *, alpha_res, ne, nsteps):
    i = pl.program_id(0)
    tm, d = x_ref.shape

    @pl.when(i == 0)
    def _():
        carry_ref[...] = jnp.zeros_like(carry_ref)

    x1 = _layer_norm(alpha_res * x_ref[...] + ga_ref[0] * mix_ref[...].astype(F32),
                     lg_ref[...], lb_ref[...])
    x1_ref[...] = x1
    u = x1 * (1.0 + scm_ref[0]) + shm_ref[0]
    up_ref[...] = _pack_bf16_pair(u[:, :d // 2], u[:, d // 2:])

    def split(a):
        hi = a.astype(BF16)
        return hi, (a - hi.astype(F32)).astype(BF16)
    u_hi, u_lo = split(u)
    w_hi, w_lo = split(wr_ref[...])
    logits = (jnp.dot(u_hi, w_hi, preferred_element_type=F32)
              + jnp.dot(u_hi, w_lo, preferred_element_type=F32)
              + jnp.dot(u_lo, w_hi, preferred_element_type=F32)) + br_ref[...]
    lane = lax.broadcasted_iota(I32, (tm, LANES), 1)
    lane_f = lane.astype(F32)
    work = jnp.where(lane < ne, logits, -jnp.inf)
    sel = jnp.zeros((tm, LANES), F32)
    idxs, vals, hits = [], [], []
    for _ in range(TOP_K):
        mx = jnp.max(work, axis=1, keepdims=True)
        ik = jnp.min(jnp.where(work == mx, lane_f, float(LANES)), axis=1, keepdims=True)
        hit = lane_f == ik
        sel = jnp.where(hit, 1.0, sel)
        work = jnp.where(hit, -jnp.inf, work)
        idxs.append(ik.astype(I32))
        vals.append(mx)
        hits.append(hit)
    exps = [jnp.exp(v - vals[0]) for v in vals]
    denom = exps[0] + exps[1] + exps[2] + exps[3]

    row = lax.broadcasted_iota(I32, (tm, tm), 0)
    col = lax.broadcasted_iota(I32, (tm, tm), 1)
    tri = jnp.where(col < row, 1.0, 0.0).astype(BF16)
    before = jnp.dot(tri, sel.astype(BF16), preferred_element_type=F32) + carry_ref[0:1, :]
    total = carry_ref[0:1, :] + jnp.sum(sel, axis=0, keepdims=True)
    carry_ref[0:1, :] = total

    eidx = jnp.zeros((tm, LANES), I32)
    rank = jnp.zeros((tm, LANES), I32)
    gate = jnp.zeros((tm, LANES), F32)
    for kk in range(TOP_K):
        rk = jnp.sum(jnp.where(hits[kk], before, 0.0), axis=1, keepdims=True)
        eidx = jnp.where(lane == kk, idxs[kk], eidx)
        rank = jnp.where(lane == kk, rk.astype(I32), rank)
        gate = jnp.where(lane == kk, exps[kk] / denom, gate)
    eidx_ref[...] = eidx
    rank_ref[...] = rank
    gate_ref[...] = gate
    cnt_ref[...] = jnp.broadcast_to(total, cnt_ref.shape).astype(I32)


def _router(x2d, mix, mod3, ln_g, ln_b, w_router_pad, b_router_pad, s, ne, alpha_res):
    n, d = x2d.shape
    tm = _tile(s, 256, 16)
    nsteps = n // tm
    per_b = s // tm
    row = lambda i: (i, 0)
    const = lambda i: (0, 0)
    modspec = lambda chunk: pl.BlockSpec((1, 1, d), lambda i: (i // per_b, 0, chunk))
    return pl.pallas_call(
        functools.partial(_router_body, alpha_res=alpha_res, ne=ne, nsteps=nsteps),
        grid=(nsteps,),
        in_specs=[pl.BlockSpec((tm, d), row), pl.BlockSpec((tm, d), row),
                  modspec(2), modspec(3), modspec(4),
                  pl.BlockSpec((1, d), const), pl.BlockSpec((1, d), const),
                  pl.BlockSpec((d, LANES), const), pl.BlockSpec((1, LANES), const)],
        out_specs=[pl.BlockSpec((tm, d), row), pl.BlockSpec((tm, d // 2), row),
                   pl.BlockSpec((tm, LANES), row), pl.BlockSpec((tm, LANES), row),
                   pl.BlockSpec((tm, LANES), row), pl.BlockSpec((8, LANES), const)],
        out_shape=[jax.ShapeDtypeStruct((n, d), F32), jax.ShapeDtypeStruct((n, d // 2), I32),
                   jax.ShapeDtypeStruct((n, LANES), I32), jax.ShapeDtypeStruct((n, LANES), I32),
                   jax.ShapeDtypeStruct((n, LANES), F32), jax.ShapeDtypeStruct((8, LANES), I32)],
        scratch_shapes=[pltpu.VMEM((8, LANES), F32)],
        compiler_params=_params(("arbitrary",)),
        name="ln1_router",
    )(x2d, mix, mod3, mod3, mod3, ln_g[None], ln_b[None], w_router_pad, b_router_pad)


def _dispatch_body(dest_ref, u_ref, xs_ref, sem):
    i = pl.program_id(0)
    tc = u_ref.shape[0]

    def start(j, cc):
        for kk in range(TOP_K):
            d = dest_ref[(i * tc + j) * TOP_K + kk]
            pltpu.make_async_copy(u_ref.at[pl.ds(j, 1)], xs_ref.at[pl.ds(d, 1)], sem.at[0]).start()
        return cc

    lax.fori_loop(0, tc, start, 0, unroll=DMA_ISSUE_UNROLL)
    for _ in range(TOP_K):
        pltpu.make_async_copy(u_ref, xs_ref.at[pl.ds(0, tc)], sem.at[0]).wait()


def _dispatch(dest_flat, u_packed, n_slots):
    n, dw = u_packed.shape
    tc = _tile(n, 512, 8)
    grid_spec = pltpu.PrefetchScalarGridSpec(
        num_scalar_prefetch=1, grid=(n // tc,),
        in_specs=[pl.BlockSpec((tc, dw), lambda i, dr: (i, 0))],
        out_specs=pl.BlockSpec(memory_space=pl.ANY),
        scratch_shapes=[pltpu.SemaphoreType.DMA((1,))])
    return pl.pallas_call(
        _dispatch_body,
        grid_spec=grid_spec,
        out_shape=jax.ShapeDtypeStruct((n_slots, dw), I32),
        compiler_params=pltpu.CompilerParams(dimension_semantics=("arbitrary",),
                                             has_side_effects=True),
        name="moe_dispatch",
    )(dest_flat, u_packed)


def _ffn_body(te_ref, nsub_ref, nused_ref, x_ref, w1g_ref, w1u_ref, b1g_ref, b1u_ref,
              w2lo_ref, w2hi_ref, b2lo_ref, b2hi_ref, o_ref, h_ref, xbuf, xsem, *, nf, tf, ts, nr):
    i = pl.program_id(0)
    st = pl.program_id(1)
    _, tm, dh = xbuf.shape
    nsub = nsub_ref[i]
    slot = i % 2

    def x_copy(tile, to_slot):
        return pltpu.make_async_copy(x_ref.at[pl.ds(pl.multiple_of(tile * tm, tm), tm)],
                                     xbuf.at[to_slot], xsem.at[to_slot])

    @pl.when(jnp.logical_and(i == 0, st == 0))
    def _():
        x_copy(0, 0).start()

    @pl.when(st == 0)
    def _():
        x_copy(i, slot).wait()

    @pl.when(jnp.logical_and(st == nf, i + 1 < nused_ref[0]))
    def _():
        x_copy(i + 1, 1 - slot).start()

    def up_step(r0, rows):
        lo, hi = _unpack_bf16_pair(xbuf[slot, r0:r0 + rows, :])
        lo, hi = lo.astype(BF16), hi.astype(BF16)

        def proj(w_ref, b_ref):
            return (jnp.dot(lo, w_ref[0, 0:dh, :].astype(BF16), preferred_element_type=F32)
                    + jnp.dot(hi, w_ref[0, dh:2 * dh, :].astype(BF16), preferred_element_type=F32)
                    + b_ref[0])
        gate = jnp.minimum(proj(w1g_ref, b1g_ref), SWIGLU_LIMIT)
        up = jnp.clip(proj(w1u_ref, b1u_ref), -SWIGLU_LIMIT, SWIGLU_LIMIT)
        act = (up + 1.0) * gate * jax.nn.sigmoid(SWIGLU_ALPHA * gate)
        h_ref[st, r0:r0 + rows, :] = act.astype(BF16)

    def down_step(r0, rows, block_rows):
        def proj(w_ref, b_ref):
            acc = jnp.broadcast_to(b_ref[0], (rows, o_ref.shape[1])).astype(F32)
            for f in range(nf):
                acc = acc + jnp.dot(h_ref[f, r0:r0 + rows, :],
                                    w_ref[0, f * tf:(f + 1) * tf, :].astype(BF16),
                                    preferred_element_type=F32)
            return acc
        if rows:
            o_ref[r0:r0 + rows, :] = _pack_bf16_pair(proj(w2lo_ref, b2lo_ref),
                                                     proj(w2hi_ref, b2hi_ref))
        if rows < block_rows:
            o_ref[r0 + rows:r0 + block_rows, :] = jnp.zeros((block_rows - rows, o_ref.shape[1]),
                                                            I32)

    for blk in range(tm // (nr * ts)):
        r0 = blk * nr * ts
        nsub_b = jnp.clip(nsub - blk * nr, 0, nr)
        for r in range(nr + 1):
            if r:
                @pl.when(jnp.logical_and(nsub_b == r, st < nf))
                def _(r=r, r0=r0):
                    up_step(r0, r * ts)

            @pl.when(jnp.logical_and(nsub_b == r, st >= nf))
            def _(r=r, r0=r0):
                down_step(r0, r * ts, nr * ts)


def _expert_ffn(xs, w1, b1, w2, b2, tile_e, tile_nsub, n_used, *, tm, ts, n_tiles):
    ne, d, de2 = w1.shape
    de = de2 // 2
    dh = d // 2
    tf = _tile(de, 256)
    td = _tile(dh, 256)
    nf, nd = de // tf, dh // td
    nstep = nf + nd
    b1r = b1.reshape(ne, 1, de2)
    b2r = b2.reshape(ne, 1, d)

    def tile_idx(i, nu):
        return jnp.maximum(jnp.minimum(i, nu[0] - 1), 0)

    def step_idx(i, s, nu):
        return jnp.where(i < nu[0], s, nstep - 1)

    def up_map(off):
        return lambda i, s, te, ns, nu: (te[tile_idx(i, nu)], 0,
                                         off + jnp.minimum(step_idx(i, s, nu), nf - 1))

    def down_map(off):
        return lambda i, s, te, ns, nu: (te[tile_idx(i, nu)], 0,
                                         off + jnp.maximum(step_idx(i, s, nu) - nf, 0))

    grid_spec = pltpu.PrefetchScalarGridSpec(
        num_scalar_prefetch=3,
        grid=(n_used[0], nstep),
        in_specs=[pl.BlockSpec(memory_space=pl.ANY),
                  pl.BlockSpec((1, d, tf), up_map(0)), pl.BlockSpec((1, d, tf), up_map(nf)),
                  pl.BlockSpec((1, 1, tf), up_map(0)), pl.BlockSpec((1, 1, tf), up_map(nf)),
                  pl.BlockSpec((1, de, td), down_map(0)), pl.BlockSpec((1, de, td), down_map(nd)),
                  pl.BlockSpec((1, 1, td), down_map(0)), pl.BlockSpec((1, 1, td), down_map(nd))],
        out_specs=pl.BlockSpec((tm, td), lambda i, s, te, ns, nu: (
            tile_idx(i, nu), jnp.maximum(step_idx(i, s, nu) - nf, 0))),
        scratch_shapes=[pltpu.VMEM((nf, tm, tf), BF16), pltpu.VMEM((2, tm, dh), I32),
                        pltpu.SemaphoreType.DMA((2,))],
    )
    return pl.pallas_call(
        functools.partial(_ffn_body, nf=nf, tf=tf, ts=ts, nr=MOE_SUBTILES_PER_BLOCK),
        grid_spec=grid_spec,
        out_shape=jax.ShapeDtypeStruct((n_tiles * tm, dh), I32),
        compiler_params=_params(("arbitrary", "arbitrary")),
        name="expert_ffn",
    )(tile_e, tile_nsub, n_used, xs, w1, w1, b1r, b1r, w2, w2, b2r, b2r)


def _combine_body(dest_ref, y_ref, gate_ref, x1_ref, gm_ref, lg_ref, lb_ref, o_ref, ybuf, sem,
                  *, alpha_res):
    i = pl.program_id(0)
    nsteps = pl.num_programs(0)
    tc = x1_ref.shape[0]

    def gather_tile(tile, slot):
        def start(j, cc):
            for kk in range(TOP_K):
                d = dest_ref[(tile * tc + j) * TOP_K + kk]
                pltpu.make_async_copy(y_ref.at[pl.ds(d, 1)], ybuf.at[slot, kk, pl.ds(j, 1)],
                                      sem.at[slot]).start()
            return cc
        lax.fori_loop(0, tc, start, 0, unroll=DMA_ISSUE_UNROLL)

    slot = i % 2

    @pl.when(i == 0)
    def _():
        gather_tile(0, 0)

    @pl.when(i + 1 < nsteps)
    def _():
        gather_tile(i + 1, 1 - slot)

    for kk in range(TOP_K):
        pltpu.make_async_copy(y_ref.at[pl.ds(0, tc)], ybuf.at[slot, kk], sem.at[slot]).wait()
    g = gate_ref[...]
    y_lo = jnp.zeros((tc, ybuf.shape[3]), F32)
    y_hi = jnp.zeros((tc, ybuf.shape[3]), F32)
    for kk in range(TOP_K):
        lo, hi = _unpack_bf16_pair(ybuf[slot, kk])
        y_lo = y_lo + g[:, kk:kk + 1] * lo
        y_hi = y_hi + g[:, kk:kk + 1] * hi
    y = jnp.concatenate([y_lo, y_hi], axis=1)
    o_ref[...] = _layer_norm(alpha_res * x1_ref[...] + gm_ref[0] * y, lg_ref[...], lb_ref[...])


def _combine(dest_flat, y_sorted, gate, x1, mod3, ln_g, ln_b, s, alpha_res):
    n, d = x1.shape
    tc = _tile(s, 256, 8)
    per_b = s // tc
    row = lambda i, dr: (i, 0)
    const = lambda i, dr: (0, 0)
    grid_spec = pltpu.PrefetchScalarGridSpec(
        num_scalar_prefetch=1, grid=(n // tc,),
        in_specs=[pl.BlockSpec(memory_space=pl.ANY),
                  pl.BlockSpec((tc, LANES), row), pl.BlockSpec((tc, d), row),
                  pl.BlockSpec((1, 1, d), lambda i, dr: (i // per_b, 0, 5)),
                  pl.BlockSpec((1, d), const), pl.BlockSpec((1, d), const)],
        out_specs=pl.BlockSpec((tc, d), row),
        scratch_shapes=[pltpu.VMEM((2, TOP_K, tc, d // 2), I32), pltpu.SemaphoreType.DMA((2,))])
    return pl.pallas_call(
        functools.partial(_combine_body, alpha_res=alpha_res),
        grid_spec=grid_spec,
        out_shape=jax.ShapeDtypeStruct((n, d), F32),
        compiler_params=_params(("arbitrary",)),
        name="moe_combine",
    )(dest_flat, y_sorted, gate, x1, mod3, ln_g[None], ln_b[None])


MOE_SUBTILES_PER_BLOCK = 5
MOE_BLOCKS_PER_TILE = 1


def _moe_tiles(n_tok, n_exp):
    mean_load = n_tok * TOP_K // n_exp
    ts = 16
    while ts * 2 <= min(256, mean_load // (4 * MOE_BLOCKS_PER_TILE)):
        ts *= 2
    return ts, MOE_BLOCKS_PER_TILE * MOE_SUBTILES_PER_BLOCK * ts


def _layer(x, c_pad, w_ada, b_ada, w_in, b_forget, q_norm_g, kv_norm_g, kidx_ln_g, kidx_ln_b,
           w_uq, w_uk, w_uv, w_iq, fox_out_g, dsa_out_g, w_o, ln1_g, ln1_b,
           w_router, b_router, w1, b1, w2, b2, ln2_g, ln2_b, rel_bias, alpha_res):
    b, s, d = x.shape
    n = b * s
    fh = b_forget.shape[0]
    ql = q_norm_g.shape[0]
    kvl = kv_norm_g.shape[0]
    dh = w_uk.shape[1]
    ih = w_iq.shape[1] // IDX_DIM
    ne = w_router.shape[1]
    fw = fh * HEAD_DIM
    dw = dh * HEAD_DIM
    assert fw == dw, "head groups share the attention kernel's column-block width"
    assert fh + ih <= LANES and ne <= LANES

    mod = _matmul(c_pad, w_ada, 6 * d, F32, bias=b_ada[None], a_silu=True,
                  tm=16, tn=512, tk=d, name="adaln")
    mod3 = mod[:b].reshape(b, 1, 6 * d)
    u = _modulate(x, mod3, 0, 1).reshape(n, d)

    w_in_t = jnp.swapaxes(w_in, 0, 1)
    q_scale = HEAD_DIM ** -0.5 * LOG2E
    qk = _matmul(u, w_in_t, 2 * fw, BF16, w_t=True, scale_cols=(fw, q_scale), **WIDE_TILES,
                 name="proj_qk")
    tail_w = -(-(ql + kvl + IDX_DIM + fh + ih) // LANES) * LANES
    tail = _matmul(u, w_in_t, tail_w, F32, w_t=True, col0=3 * fw, tm=2048, name="proj_tail")
    cqn, ckvn, kin, small, cum = _tail_prep(tail, b, s, ql, kvl, fh, ih, q_norm_g, kv_norm_g,
                                            kidx_ln_g, kidx_ln_b, b_forget)
    v_t = _matmul(w_in_t, u, n, BF16, w_t=True, a_rows=(2 * fw, fw), **WIDE_TILES,
                  name="proj_vt")

    t = 256 if s % 256 == 0 else 128
    fox = _attention("fox", b, s, fh, t, (qk, 0), (qk, 1), v_t, fox_out_g, cum=cum)

    q_d = _matmul(cqn, w_uq, dw, BF16, scale_cols=(dw, q_scale), **WIDE_TILES, name="proj_qd")
    q_i = _matmul(cqn, w_iq, ih * IDX_DIM, BF16, **WIDE_TILES, name="proj_qi")
    k_d = _matmul(ckvn, w_uk.reshape(kvl, dw), dw, BF16, **WIDE_TILES, name="expand_k")
    vd_t = _matmul(jnp.swapaxes(w_uv.reshape(kvl, dw), 0, 1), ckvn, n, BF16, w_t=True,
                   **WIDE_TILES, name="expand_vt")
    k_sel = min(TOPK_MAX, s // 4)
    mask = _indexer(q_i, kin, small, b, s, ih, fh, t, k_sel)
    bias_t = _bias_tiles(rel_bias, dh, t)
    dsa = _attention("dsa", b, s, dh, t, (q_d, 0), (k_d, 0), vd_t, dsa_out_g,
                     mask=mask, bias_tiles=bias_t)

    mix = _matmul(fox, w_o, d, BF16, a2=dsa, **WIDE_TILES, name="proj_out")

    wr_pad = jnp.zeros((d, LANES), F32).at[:, :ne].set(w_router)
    br_pad = jnp.zeros((1, LANES), F32).at[0, :ne].set(b_router)
    x1, u_packed, eidx, rank, gate, cnt = _router(x.reshape(n, d), mix, mod3, ln1_g, ln1_b,
                                                  wr_pad, br_pad, s, ne, alpha_res)
    ts, tm = _moe_tiles(n, ne)
    n_tiles = -(-(n * TOP_K) // tm) + ne
    counts = cnt[0, :ne]
    ntile = (counts + tm - 1) // tm
    tend = jnp.cumsum(ntile)
    tstart = tend - ntile
    n_used = tend[-1]
    onehot = eidx[:, :TOP_K, None] == jnp.arange(ne, dtype=I32)
    dest = jnp.sum(jnp.where(onehot, tstart * tm, 0), axis=-1) + rank[:, :TOP_K]
    dest_flat = dest.reshape(n * TOP_K).astype(I32)
    tid = jnp.minimum(jnp.arange(n_tiles, dtype=I32), n_used - 1)
    tile_e = jnp.minimum(jnp.searchsorted(tend, tid, side="right"), ne - 1).astype(I32)
    valid = jnp.clip(counts[tile_e] - (tid - tstart[tile_e]) * tm, 0, tm)
    valid = jnp.where(jnp.arange(n_tiles) < n_used, valid, 0)
    tile_nsub = ((valid + ts - 1) // ts).astype(I32)
    xs = _dispatch(dest_flat, u_packed, n_tiles * tm)
    y_sorted = _expert_ffn(xs, w1, b1, w2, b2, tile_e, tile_nsub,
                           n_used.reshape(1).astype(I32), tm=tm, ts=ts, n_tiles=n_tiles)
    out = _combine(dest_flat, y_sorted, gate, x1, mod3, ln2_g, ln2_b, s, alpha_res)
    return out.reshape(b, s, d)


def kernel(x, c, w_ada, b_ada, w_in, b_forget, q_norm_g, kv_norm_g, kidx_ln_g, kidx_ln_b, w_uq, w_uk, w_uv, w_iq, fox_out_g, dsa_out_g, w_o, ln1_g, ln1_b, w_router, b_router, w1, b1, w2, b2, ln2_g, ln2_b, rel_bias):
    depth = w_ada.shape[0]
    alpha_res = (2 * depth) ** 0.25
    b, d = c.shape
    c_pad = jnp.zeros((16, d), F32).at[:b].set(c)
    for l in range(depth):
        x = _layer(x, c_pad, w_ada[l], b_ada[l], w_in[l], b_forget[l], q_norm_g[l], kv_norm_g[l],
                   kidx_ln_g[l], kidx_ln_b[l], w_uq[l], w_uk[l], w_uv[l], w_iq[l], fox_out_g[l],
                   dsa_out_g[l], w_o[l], ln1_g[l], ln1_b[l], w_router[l], b_router[l],
                   w1[l], b1[l], w2[l], b2[l], ln2_g[l], ln2_b[l], rel_bias, alpha_res)
    return x
```

```python
import functools
import math

import numpy as np
import jax
import jax.numpy as jnp
from jax import lax
from jax.experimental import pallas as pl
from jax.experimental.pallas import tpu as pltpu

F32 = jnp.float32
BF16 = jnp.bfloat16
I32 = jnp.int32

LANES = 128
HEAD_DIM = 128
IDX_DIM = 128
TOPK_MAX = 256
N_BUCKETS = 32
MAX_DISTANCE = 128
TOP_K = 4
SWIGLU_LIMIT = 7.0
SWIGLU_ALPHA = 1.702
NEG_BIG = -1e30
LOG2E = math.log2(math.e)
VMEM_LIMIT = 56 * 1024 * 1024
INT_MIN = -2 ** 31
DMA_ISSUE_UNROLL = 8
WIDE_TILES = dict(tm=2048, tn=2048, tk=512)


def _tile(n, pref, unit=LANES):
    if n <= pref:
        return n
    t = (pref // unit) * unit
    while t > unit and n % t:
        t -= unit
    assert n % t == 0, (n, pref)
    return t


def _pack_bf16_pair(lo, hi):
    lo_bits = lax.bitcast_convert_type(lo.astype(BF16).astype(F32), I32)
    hi_bits = lax.bitcast_convert_type(hi.astype(BF16).astype(F32), I32)
    return lax.shift_right_logical(lo_bits, 16) | (hi_bits & jnp.int32(-65536))


def _unpack_bf16_pair(words):
    lo = lax.bitcast_convert_type(lax.shift_left(words, 16), F32)
    hi = lax.bitcast_convert_type(words & jnp.int32(-65536), F32)
    return lo, hi


def _params(sem, vmem=VMEM_LIMIT):
    return pltpu.CompilerParams(dimension_semantics=sem, vmem_limit_bytes=vmem)


def _mm_body(*refs, nk, nk1, a_silu, has_bias, col_block0, w_cols_if_overhang, w_t, out_scale):
    refs = list(refs)
    a_ref = refs.pop(0)
    a2_ref = refs.pop(0) if nk1 < nk else None
    w_ref = refs.pop(0)
    b_ref = refs.pop(0) if has_bias else None
    o_ref, acc_ref = refs
    k = pl.program_id(2)

    @pl.when(k == 0)
    def _():
        acc_ref[...] = jnp.zeros_like(acc_ref)

    def accumulate(src_ref):
        a = src_ref[...]
        if a_silu:
            a = a.astype(F32)
            a = a * jax.nn.sigmoid(a)
        w = w_ref[...]
        n_axis = 0 if w_t else 1
        if w_cols_if_overhang is not None:
            tn = w.shape[n_axis]
            col = lax.broadcasted_iota(I32, w.shape, n_axis) + (pl.program_id(1) + col_block0) * tn
            w = jnp.where(col < w_cols_if_overhang, w, 0.0)
        acc_ref[...] += lax.dot_general(a.astype(BF16), w.astype(BF16),
                                        (((1,), (1 if w_t else 0,)), ((), ())),
                                        preferred_element_type=F32)

    if a2_ref is None:
        accumulate(a_ref)
    else:
        pl.when(k < nk1)(lambda: accumulate(a_ref))
        pl.when(k >= nk1)(lambda: accumulate(a2_ref))

    @pl.when(k == nk - 1)
    def _():
        r = acc_ref[...]
        if has_bias:
            r = r + b_ref[...]
        if out_scale is not None:
            n_blocks, factor = out_scale
            r = r * jnp.where(pl.program_id(1) < n_blocks, factor, 1.0)
        o_ref[...] = r.astype(o_ref.dtype)


def _matmul(a, w, n_out, out_dtype, *, a2=None, bias=None, a_silu=False, col0=0, w_t=False,
            a_rows=None, scale_cols=None, tm=1024, tn=1024, tk=1024, name="matmul"):
    m, kdim = a.shape
    row0 = 0
    if a_rows is not None:
        row0, m = a_rows
    w_cols = w.shape[0] if w_t else w.shape[1]
    tm = _tile(math.gcd(m, row0), tm, 16)
    row_block0 = row0 // tm
    tn = _tile(math.gcd(n_out, col0, scale_cols[0] if scale_cols else 0), tn)
    tk = _tile(kdim, tk)
    nk1 = kdim // tk
    nk = nk1 if a2 is None else nk1 + a2.shape[1] // tk
    col_block0 = col0 // tn
    overhang = col0 + n_out > w_cols
    out_scale = None
    if scale_cols is not None:
        assert scale_cols[0] % tn == 0
        out_scale = (scale_cols[0] // tn, scale_cols[1])
    in_specs = [pl.BlockSpec((tm, tk), lambda i, j, k: (i + row_block0, jnp.minimum(k, nk1 - 1)))]
    args = [a]
    if a2 is not None:
        assert a2.shape[1] % tk == 0
        in_specs.append(pl.BlockSpec((tm, tk), lambda i, j, k: (i, jnp.maximum(k - nk1, 0))))
        args.append(a2)
    if w_t:
        in_specs.append(pl.BlockSpec((tn, tk), lambda i, j, k: (j + col_block0, k)))
    else:
        in_specs.append(pl.BlockSpec((tk, tn), lambda i, j, k: (k, j + col_block0)))
    args.append(w)
    if bias is not None:
        in_specs.append(pl.BlockSpec((1, tn), lambda i, j, k: (0, j)))
        args.append(bias)
    return pl.pallas_call(
        functools.partial(_mm_body, nk=nk, nk1=nk1, a_silu=a_silu, has_bias=bias is not None,
                          col_block0=col_block0, w_t=w_t, out_scale=out_scale,
                          w_cols_if_overhang=w_cols if overhang else None),
        grid=(m // tm, n_out // tn, nk),
        in_specs=in_specs,
        out_specs=pl.BlockSpec((tm, tn), lambda i, j, k: (i, j)),
        out_shape=jax.ShapeDtypeStruct((m, n_out), out_dtype),
        scratch_shapes=[pltpu.VMEM((tm, tn), F32)],
        compiler_params=_params(("parallel", "parallel", "arbitrary")),
        name=name,
    )(*args)


def _modulate_body(x_ref, sh_ref, sc_ref, o_ref):
    o_ref[0] = (x_ref[0] * (1.0 + sc_ref[0]) + sh_ref[0]).astype(o_ref.dtype)


def _modulate(x, mod3, shift_chunk, scale_chunk):
    b, s, d = x.shape
    ts = _tile(s, 512, 16)
    return pl.pallas_call(
        _modulate_body,
        grid=(b, s // ts),
        in_specs=[pl.BlockSpec((1, ts, d), lambda i, j: (i, j, 0)),
                  pl.BlockSpec((1, 1, d), lambda i, j: (i, 0, shift_chunk)),
                  pl.BlockSpec((1, 1, d), lambda i, j: (i, 0, scale_chunk))],
        out_specs=pl.BlockSpec((1, ts, d), lambda i, j: (i, j, 0)),
        out_shape=jax.ShapeDtypeStruct((b, s, d), BF16),
        compiler_params=_params(("parallel", "parallel")),
        name="modulate",
    )(x, mod3, mod3)


def _tail_body(t_ref, qg_ref, kvg_ref, lng_ref, lnb_ref, bf_ref,
               cq_ref, ckv_ref, kin_ref, small_ref, cum_ref, carry_ref, *, fh, ql, kvl, w_scale):
    j = pl.program_id(1)
    ts = t_ref.shape[0]

    @pl.when(j == 0)
    def _():
        carry_ref[...] = jnp.zeros_like(carry_ref)

    cq = t_ref[:, fh:fh + ql]
    cq_ref[...] = (cq * lax.rsqrt(jnp.mean(cq * cq, axis=-1, keepdims=True) + 1e-6)
                   * qg_ref[...]).astype(cq_ref.dtype)
    ckv = t_ref[:, fh + ql:fh + ql + kvl]
    ckv_ref[...] = (ckv * lax.rsqrt(jnp.mean(ckv * ckv, axis=-1, keepdims=True) + 1e-6)
                    * kvg_ref[...]).astype(ckv_ref.dtype)
    ki = t_ref[:, fh + ql + kvl:fh + ql + kvl + IDX_DIM]
    mu = jnp.mean(ki, axis=-1, keepdims=True)
    kc = ki - mu
    var = jnp.mean(kc * kc, axis=-1, keepdims=True)
    kin_ref[...] = (kc * lax.rsqrt(var + 1e-5) * lng_ref[...] + lnb_ref[...]).astype(kin_ref.dtype)

    small_ref[...] = t_ref[:, ql + kvl + IDX_DIM:ql + kvl + IDX_DIM + LANES] * w_scale
    z = t_ref[:, 0:LANES] + bf_ref[...]
    log_f = jnp.minimum(z, 0.0) - jnp.log(1.0 + jnp.exp(-jnp.abs(z)))
    row = lax.broadcasted_iota(I32, (ts, ts), 0)
    col = lax.broadcasted_iota(I32, (ts, ts), 1)
    tri = jnp.where(col <= row, 1.0, 0.0).astype(F32)
    incl = jnp.dot(tri, log_f, preferred_element_type=F32,
                   precision=lax.Precision.HIGHEST) + carry_ref[0:1, :]
    carry_ref[0:1, :] = incl[ts - 1:ts, :]
    cum_ref[...] = incl * (-LOG2E)


def _tail_prep(tail, b, s, ql, kvl, fh, ih, q_norm_g, kv_norm_g, ln_g, ln_b, b_forget):
    n = tail.shape[0]
    assert ql % LANES == 0 and kvl % LANES == 0
    ts = _tile(s, 256)
    nj = s // ts
    bf_pad = jnp.zeros((1, LANES), F32).at[0, :fh].set(b_forget)
    w_scale = (ih ** -0.5) * (IDX_DIM ** -0.5)
    row = lambda i, j: (i * nj + j, 0)
    const = lambda i, j: (0, 0)
    return pl.pallas_call(
        functools.partial(_tail_body, fh=fh, ql=ql, kvl=kvl, w_scale=w_scale),
        grid=(b, nj),
        in_specs=[pl.BlockSpec((ts, tail.shape[1]), row),
                  pl.BlockSpec((1, ql), const), pl.BlockSpec((1, kvl), const),
                  pl.BlockSpec((1, IDX_DIM), const), pl.BlockSpec((1, IDX_DIM), const),
                  pl.BlockSpec((1, LANES), const)],
        out_specs=[pl.BlockSpec((ts, ql), row), pl.BlockSpec((ts, kvl), row),
                   pl.BlockSpec((ts, IDX_DIM), row), pl.BlockSpec((ts, LANES), row),
                   pl.BlockSpec((ts, LANES), row)],
        out_shape=[jax.ShapeDtypeStruct((n, ql), BF16), jax.ShapeDtypeStruct((n, kvl), BF16),
                   jax.ShapeDtypeStruct((n, IDX_DIM), BF16), jax.ShapeDtypeStruct((n, LANES), F32),
                   jax.ShapeDtypeStruct((n, LANES), F32)],
        scratch_shapes=[pltpu.VMEM((8, LANES), F32)],
        compiler_params=_params(("parallel", "arbitrary")),
        name="tail_prep",
    )(tail, q_norm_g[None], kv_norm_g[None], ln_g[None], ln_b[None], bf_pad)


_MAX_EXACT = N_BUCKETS // 2
_BUCKET_THRESHOLDS = tuple(
    int(math.ceil(_MAX_EXACT * (MAX_DISTANCE / _MAX_EXACT) ** (k / (N_BUCKETS - _MAX_EXACT)) - 1e-9))
    for k in range(1, N_BUCKETS - _MAX_EXACT))


def _bias_body(rb_ref, o_ref, *, t):
    which = pl.program_id(0)
    row = lax.broadcasted_iota(I32, (t, t), 0)
    col = lax.broadcasted_iota(I32, (t, t), 1)
    d = jnp.maximum(which * t + col - row, 0)
    large = jnp.full((t, t), _MAX_EXACT, I32)
    for thr in _BUCKET_THRESHOLDS:
        large = large + jnp.where(d >= thr, 1, 0)
    bucket = jnp.where(d < _MAX_EXACT, d, large)
    for h in range(o_ref.shape[1]):
        bias = jnp.zeros((t, t), F32)
        for bkt in range(N_BUCKETS):
            bias = jnp.where(bucket == bkt, rb_ref[bkt, h], bias)
        o_ref[0, h] = (bias - rb_ref[N_BUCKETS - 1, h]) * LOG2E


def _bias_tiles(rel_bias, nh, t):
    assert 2 * t - (t - 1) >= _BUCKET_THRESHOLDS[-1], "key tiles two or more away are all last-bucket"
    return pl.pallas_call(
        functools.partial(_bias_body, t=t),
        grid=(2,),
        in_specs=[pl.BlockSpec(memory_space=pltpu.SMEM)],
        out_specs=pl.BlockSpec((1, nh, t, t), lambda w: (w, 0, 0, 0)),
        out_shape=jax.ShapeDtypeStruct((2, nh, t, t), F32),
        compiler_params=_params(("parallel",)),
        name="bias_tiles",
    )(rel_bias)


ATTN_QCOLS = 128


def _attn_body(qi_ref, kj_ref, *refs, nh, t, mode):
    if mode == "fox":
        q_ref, k_ref, vt_ref, g_ref, cum_ref, o_ref, m_ref, l_ref, acc_ref, neg_ref = refs
    else:
        q_ref, k_ref, vt_ref, g_ref, mask_ref, bt_ref, o_ref, m_ref, l_ref, acc_ref, neg_ref = refs
    p = pl.program_id(1)
    qi = qi_ref[p]
    kj = kj_ref[p]
    nblk = t // ATTN_QCOLS

    @pl.when(kj == 0)
    def _():
        m_ref[...] = jnp.full_like(m_ref, NEG_BIG)
        l_ref[...] = jnp.zeros_like(l_ref)
        acc_ref[...] = jnp.zeros_like(acc_ref)

    def heads(addend):
        for h in range(nh):
            sl = slice(h * HEAD_DIM, (h + 1) * HEAD_DIM)
            for r in range(nblk):
                qs = slice(r * ATTN_QCOLS, (r + 1) * ATTN_QCOLS)
                idx = h * nblk + r
                s = lax.dot_general(k_ref[:, sl], q_ref[qs, sl], (((1,), (1,)), ((), ())),
                                    preferred_element_type=F32)
                s = s + addend(h, qs)
                m_prev = m_ref[idx]
                m_next = jnp.maximum(m_prev, jnp.max(s, axis=0, keepdims=True))
                alpha = jnp.exp2(m_prev - m_next)
                pr = jnp.exp2(s - m_next)
                l_ref[idx] = alpha * l_ref[idx] + jnp.sum(pr, axis=0, keepdims=True)
                acc_ref[idx] = acc_ref[idx] * alpha + jnp.dot(
                    vt_ref[sl, :], pr.astype(BF16), preferred_element_type=F32)
                m_ref[idx] = m_next

    if mode == "fox":
        @pl.when(kj == qi)
        def _():
            key_i = lax.broadcasted_iota(I32, (t, t), 0)
            qry_i = lax.broadcasted_iota(I32, (t, t), 1)
            neg_ref[...] = jnp.where(key_i <= qry_i, 0.0, NEG_BIG).astype(F32)
            heads(lambda h, qs: neg_ref[:, qs] + cum_ref[:, h:h + 1])

        @pl.when(kj != qi)
        def _():
            heads(lambda h, qs: cum_ref[:, h:h + 1])
    else:
        neg_ref[...] = (mask_ref[0, 0].astype(F32) - 1.0) * (-NEG_BIG)

        @pl.when(qi - kj >= 2)
        def _():
            heads(lambda h, qs: neg_ref[:, qs])

        @pl.when(qi - kj < 2)
        def _():
            near = qi - kj
            heads(lambda h, qs: neg_ref[:, qs] + bt_ref[near, h, :, qs])

    @pl.when(kj == qi)
    def _():
        for r in range(nblk):
            qs = slice(r * ATTN_QCOLS, (r + 1) * ATTN_QCOLS)
            ss = jnp.zeros((1, ATTN_QCOLS), F32)
            for h in range(nh):
                o = acc_ref[h * nblk + r] / l_ref[h * nblk + r]
                acc_ref[h * nblk + r] = o
                ss = ss + jnp.sum(o * o, axis=0, keepdims=True)
            rinv = lax.rsqrt(ss / (nh * HEAD_DIM) + 1e-6)
            for h in range(nh):
                sl = slice(h * HEAD_DIM, (h + 1) * HEAD_DIM)
                o_ref[qs, sl] = ((acc_ref[h * nblk + r] * rinv).T * g_ref[:, sl]).astype(o_ref.dtype)


def _attention(mode, b, s, nh, t, q_src, k_src, vt, gain, *, cum=None, mask=None, bias_tiles=None):
    nq = s // t
    pairs = [(i, j) for i in range(nq) for j in range(i + 1)]
    qi_arr = jnp.asarray(np.array([pq for pq, _ in pairs], np.int32))
    kj_arr = jnp.asarray(np.array([pk for _, pk in pairs], np.int32))
    hw = nh * HEAD_DIM
    n = b * s
    nblk = t // ATTN_QCOLS

    def qmap(cb):
        return lambda bi, p, qi, kj: (bi * nq + qi[p], cb)

    def kmap(cb):
        return lambda bi, p, qi, kj: (bi * nq + kj[p], cb)

    in_specs = [pl.BlockSpec((t, hw), qmap(q_src[1])),
                pl.BlockSpec((t, hw), kmap(k_src[1])),
                pl.BlockSpec((hw, t), lambda bi, p, qi, kj: (0, bi * nq + kj[p])),
                pl.BlockSpec((1, hw), lambda bi, p, qi, kj: (0, 0))]
    args = [q_src[0], k_src[0], vt, gain[None]]
    if mode == "fox":
        in_specs.append(pl.BlockSpec((t, LANES), kmap(0)))
        args.append(cum)
    else:
        in_specs += [pl.BlockSpec((1, 1, t, t), lambda bi, p, qi, kj: (bi, kj[p], 0, qi[p])),
                     pl.BlockSpec(memory_space=pltpu.VMEM)]
        args += [mask, bias_tiles]
    grid_spec = pltpu.PrefetchScalarGridSpec(
        num_scalar_prefetch=2,
        grid=(b, len(pairs)),
        in_specs=in_specs,
        out_specs=pl.BlockSpec((t, hw), lambda bi, p, qi, kj: (bi * nq + qi[p], 0)),
        scratch_shapes=[pltpu.VMEM((nh * nblk, 1, ATTN_QCOLS), F32),
                        pltpu.VMEM((nh * nblk, 1, ATTN_QCOLS), F32),
                        pltpu.VMEM((nh * nblk, HEAD_DIM, ATTN_QCOLS), F32),
                        pltpu.VMEM((t, t), F32)],
    )
    return pl.pallas_call(
        functools.partial(_attn_body, nh=nh, t=t, mode=mode),
        grid_spec=grid_spec,
        out_shape=jax.ShapeDtypeStruct((n, hw), BF16),
        compiler_params=_params(("parallel", "arbitrary")),
        name=mode + "_attention",
    )(qi_arr, kj_arr, *args)


IDX_QCOLS = 128
SUBLANES = 8


def _float_key(x):
    bits = lax.bitcast_convert_type(x, I32)
    return bits ^ (lax.shift_right_arithmetic(bits, 31) & 0x7FFFFFFF)


def _indexer_body(q_ref, k_ref, w_ref, o_ref, key_ref, *, ih, w_lane0, tq, ck, nchunk, k_sel):
    i = pl.program_id(1)
    n_valid = ((i + 1) * tq + ck - 1) // ck
    key_i = lax.broadcasted_iota(I32, (ck, tq), 0)
    qry_i = lax.broadcasted_iota(I32, (ck, tq), 1) + i * tq
    w_t = w_ref[...].T

    def score_chunk(c, carry):
        kc = k_ref[0, pl.ds(pl.multiple_of(c * ck, ck), ck), :]
        for r in range(tq // IDX_QCOLS):
            qs = slice(r * IDX_QCOLS, (r + 1) * IDX_QCOLS)
            acc = jnp.zeros((ck, IDX_QCOLS), F32)
            for h in range(ih):
                sh = lax.dot_general(kc, q_ref[qs, h * IDX_DIM:(h + 1) * IDX_DIM],
                                     (((1,), (1,)), ((), ())), preferred_element_type=F32)
                acc = acc + w_t[w_lane0 + h:w_lane0 + h + 1, qs] * jnp.maximum(sh, 0.0)
            causal = (lax.broadcasted_iota(I32, (ck, IDX_QCOLS), 0) + c * ck
                      <= lax.broadcasted_iota(I32, (ck, IDX_QCOLS), 1) + (i * tq + r * IDX_QCOLS))
            key_ref[c, :, qs] = _float_key(jnp.where(causal, acc, -jnp.inf))
        return carry

    lax.fori_loop(0, n_valid, score_chunk, 0)

    def count(pred_fn, level):
        def body(c, acc):
            ind = jnp.where(pred_fn(key_ref[c], level), 1, 0)
            return acc + jnp.sum(ind.reshape(ck // SUBLANES, SUBLANES, tq), axis=0)
        acc = lax.fori_loop(0, n_valid, body, jnp.zeros((SUBLANES, tq), I32))
        return jnp.sum(acc, axis=0, keepdims=True)

    ge = lambda kv, lv: kv >= lv
    t0 = jnp.full((1, tq), INT_MIN, I32)
    zero = jnp.zeros((1, tq), I32)
    thr = jnp.where(count(ge, zero) >= k_sel, zero, t0)

    def bit_step(bi, thr):
        cand = thr | lax.shift_left(jnp.int32(1), 30 - bi)
        return jnp.where(count(ge, cand) >= k_sel, cand, thr)

    thr = lax.fori_loop(0, 31, bit_step, thr)
    need = (k_sel - count(lambda kv, lv: kv > lv, thr)).astype(F32)

    lrow = lax.broadcasted_iota(I32, (ck, ck), 0)
    lcol = lax.broadcasted_iota(I32, (ck, ck), 1)
    lower = jnp.where(lcol <= lrow, 1.0, 0.0).astype(BF16)
    ones = jnp.ones((SUBLANES, ck), BF16)

    def select_chunk(c, offset):
        kv = key_ref[c]
        eq = kv == thr
        eqb = jnp.where(eq, 1.0, 0.0).astype(BF16)
        rank = jnp.dot(lower, eqb, preferred_element_type=F32) + offset
        take = (kv > thr) | (eq & (rank <= need))
        take = take & (key_i + c * ck <= qry_i)
        o_ref[0, c] = jnp.where(take, 1, 0).astype(jnp.int8)
        return offset + jnp.dot(ones, eqb, preferred_element_type=F32)[0:1, :]

    lax.fori_loop(0, n_valid, select_chunk, jnp.zeros((1, tq), F32))

    def zero_chunk(c, carry):
        o_ref[0, c] = jnp.zeros((ck, tq), jnp.int8)
        return carry

    lax.fori_loop(n_valid, nchunk, zero_chunk, 0)


def _indexer(q_idx, k_in, small, b, s, ih, w_lane0, ck, k_sel):
    tq = _tile(s, 256, IDX_QCOLS)
    nq = s // tq
    nchunk = s // ck
    k3 = k_in.reshape(b, s, IDX_DIM)
    return pl.pallas_call(
        functools.partial(_indexer_body, ih=ih, w_lane0=w_lane0, tq=tq, ck=ck, nchunk=nchunk,
                          k_sel=k_sel),
        grid=(b, nq),
        in_specs=[pl.BlockSpec((tq, ih * IDX_DIM), lambda bi, i: (bi * nq + i, 0)),
                  pl.BlockSpec((1, s, IDX_DIM), lambda bi, i: (bi, 0, 0)),
                  pl.BlockSpec((tq, LANES), lambda bi, i: (bi * nq + i, 0))],
        out_specs=pl.BlockSpec((1, nchunk, ck, tq), lambda bi, i: (bi, 0, 0, i)),
        out_shape=jax.ShapeDtypeStruct((b, nchunk, ck, s), jnp.int8),
        scratch_shapes=[pltpu.VMEM((nchunk, ck, tq), I32)],
        compiler_params=_params(("parallel", "parallel")),
        name="indexer_topk",
    )(q_idx, k3, small)


def _layer_norm(h, g, bvec):
    mu = jnp.mean(h, axis=-1, keepdims=True)
    hc = h - mu
    var = jnp.mean(hc * hc, axis=-1, keepdims=True)
    return hc * lax.rsqrt(var + 1e-5) * g + bvec


def _router_body(x_ref, mix_ref, ga_ref, shm_ref, scm_ref, lg_ref, lb_ref, wr_ref, br_ref,
                 x1_ref, up_ref, eidx_ref, rank_ref, gate_ref, cnt_ref, carry_ref,
                 *, alpha_res, ne):
    i = pl.program_id(0)
    tm, d = x_ref.shape

    @pl.when(i == 0)
    def _():
        carry_ref[...] = jnp.zeros_like(carry_ref)

    x1 = _layer_norm(alpha_res * x_ref[...] + ga_ref[0] * mix_ref[...].astype(F32),
                     lg_ref[...], lb_ref[...])
    x1_ref[...] = x1
    u = x1 * (1.0 + scm_ref[0]) + shm_ref[0]
    up_ref[...] = _pack_bf16_pair(u[:, :d // 2], u[:, d // 2:])

    def split(a):
        hi = a.astype(BF16)
        return hi, (a - hi.astype(F32)).astype(BF16)
    u_hi, u_lo = split(u)
    w_hi, w_lo = split(wr_ref[...])
    logits = (jnp.dot(u_hi, w_hi, preferred_element_type=F32)
              + jnp.dot(u_hi, w_lo, preferred_element_type=F32)
              + jnp.dot(u_lo, w_hi, preferred_element_type=F32)) + br_ref[...]
    lane = lax.broadcasted_iota(I32, (tm, LANES), 1)
    lane_f = lane.astype(F32)
    work = jnp.where(lane < ne, logits, -jnp.inf)
    sel = jnp.zeros((tm, LANES), F32)
    idxs, vals, hits = [], [], []
    for _ in range(TOP_K):
        mx = jnp.max(work, axis=1, keepdims=True)
        ik = jnp.min(jnp.where(work == mx, lane_f, float(LANES)), axis=1, keepdims=True)
        hit = lane_f == ik
        sel = jnp.where(hit, 1.0, sel)
        work = jnp.where(hit, -jnp.inf, work)
        idxs.append(ik.astype(I32))
        vals.append(mx)
        hits.append(hit)
    exps = [jnp.exp(v - vals[0]) for v in vals]
    denom = exps[0] + exps[1] + exps[2] + exps[3]

    row = lax.broadcasted_iota(I32, (tm, tm), 0)
    col = lax.broadcasted_iota(I32, (tm, tm), 1)
    tri = jnp.where(col < row, 1.0, 0.0).astype(BF16)
    before = jnp.dot(tri, sel.astype(BF16), preferred_element_type=F32) + carry_ref[0:1, :]
    total = carry_ref[0:1, :] + jnp.sum(sel, axis=0, keepdims=True)
    carry_ref[0:1, :] = total

    eidx = jnp.zeros((tm, LANES), I32)
    rank = jnp.zeros((tm, LANES), I32)
    gate = jnp.zeros((tm, LANES), F32)
    for kk in range(TOP_K):
        rk = jnp.sum(jnp.where(hits[kk], before, 0.0), axis=1, keepdims=True)
        eidx = jnp.where(lane == kk, idxs[kk], eidx)
        rank = jnp.where(lane == kk, rk.astype(I32), rank)
        gate = jnp.where(lane == kk, exps[kk] / denom, gate)
    eidx_ref[...] = eidx
    rank_ref[...] = rank
    gate_ref[...] = gate
    cnt_ref[...] = jnp.broadcast_to(total, cnt_ref.shape).astype(I32)


def _router(x2d, mix, mod3, ln_g, ln_b, w_router_pad, b_router_pad, s, ne, alpha_res):
    n, d = x2d.shape
    tm = _tile(s, 256, 16)
    nsteps = n // tm
    per_b = s // tm
    row = lambda i: (i, 0)
    const = lambda i: (0, 0)
    modspec = lambda chunk: pl.BlockSpec((1, 1, d), lambda i: (i // per_b, 0, chunk))
    return pl.pallas_call(
        functools.partial(_router_body, alpha_res=alpha_res, ne=ne),
        grid=(nsteps,),
        in_specs=[pl.BlockSpec((tm, d), row), pl.BlockSpec((tm, d), row),
                  modspec(2), modspec(3), modspec(4),
                  pl.BlockSpec((1, d), const), pl.BlockSpec((1, d), const),
                  pl.BlockSpec((d, LANES), const), pl.BlockSpec((1, LANES), const)],
        out_specs=[pl.BlockSpec((tm, d), row), pl.BlockSpec((tm, d // 2), row),
                   pl.BlockSpec((tm, LANES), row), pl.BlockSpec((tm, LANES), row),
                   pl.BlockSpec((tm, LANES), row), pl.BlockSpec((8, LANES), const)],
        out_shape=[jax.ShapeDtypeStruct((n, d), F32), jax.ShapeDtypeStruct((n, d // 2), I32),
                   jax.ShapeDtypeStruct((n, LANES), I32), jax.ShapeDtypeStruct((n, LANES), I32),
                   jax.ShapeDtypeStruct((n, LANES), F32), jax.ShapeDtypeStruct((8, LANES), I32)],
        scratch_shapes=[pltpu.VMEM((8, LANES), F32)],
        compiler_params=_params(("arbitrary",)),
        name="ln1_router",
    )(x2d, mix, mod3, mod3, mod3, ln_g[None], ln_b[None], w_router_pad, b_router_pad)


def _dispatch_body(dest_ref, u_ref, xs_ref, sem):
    i = pl.program_id(0)
    tc = u_ref.shape[0]

    def start(j, cc):
        for kk in range(TOP_K):
            d = dest_ref[(i * tc + j) * TOP_K + kk]
            pltpu.make_async_copy(u_ref.at[pl.ds(j, 1)], xs_ref.at[pl.ds(d, 1)],
                                  sem.at[0]).start(priority=kk % 2)
        return cc

    lax.fori_loop(0, tc, start, 0, unroll=DMA_ISSUE_UNROLL)
    for _ in range(TOP_K):
        pltpu.make_async_copy(u_ref, xs_ref.at[pl.ds(0, tc)], sem.at[0]).wait()


def _dispatch(dest_flat, u_packed, n_slots):
    n, dw = u_packed.shape
    tc = _tile(n, 512, 8)
    grid_spec = pltpu.PrefetchScalarGridSpec(
        num_scalar_prefetch=1, grid=(n // tc,),
        in_specs=[pl.BlockSpec((tc, dw), lambda i, dr: (i, 0))],
        out_specs=pl.BlockSpec(memory_space=pl.ANY),
        scratch_shapes=[pltpu.SemaphoreType.DMA((1,))])
    return pl.pallas_call(
        _dispatch_body,
        grid_spec=grid_spec,
        out_shape=jax.ShapeDtypeStruct((n_slots, dw), I32),
        compiler_params=pltpu.CompilerParams(dimension_semantics=("arbitrary",),
                                             has_side_effects=True),
        name="moe_dispatch",
    )(dest_flat, u_packed)


def _ffn_body(te_ref, nsub_ref, nused_ref, x_ref, w1g_ref, w1u_ref, b1g_ref, b1u_ref,
              w2lo_ref, w2hi_ref, b2lo_ref, b2hi_ref, o_ref, h_ref, xbuf, xsem, *, nf, tf, ts, nr):
    i = pl.program_id(0)
    st = pl.program_id(1)
    _, tm, dh = xbuf.shape
    nsub = nsub_ref[i]
    slot = i % 2

    def x_copy(tile, to_slot):
        return pltpu.make_async_copy(x_ref.at[pl.ds(pl.multiple_of(tile * tm, tm), tm)],
                                     xbuf.at[to_slot], xsem.at[to_slot])

    @pl.when(jnp.logical_and(i == 0, st == 0))
    def _():
        x_copy(0, 0).start()

    @pl.when(st == 0)
    def _():
        x_copy(i, slot).wait()

    @pl.when(jnp.logical_and(st == 1, i + 1 < nused_ref[0]))
    def _():
        x_copy(i + 1, 1 - slot).start()

    def up_step(r0, rows):
        lo, hi = _unpack_bf16_pair(xbuf[slot, r0:r0 + rows, :])
        lo, hi = lo.astype(BF16), hi.astype(BF16)

        def proj(w_ref, b_ref):
            return (jnp.dot(lo, w_ref[0, 0:dh, :].astype(BF16), preferred_element_type=F32)
                    + jnp.dot(hi, w_ref[0, dh:2 * dh, :].astype(BF16), preferred_element_type=F32)
                    + b_ref[0])
        gate = jnp.minimum(proj(w1g_ref, b1g_ref), SWIGLU_LIMIT)
        up = jnp.clip(proj(w1u_ref, b1u_ref), -SWIGLU_LIMIT, SWIGLU_LIMIT)
        act = (up + 1.0) * gate * jax.nn.sigmoid(SWIGLU_ALPHA * gate)
        h_ref[st, r0:r0 + rows, :] = act.astype(BF16)

    def down_step(r0, rows, block_rows):
        def proj(w_ref, b_ref):
            acc = jnp.broadcast_to(b_ref[0], (rows, o_ref.shape[1])).astype(F32)
            for f in range(nf):
                acc = acc + jnp.dot(h_ref[f, r0:r0 + rows, :],
                                    w_ref[0, f * tf:(f + 1) * tf, :].astype(BF16),
                                    preferred_element_type=F32)
            return acc
        if rows:
            o_ref[r0:r0 + rows, :] = _pack_bf16_pair(proj(w2lo_ref, b2lo_ref),
                                                     proj(w2hi_ref, b2hi_ref))
        if rows < block_rows:
            o_ref[r0 + rows:r0 + block_rows, :] = jnp.zeros((block_rows - rows, o_ref.shape[1]),
                                                            I32)

    for blk in range(tm // (nr * ts)):
        r0 = blk * nr * ts
        nsub_b = jnp.clip(nsub - blk * nr, 0, nr)
        for r in range(nr + 1):
            if r:
                @pl.when(jnp.logical_and(nsub_b == r, st < nf))
                def _(r=r, r0=r0):
                    up_step(r0, r * ts)

            @pl.when(jnp.logical_and(nsub_b == r, st >= nf))
            def _(r=r, r0=r0):
                down_step(r0, r * ts, nr * ts)


def _expert_ffn(xs, w1, b1, w2, b2, tile_e, tile_nsub, n_used, *, tm, ts, n_tiles):
    ne, d, de2 = w1.shape
    de = de2 // 2
    dh = d // 2
    tf = _tile(de, 256)
    td = _tile(dh, 256)
    nf, nd = de // tf, dh // td
    nstep = nf + nd
    b1r = b1.reshape(ne, 1, de2)
    b2r = b2.reshape(ne, 1, d)

    def tile_idx(i, nu):
        return jnp.maximum(jnp.minimum(i, nu[0] - 1), 0)

    def step_idx(i, s, nu):
        return jnp.where(i < nu[0], s, nstep - 1)

    def up_map(off):
        return lambda i, s, te, ns, nu: (te[tile_idx(i, nu)], 0,
                                         off + jnp.minimum(step_idx(i, s, nu), nf - 1))

    def down_map(off):
        return lambda i, s, te, ns, nu: (te[tile_idx(i, nu)], 0,
                                         off + jnp.maximum(step_idx(i, s, nu) - nf, 0))

    grid_spec = pltpu.PrefetchScalarGridSpec(
        num_scalar_prefetch=3,
        grid=(n_used[0], nstep),
        in_specs=[pl.BlockSpec(memory_space=pl.ANY),
                  pl.BlockSpec((1, d, tf), up_map(0)), pl.BlockSpec((1, d, tf), up_map(nf)),
                  pl.BlockSpec((1, 1, tf), up_map(0)), pl.BlockSpec((1, 1, tf), up_map(nf)),
                  pl.BlockSpec((1, de, td), down_map(0)), pl.BlockSpec((1, de, td), down_map(nd)),
                  pl.BlockSpec((1, 1, td), down_map(0)), pl.BlockSpec((1, 1, td), down_map(nd))],
        out_specs=pl.BlockSpec((tm, td), lambda i, s, te, ns, nu: (
            tile_idx(i, nu), jnp.maximum(step_idx(i, s, nu) - nf, 0))),
        scratch_shapes=[pltpu.VMEM((nf, tm, tf), BF16), pltpu.VMEM((2, tm, dh), I32),
                        pltpu.SemaphoreType.DMA((2,))],
    )
    return pl.pallas_call(
        functools.partial(_ffn_body, nf=nf, tf=tf, ts=ts, nr=MOE_SUBTILES_PER_BLOCK),
        grid_spec=grid_spec,
        out_shape=jax.ShapeDtypeStruct((n_tiles * tm, dh), I32),
        compiler_params=_params(("arbitrary", "arbitrary")),
        name="expert_ffn",
    )(tile_e, tile_nsub, n_used, xs, w1, w1, b1r, b1r, w2, w2, b2r, b2r)


def _combine_body(dest_ref, y_ref, gate_ref, x1_ref, gm_ref, lg_ref, lb_ref, o_ref, ybuf, sem,
                  *, alpha_res):
    i = pl.program_id(0)
    nsteps = pl.num_programs(0)
    tc = x1_ref.shape[0]

    def gather_tile(tile, slot):
        def start(j, cc):
            for kk in range(TOP_K):
                d = dest_ref[(tile * tc + j) * TOP_K + kk]
                pltpu.make_async_copy(y_ref.at[pl.ds(d, 1)], ybuf.at[slot, kk, pl.ds(j, 1)],
                                      sem.at[slot]).start(priority=kk % 2)
            return cc
        lax.fori_loop(0, tc, start, 0, unroll=DMA_ISSUE_UNROLL)

    slot = i % 2

    @pl.when(i == 0)
    def _():
        gather_tile(0, 0)

    @pl.when(i + 1 < nsteps)
    def _():
        gather_tile(i + 1, 1 - slot)

    for kk in range(TOP_K):
        pltpu.make_async_copy(y_ref.at[pl.ds(0, tc)], ybuf.at[slot, kk], sem.at[slot]).wait()
    g = gate_ref[...]
    y_lo = jnp.zeros((tc, ybuf.shape[3]), F32)
    y_hi = jnp.zeros((tc, ybuf.shape[3]), F32)
    for kk in range(TOP_K):
        lo, hi = _unpack_bf16_pair(ybuf[slot, kk])
        y_lo = y_lo + g[:, kk:kk + 1] * lo
        y_hi = y_hi + g[:, kk:kk + 1] * hi
    y = jnp.concatenate([y_lo, y_hi], axis=1)
    o_ref[...] = _layer_norm(alpha_res * x1_ref[...] + gm_ref[0] * y, lg_ref[...], lb_ref[...])


def _combine(dest_flat, y_sorted, gate, x1, mod3, ln_g, ln_b, s, alpha_res):
    n, d = x1.shape
    tc = _tile(s, 128, 8)
    per_b = s // tc
    row = lambda i, dr: (i, 0)
    const = lambda i, dr: (0, 0)
    grid_spec = pltpu.PrefetchScalarGridSpec(
        num_scalar_prefetch=1, grid=(n // tc,),
        in_specs=[pl.BlockSpec(memory_space=pl.ANY),
                  pl.BlockSpec((tc, LANES), row), pl.BlockSpec((tc, d), row),
                  pl.BlockSpec((1, 1, d), lambda i, dr: (i // per_b, 0, 5)),
                  pl.BlockSpec((1, d), const), pl.BlockSpec((1, d), const)],
        out_specs=pl.BlockSpec((tc, d), row),
        scratch_shapes=[pltpu.VMEM((2, TOP_K, tc, d // 2), I32), pltpu.SemaphoreType.DMA((2,))])
    return pl.pallas_call(
        functools.partial(_combine_body, alpha_res=alpha_res),
        grid_spec=grid_spec,
        out_shape=jax.ShapeDtypeStruct((n, d), F32),
        compiler_params=_params(("arbitrary",)),
        name="moe_combine",
    )(dest_flat, y_sorted, gate, x1, mod3, ln_g[None], ln_b[None])


MOE_SUBTILES_PER_BLOCK = 5
MOE_BLOCKS_PER_TILE = 1


def _moe_tiles(n_tok, n_exp):
    mean_load = n_tok * TOP_K // n_exp
    ts = 16
    while ts * 2 <= min(256, mean_load // (4 * MOE_BLOCKS_PER_TILE)):
        ts *= 2
    return ts, MOE_BLOCKS_PER_TILE * MOE_SUBTILES_PER_BLOCK * ts


def _layer(x, c_pad, w_ada, b_ada, w_in, b_forget, q_norm_g, kv_norm_g, kidx_ln_g, kidx_ln_b,
           w_uq, w_uk, w_uv, w_iq, fox_out_g, dsa_out_g, w_o, ln1_g, ln1_b,
           w_router, b_router, w1, b1, w2, b2, ln2_g, ln2_b, rel_bias, alpha_res):
    b, s, d = x.shape
    n = b * s
    fh = b_forget.shape[0]
    ql = q_norm_g.shape[0]
    kvl = kv_norm_g.shape[0]
    dh = w_uk.shape[1]
    ih = w_iq.shape[1] // IDX_DIM
    ne = w_router.shape[1]
    fw = fh * HEAD_DIM
    dw = dh * HEAD_DIM
    assert fw == dw, "head groups share the attention kernel's column-block width"
    assert fh + ih <= LANES and ne <= LANES

    mod = _matmul(c_pad, w_ada, 6 * d, F32, bias=b_ada[None], a_silu=True,
                  tm=16, tn=512, tk=d, name="adaln")
    mod3 = mod[:b].reshape(b, 1, 6 * d)
    u = _modulate(x, mod3, 0, 1).reshape(n, d)

    w_in_t = jnp.swapaxes(w_in, 0, 1)
    q_scale = HEAD_DIM ** -0.5 * LOG2E
    qk = _matmul(u, w_in_t, 2 * fw, BF16, w_t=True, scale_cols=(fw, q_scale), **WIDE_TILES,
                 name="proj_qk")
    tail_w = -(-(ql + kvl + IDX_DIM + fh + ih) // LANES) * LANES
    tail = _matmul(u, w_in_t, tail_w, F32, w_t=True, col0=3 * fw, tm=2048, name="proj_tail")
    cqn, ckvn, kin, small, cum = _tail_prep(tail, b, s, ql, kvl, fh, ih, q_norm_g, kv_norm_g,
                                            kidx_ln_g, kidx_ln_b, b_forget)
    v_t = _matmul(w_in_t, u, n, BF16, w_t=True, a_rows=(2 * fw, fw), **WIDE_TILES,
                  name="proj_vt")

    t = 256 if s % 256 == 0 else 128
    fox = _attention("fox", b, s, fh, t, (qk, 0), (qk, 1), v_t, fox_out_g, cum=cum)

    q_d = _matmul(cqn, w_uq, dw, BF16, scale_cols=(dw, q_scale), **WIDE_TILES, name="proj_qd")
    q_i = _matmul(cqn, w_iq, ih * IDX_DIM, BF16, **WIDE_TILES, name="proj_qi")
    k_d = _matmul(ckvn, w_uk.reshape(kvl, dw), dw, BF16, **WIDE_TILES, name="expand_k")
    vd_t = _matmul(jnp.swapaxes(w_uv.reshape(kvl, dw), 0, 1), ckvn, n, BF16, w_t=True,
                   **WIDE_TILES, name="expand_vt")
    k_sel = min(TOPK_MAX, s // 4)
    mask = _indexer(q_i, kin, small, b, s, ih, fh, t, k_sel)
    bias_t = _bias_tiles(rel_bias, dh, t)
    dsa = _attention("dsa", b, s, dh, t, (q_d, 0), (k_d, 0), vd_t, dsa_out_g,
                     mask=mask, bias_tiles=bias_t)

    mix = _matmul(fox, w_o, d, BF16, a2=dsa, **WIDE_TILES, name="proj_out")

    wr_pad = jnp.zeros((d, LANES), F32).at[:, :ne].set(w_router)
    br_pad = jnp.zeros((1, LANES), F32).at[0, :ne].set(b_router)
    x1, u_packed, eidx, rank, gate, cnt = _router(x.reshape(n, d), mix, mod3, ln1_g, ln1_b,
                                                  wr_pad, br_pad, s, ne, alpha_res)
    ts, tm = _moe_tiles(n, ne)
    n_tiles = -(-(n * TOP_K) // tm) + ne
    counts = cnt[0, :ne]
    ntile = (counts + tm - 1) // tm
    tend = jnp.cumsum(ntile)
    tstart = tend - ntile
    n_used = tend[-1]
    onehot = eidx[:, :TOP_K, None] == jnp.arange(ne, dtype=I32)
    dest = jnp.sum(jnp.where(onehot, tstart * tm, 0), axis=-1) + rank[:, :TOP_K]
    dest_flat = dest.reshape(n * TOP_K).astype(I32)
    tid = jnp.minimum(jnp.arange(n_tiles, dtype=I32), n_used - 1)
    tile_e = jnp.minimum(jnp.searchsorted(tend, tid, side="right"), ne - 1).astype(I32)
    valid = jnp.clip(counts[tile_e] - (tid - tstart[tile_e]) * tm, 0, tm)
    valid = jnp.where(jnp.arange(n_tiles) < n_used, valid, 0)
    tile_nsub = ((valid + ts - 1) // ts).astype(I32)
    xs = _dispatch(dest_flat, u_packed, n_tiles * tm)
    y_sorted = _expert_ffn(xs, w1, b1, w2, b2, tile_e, tile_nsub,
                           n_used.reshape(1).astype(I32), tm=tm, ts=ts, n_tiles=n_tiles)
    out = _combine(dest_flat, y_sorted, gate, x1, mod3, ln2_g, ln2_b, s, alpha_res)
    return out.reshape(b, s, d)


def kernel(x, c, w_ada, b_ada, w_in, b_forget, q_norm_g, kv_norm_g, kidx_ln_g, kidx_ln_b, w_uq, w_uk, w_uv, w_iq, fox_out_g, dsa_out_g, w_o, ln1_g, ln1_b, w_router, b_router, w1, b1, w2, b2, ln2_g, ln2_b, rel_bias):
    depth = w_ada.shape[0]
    alpha_res = (2 * depth) ** 0.25
    b, d = c.shape
    c_pad = jnp.zeros((16, d), F32).at[:b].set(c)
    for l in range(depth):
        x = _layer(x, c_pad, w_ada[l], b_ada[l], w_in[l], b_forget[l], q_norm_g[l], kv_norm_g[l],
                   kidx_ln_g[l], kidx_ln_b[l], w_uq[l], w_uk[l], w_uv[l], w_iq[l], fox_out_g[l],
                   dsa_out_g[l], w_o[l], ln1_g[l], ln1_b[l], w_router[l], b_router[l],
                   w1[l], b1[l], w2[l], b2[l], ln2_g[l], ln2_b[l], rel_bias, alpha_res)
    return x
```
